```python
import jax, jax.numpy as jnp
from jax import lax
import numpy as np

D_MODEL = 2048
BATCH = 4
SEQ = 2048
DEPTH = 1
DEC_BATCH = 128
DEC_SEQ = 8
PAST_LEN = 16384
PAGE_SIZE = 128

DN_HEADS = 8
DN_DK = 128
DN_DV = 128
DN_WIDTH = DN_HEADS * DN_DV
QK_WIDTH = DN_HEADS * DN_DK
QKV_DIM = 2 * QK_WIDTH + DN_WIDTH
DN_CHUNK = 64
CONV_W = 4
MLP_WIDTH = D_MODEL - DN_WIDTH
MLP_GROUPS = 4
MLP_GROUP_DIM = MLP_WIDTH // MLP_GROUPS
MLP_CHUNK = 128
DN_IN = QKV_DIM + DN_WIDTH + 2 * DN_HEADS
IN_DIM = DN_IN + 2 * MLP_WIDTH
D_FF = 5632
N_SUB = 3
DEEPNORM_ALPHA = (2.0 * DEPTH) ** 0.25
DEEPNORM_BETA = (8.0 * DEPTH) ** -0.25
LN_EPS = 1e-5
RMS_EPS = 1e-6

kernel_name = 'hybrid_gdn_chunkmlp_macaron_step'


def _layer_norm(x, g, b):
    xf = x.astype(jnp.float32)
    mu = jnp.mean(xf, -1, keepdims=True)
    var = jnp.mean(jnp.square(xf - mu), -1, keepdims=True)
    return ((xf - mu) * lax.rsqrt(var + LN_EPS) * g.astype(jnp.float32) + b.astype(jnp.float32)).astype(x.dtype)


def _rms_norm_f32(xf, g):
    return xf * lax.rsqrt(jnp.mean(jnp.square(xf), -1, keepdims=True) + RMS_EPS) * g.astype(jnp.float32)


def _l2norm(x):
    xf = x.astype(jnp.float32)
    return xf * lax.rsqrt(jnp.sum(xf * xf, -1, keepdims=True) + RMS_EPS)


def _swiglu(h, wg, wu, wd):
    return (jax.nn.silu(h @ wg) * (h @ wu)) @ wd


def _causal_conv(x_raw, buf, w):
    L = x_raw.shape[1]
    xcat = jnp.concatenate([buf.astype(x_raw.dtype), x_raw], axis=1)
    y = w[0] * xcat[:, 0:L]
    for j in range(1, CONV_W):
        y = y + w[j] * xcat[:, j:j + L]
    return jax.nn.silu(y), xcat[:, L:]


def _gated_delta_chunked(q, k, v, g, beta, s0, chunk):
    B, L, H, DK = q.shape
    DV = v.shape[-1]
    N = L // chunk

    def blocks(t):
        t = t.astype(jnp.float32).reshape((B, N, chunk, H) + t.shape[3:])
        return jnp.moveaxis(t, (1, 3), (0, 2))

    qb, kb, vb, gb, bb = blocks(q), blocks(k), blocks(v), blocks(g), blocks(beta)
    gc = jnp.cumsum(gb, axis=-1)
    idx = jnp.arange(chunk)
    incl = idx[:, None] >= idx[None, :]
    strict = idx[:, None] > idx[None, :]
    decay = jnp.exp(jnp.where(incl, gc[..., :, None] - gc[..., None, :], -jnp.inf))
    kbeta = kb * bb[..., None]
    m = jnp.einsum('nbhid,nbhjd->nbhij', kbeta, kb) * jnp.where(strict, decay, 0.0)
    eye = jnp.eye(chunk, dtype=jnp.float32)
    t_inv = lax.linalg.triangular_solve(eye + m, jnp.broadcast_to(eye, m.shape),
                                        left_side=True, lower=True, unit_diagonal=True)
    u = jnp.einsum('nbhij,nbhje->nbhie', t_inv, vb * bb[..., None])
    w = jnp.einsum('nbhij,nbhjd->nbhid', t_inv, kbeta * jnp.exp(gc)[..., None])
    qk = jnp.einsum('nbhid,nbhjd->nbhij', qb, kb) * decay
    q_dec = qb * jnp.exp(gc)[..., None]
    k_dec = kb * jnp.exp(gc[..., -1:] - gc)[..., None]
    g_last = jnp.exp(gc[..., -1])

    def step(s, xs):
        u_n, w_n, qk_n, qd_n, kd_n, gl_n = xs
        v_new = u_n - jnp.einsum('bhcd,bhde->bhce', w_n, s)
        o_n = jnp.einsum('bhcd,bhde->bhce', qd_n, s) + jnp.einsum('bhij,bhje->bhie', qk_n, v_new)
        s = s * gl_n[..., None, None] + jnp.einsum('bhcd,bhce->bhde', kd_n, v_new)
        return s, o_n

    s_fin, o = lax.scan(step, s0, (u, w, qk, q_dec, k_dec, g_last))
    o = jnp.moveaxis(o, (0, 2), (1, 3)).reshape(B, L, H, DV)
    return o, s_fin


def _gated_deltanet(p, conv_buf, s0, conv_w, a_log, dt_bias, dn_norm_g):
    B, L, _ = p.shape
    qkv_raw = p[..., :QKV_DIM]
    z = p[..., QKV_DIM:QKV_DIM + DN_WIDTH]
    b_raw = p[..., QKV_DIM + DN_WIDTH:QKV_DIM + DN_WIDTH + DN_HEADS]
    a_raw = p[..., QKV_DIM + DN_WIDTH + DN_HEADS:DN_IN]
    qkv, new_buf = _causal_conv(qkv_raw, conv_buf, conv_w)
    q = _l2norm(qkv[..., :QK_WIDTH].reshape(B, L, DN_HEADS, DN_DK)) * (DN_DK ** -0.5)
    k = _l2norm(qkv[..., QK_WIDTH:2 * QK_WIDTH].reshape(B, L, DN_HEADS, DN_DK))
    v = qkv[..., 2 * QK_WIDTH:].reshape(B, L, DN_HEADS, DN_DV).astype(jnp.float32)
    beta = jax.nn.sigmoid(b_raw.astype(jnp.float32))
    g = -jnp.exp(a_log.astype(jnp.float32)) * jax.nn.softplus(a_raw.astype(jnp.float32) + dt_bias.astype(jnp.float32))
    chunk = DN_CHUNK if L % DN_CHUNK == 0 else L
    o, s_new = _gated_delta_chunked(q, k, v, g, beta, s0.astype(jnp.float32), chunk)
    o = _rms_norm_f32(o, dn_norm_g) * jax.nn.silu(z.astype(jnp.float32).reshape(B, L, DN_HEADS, DN_DV))
    return o.reshape(B, L, DN_WIDTH).astype(p.dtype), new_buf, s_new.astype(s0.dtype)


def _chunk_mlp(p, ln_g, ln_b, w_spatial, b_spatial):
    B, L, _ = p.shape
    u = jax.nn.gelu(p[..., :MLP_WIDTH])
    v = jax.nn.gelu(p[..., MLP_WIDTH:]).reshape(B, L, MLP_GROUPS, MLP_GROUP_DIM)
    v = _layer_norm(v, ln_g, ln_b)
    c = min(L, MLP_CHUNK)
    n = L // c
    ws = w_spatial[:, :c, :c] * jnp.tril(jnp.ones((c, c), w_spatial.dtype))
    vb = v.reshape(B, n, c, MLP_GROUPS, MLP_GROUP_DIM)
    s = jnp.einsum('gij,bnjgd->bnigd', ws, vb) + b_spatial[:, :c].T[None, None, :, :, None]
    out = u.reshape(B, n, c, MLP_GROUPS, MLP_GROUP_DIM) * s
    return out.reshape(B, L, MLP_WIDTH), v.reshape(B, L, MLP_WIDTH)


def _trunk_layer(x, c, conv_buf, s0, w_ada, b_ada, ln_g, ln_b, ffn_wg, ffn_wu, ffn_wd,
                 w_in, conv_w, a_log, dt_bias, dn_norm_g, mlp_ln_g, mlp_ln_b,
                 w_spatial, b_spatial, w_out):
    Bc = c.shape[0]
    mod = (jax.nn.silu(c) @ w_ada + b_ada).reshape(Bc, N_SUB, 3, D_MODEL)[:, :, :, None, :]
    h = x * (1 + mod[:, 0, 1]) + mod[:, 0, 0]
    x = _layer_norm(DEEPNORM_ALPHA * x + 0.5 * mod[:, 0, 2] * _swiglu(h, ffn_wg[0], ffn_wu[0], ffn_wd[0]), ln_g[0], ln_b[0])
    h = x * (1 + mod[:, 1, 1]) + mod[:, 1, 0]
    proj = h @ w_in
    o_dn, new_buf, s_new = _gated_deltanet(proj[..., :DN_IN], conv_buf, s0, conv_w, a_log, dt_bias, dn_norm_g)
    o_mlp, v_rows = _chunk_mlp(proj[..., DN_IN:], mlp_ln_g, mlp_ln_b, w_spatial, b_spatial)
    mix = jnp.concatenate([o_dn, o_mlp], axis=-1) @ w_out
    x = _layer_norm(DEEPNORM_ALPHA * x + mod[:, 1, 2] * mix, ln_g[1], ln_b[1])
    h = x * (1 + mod[:, 2, 1]) + mod[:, 2, 0]
    x = _layer_norm(DEEPNORM_ALPHA * x + 0.5 * mod[:, 2, 2] * _swiglu(h, ffn_wg[1], ffn_wu[1], ffn_wd[1]), ln_g[2], ln_b[2])
    return x, new_buf, s_new, v_rows


def setup_inputs(seed: int = 0) -> dict:
    key = jax.random.key(seed)
    ks = jax.random.split(key, 24)
    f32 = jnp.float32
    nrm = lambda k, shape, s: jax.random.normal(k, shape, f32) * s
    dt = jnp.exp(jax.random.uniform(ks[13], (DEPTH, DN_HEADS), f32, float(np.log(1e-3)), float(np.log(1e-1))))
    return {
        'x_prompt': nrm(ks[0], (BATCH, SEQ, D_MODEL), 1.0),
        'x_sample': nrm(ks[1], (DEC_BATCH, DEC_SEQ, D_MODEL), 1.0),
        'c_prompt': nrm(ks[2], (BATCH, D_MODEL), 1.0),
        'c_sample': nrm(ks[3], (DEC_BATCH, D_MODEL), 1.0),
        'state_delta': nrm(ks[4], (DEPTH, DEC_BATCH, DN_HEADS, DN_DK, DN_DV), 0.1),
        'state_conv': nrm(ks[5], (DEPTH, DEC_BATCH, CONV_W - 1, QKV_DIM), 1.0),
        'w_ada': nrm(ks[6], (DEPTH, D_MODEL, N_SUB * 3 * D_MODEL), 0.5 * D_MODEL ** -0.5),
        'b_ada': nrm(ks[7], (DEPTH, N_SUB * 3 * D_MODEL), 0.01),
        'ln_g': 1.0 + nrm(ks[8], (DEPTH, N_SUB, D_MODEL), 0.01),
        'ln_b': nrm(ks[9], (DEPTH, N_SUB, D_MODEL), 0.01),
        'ffn_wg': nrm(ks[10], (DEPTH, 2, D_MODEL, D_FF), D_MODEL ** -0.5),
        'ffn_wu': nrm(ks[11], (DEPTH, 2, D_MODEL, D_FF), D_MODEL ** -0.5),
        'ffn_wd': nrm(ks[12], (DEPTH, 2, D_FF, D_MODEL), DEEPNORM_BETA * D_FF ** -0.5),
        'w_in': nrm(ks[14], (DEPTH, D_MODEL, IN_DIM), D_MODEL ** -0.5),
        'conv_w': nrm(ks[15], (DEPTH, CONV_W, QKV_DIM), CONV_W ** -0.5),
        'a_log': jnp.log(jax.random.uniform(ks[16], (DEPTH, DN_HEADS), f32, 1.0, 16.0)),
        'dt_bias': dt + jnp.log(-jnp.expm1(-dt)),
        'dn_norm_g': 1.0 + nrm(ks[17], (DEPTH, DN_DV), 0.01),
        'mlp_ln_g': 1.0 + nrm(ks[18], (DEPTH, MLP_GROUPS, MLP_GROUP_DIM), 0.01),
        'mlp_ln_b': nrm(ks[19], (DEPTH, MLP_GROUPS, MLP_GROUP_DIM), 0.01),
        'w_spatial': nrm(ks[20], (DEPTH, MLP_GROUPS, MLP_CHUNK, MLP_CHUNK), 0.5 * MLP_CHUNK ** -0.5),
        'b_spatial': 1.0 + nrm(ks[21], (DEPTH, MLP_GROUPS, MLP_CHUNK), 0.01),
        'w_out': nrm(ks[22], (DEPTH, D_MODEL, D_MODEL), DEEPNORM_BETA * D_MODEL ** -0.5),
    }


def reference(x_prompt, x_sample, c_prompt, c_sample, state_delta, state_conv,
              w_ada, b_ada, ln_g, ln_b, ffn_wg, ffn_wu, ffn_wd, w_in, conv_w,
              a_log, dt_bias, dn_norm_g, mlp_ln_g, mlp_ln_b, w_spatial, b_spatial, w_out):
    bp = x_prompt.shape[0]
    conv0 = jnp.zeros((bp, CONV_W - 1, QKV_DIM), x_prompt.dtype)
    s0 = jnp.zeros((bp, DN_HEADS, DN_DK, DN_DV), x_prompt.dtype)
    y_p, y_s = x_prompt, x_sample
    delta_p, conv_p, delta_s, conv_s, vrows_s = [], [], [], [], []
    for layer in range(DEPTH):
        wts = (w_ada[layer], b_ada[layer], ln_g[layer], ln_b[layer], ffn_wg[layer], ffn_wu[layer],
               ffn_wd[layer], w_in[layer], conv_w[layer], a_log[layer], dt_bias[layer],
               dn_norm_g[layer], mlp_ln_g[layer], mlp_ln_b[layer], w_spatial[layer],
               b_spatial[layer], w_out[layer])
        y_p, cb_p, ds_p, _ = _trunk_layer(y_p, c_prompt, conv0, s0, *wts)
        y_s, cb_s, ds_s, vr_s = _trunk_layer(y_s, c_sample, state_conv[layer], state_delta[layer], *wts)
        delta_p.append(ds_p)
        conv_p.append(cb_p)
        delta_s.append(ds_s)
        conv_s.append(cb_s)
        vrows_s.append(vr_s)
    return (y_p, y_s, jnp.stack(delta_p), jnp.stack(conv_p), jnp.stack(delta_s), jnp.stack(conv_s), jnp.stack(vrows_s))
```

```python
import functools
import math

import jax
import jax.numpy as jnp
from jax import lax
from jax.experimental import pallas as pl
from jax.experimental.pallas import tpu as pltpu

F32 = jnp.float32
BF16 = jnp.bfloat16

DN_HEADS = 8
DN_CHUNK = 64
CONV_W = 4
MLP_GROUPS = 4
MLP_CHUNK = 128
N_SUB = 3
LN_EPS = 1e-5
RMS_EPS = 1e-6

LANES = 128
SUBLANES = 8
MIB = 2 ** 20

FFN_TM = 1024
FFN_TF = 512
PROJ_TN = 896
MIX_TM = 512
ADA_TN = 1024
SEQ_TILE = 128


def _dot(a, b, precision=None):
    return jnp.dot(a, b, preferred_element_type=F32, precision=precision)


def _dot_nt(a, b, precision=None):
    return lax.dot_general(a, b, (((1,), (1,)), ((), ())),
                           preferred_element_type=F32, precision=precision)


def _dot_tn(a, b):
    return lax.dot_general(a, b, (((0,), (0,)), ((), ())), preferred_element_type=F32)


def _layer_norm(y, g, b):
    mu = jnp.mean(y, axis=-1, keepdims=True)
    yc = y - mu
    var = jnp.mean(yc * yc, axis=-1, keepdims=True)
    return yc * lax.rsqrt(var + LN_EPS) * g + b


def _silu(x):
    return x * jax.nn.sigmoid(x)


def _mod_rows(ref, seq_tiles):
    if seq_tiles:
        return ref[pl.ds(pl.program_id(0) // seq_tiles, 1), :]
    return ref[...]


def _ada_body(c_ref, w_ref, b_ref, ms_ref, mp_ref):
    c = c_ref[...]
    a = _silu(c).astype(BF16)
    y = _dot(a, w_ref[...].astype(BF16)) + b_ref[...]
    ns = ms_ref.shape[0]
    ms_ref[...] = y[:ns]
    mp_ref[...] = y[ns:]


def _ada(c_all, w_ada, b_ada, n_sample):
    rows, d = c_all.shape
    n = w_ada.shape[1]
    return pl.pallas_call(
        _ada_body,
        name="ada_mod",
        grid=(n // ADA_TN,),
        in_specs=[
            pl.BlockSpec((rows, d), lambda j: (0, 0)),
            pl.BlockSpec((d, ADA_TN), lambda j: (0, j)),
            pl.BlockSpec((1, ADA_TN), lambda j: (0, j)),
        ],
        out_specs=[
            pl.BlockSpec((n_sample, ADA_TN), lambda j: (0, j)),
            pl.BlockSpec((rows - n_sample, ADA_TN), lambda j: (0, j)),
        ],
        out_shape=[
            jax.ShapeDtypeStruct((n_sample, n), F32),
            jax.ShapeDtypeStruct((rows - n_sample, n), F32),
        ],
        compiler_params=pltpu.CompilerParams(
            dimension_semantics=("arbitrary",), vmem_limit_bytes=40 * MIB),
    )(c_all, w_ada, b_ada.reshape(1, n))


class _Group:
    def __init__(self, n_seq, n_rows, tm):
        if n_rows >= tm:
            assert n_rows % tm == 0
            self.sb, self.rb = 1, tm
        else:
            assert tm % n_rows == 0 and n_seq % (tm // n_rows) == 0
            self.sb, self.rb = tm // n_rows, n_rows
        self.n_seq, self.n_rows = n_seq, n_rows
        self.tiles_r = n_rows // self.rb
        self.tm = self.sb * self.rb
        self.n_tiles = (n_seq // self.sb) * self.tiles_r
        self.seq_tiles = self.tiles_r if self.sb == 1 else 0

    def x_spec(self, d, ngrid, **kw):
        tr = self.tiles_r
        if ngrid == 2:
            return pl.BlockSpec((self.sb, self.rb, d), lambda m, f: (m // tr, m % tr, 0), **kw)
        return pl.BlockSpec((self.sb, self.rb, d), lambda m: (m // tr, m % tr, 0), **kw)

    def mod_spec(self, mod, d, col, ngrid):
        if self.sb == 1:
            rows, row_blk = mod.shape[0], (lambda m: 0)
        else:
            rows, row_blk = self.sb, (lambda m: m)
        if ngrid == 2:
            return pl.BlockSpec((rows, d), lambda m, f: (row_blk(m), col))
        return pl.BlockSpec((rows, d), lambda m: (row_blk(m), col))


def _ffn_body(x_ref, sh_ref, sc_ref, gt_ref, wg_ref, wu_ref, wd_ref, lng_ref, lnb_ref,
              o_ref, h_sc, *, seq_tiles, alpha):
    f = pl.program_id(1)
    tm, d = o_ref.shape

    @pl.when(f == 0)
    def _():
        x = x_ref[...]
        h = x * (1.0 + _mod_rows(sc_ref, seq_tiles)[:, None, :]) + _mod_rows(sh_ref, seq_tiles)[:, None, :]
        h_sc[...] = h.reshape(tm, d).astype(BF16)
        o_ref[...] = jnp.zeros_like(o_ref)

    h = h_sc[...]
    g = _dot(h, wg_ref[...])
    u = _dot(h, wu_ref[...])
    a = (_silu(g) * u).astype(BF16)
    o_ref[...] += _dot(a, wd_ref[...])

    @pl.when(f == pl.num_programs(1) - 1)
    def _():
        x = x_ref[...]
        gate = _mod_rows(gt_ref, seq_tiles)[:, None, :]
        y = alpha * x + (0.5 * gate) * o_ref[...].reshape(x.shape)
        o_ref[...] = _layer_norm(y.reshape(tm, d), lng_ref[...], lnb_ref[...])


def _ffn(x3, mod, sub, wg, wu, wd, ln_g, ln_b, alpha):
    n_seq, n_rows, d = x3.shape
    dff = wg.shape[1]
    grp = _Group(n_seq, n_rows, FFN_TM)
    tm = grp.tm
    out = pl.pallas_call(
        functools.partial(_ffn_body, seq_tiles=grp.seq_tiles, alpha=alpha),
        name="swiglu_ln",
        grid=(grp.n_tiles, dff // FFN_TF),
        in_specs=[
            grp.x_spec(d, 2, pipeline_mode=pl.Buffered(1)),
            grp.mod_spec(mod, d, sub * 3 + 0, 2),
            grp.mod_spec(mod, d, sub * 3 + 1, 2),
            grp.mod_spec(mod, d, sub * 3 + 2, 2),
            pl.BlockSpec((d, FFN_TF), lambda m, f: (0, f)),
            pl.BlockSpec((d, FFN_TF), lambda m, f: (0, f)),
            pl.BlockSpec((FFN_TF, d), lambda m, f: (f, 0)),
            pl.BlockSpec((1, d), lambda m, f: (0, 0)),
            pl.BlockSpec((1, d), lambda m, f: (0, 0)),
        ],
        out_specs=pl.BlockSpec((tm, d), lambda m, f: (m, 0)),
        out_shape=jax.ShapeDtypeStruct((n_seq * n_rows, d), F32),
        scratch_shapes=[pltpu.VMEM((tm, d), BF16)],
        compiler_params=pltpu.CompilerParams(
            dimension_semantics=("arbitrary", "arbitrary"), vmem_limit_bytes=56 * MIB),
    )(x3, mod, mod, mod, wg, wu, wd, ln_g.reshape(1, d), ln_b.reshape(1, d))
    return out.reshape(n_seq, n_rows, d)


def _proj_body(x_ref, sh_ref, sc_ref, w_ref, o_ref, h_sc, *, seq_tiles):
    tm, d = h_sc.shape

    @pl.when(pl.program_id(1) == 0)
    def _():
        x = x_ref[...]
        h = x * (1.0 + _mod_rows(sc_ref, seq_tiles)[:, None, :]) + _mod_rows(sh_ref, seq_tiles)[:, None, :]
        h_sc[...] = h.reshape(tm, d).astype(BF16)

    o_ref[...] = _dot(h_sc[...], w_ref[...])


def _proj(x3, mod, sub, w_cat):
    n_seq, n_rows, d = x3.shape
    n = w_cat.shape[1]
    grp = _Group(n_seq, n_rows, FFN_TM)
    tm = grp.tm
    return pl.pallas_call(
        functools.partial(_proj_body, seq_tiles=grp.seq_tiles),
        name="in_proj",
        grid=(grp.n_tiles, n // PROJ_TN),
        in_specs=[
            grp.x_spec(d, 2, pipeline_mode=pl.Buffered(1)),
            grp.mod_spec(mod, d, sub * 3 + 0, 2),
            grp.mod_spec(mod, d, sub * 3 + 1, 2),
            pl.BlockSpec((d, PROJ_TN), lambda m, j: (0, j)),
        ],
        out_specs=pl.BlockSpec((tm, PROJ_TN), lambda m, j: (m, j)),
        out_shape=jax.ShapeDtypeStruct((n_seq * n_rows, n), F32),
        scratch_shapes=[pltpu.VMEM((tm, d), BF16)],
        compiler_params=pltpu.CompilerParams(
            dimension_semantics=("arbitrary", "arbitrary"), vmem_limit_bytes=48 * MIB),
    )(x3, mod, mod, w_cat)


def _block_masks(n, blk):
    shift = int(math.log2(blk))
    i = lax.broadcasted_iota(jnp.int32, (n, n), 0)
    j = lax.broadcasted_iota(jnp.int32, (n, n), 1)
    same = (i >> shift) == (j >> shift)
    return i, j, same


def _gdn_body(*refs, blk, chained, tiles_per_seq, dk_scale):
    if chained:
        (qkv_ref, z_ref, ba_ref, cw_ref, gp_ref, ng_ref, o_ref, st_ref, xc_sc, act_sc) = refs
    else:
        (qkv_ref, z_ref, ba_ref, buf_ref, sin_ref, cw_ref, gp_ref, ng_ref, o_ref, st_ref,
         xc_sc, bs_sc, act_sc) = refs
    lt, cqkv = qkv_ref.shape
    hd = LANES
    nh = cqkv // (3 * hd)
    t = pl.program_id(0)
    halo = SUBLANES

    if chained:
        first = (t % tiles_per_seq) == 0

        @pl.when(first)
        def _():
            xc_sc[pl.ds(0, halo), :] = jnp.zeros((halo, cqkv), F32)
            st_ref[...] = jnp.zeros_like(st_ref)

        @pl.when(jnp.logical_not(first))
        def _():
            xc_sc[pl.ds(0, halo), :] = xc_sc[pl.ds(lt, halo), :]
    else:
        xc_sc[pl.ds(0, halo), :] = jnp.zeros((halo, cqkv), F32)
        bs_sc[pl.ds(0, lt), :] = buf_ref[...]
        bs_sc[pl.ds(lt, halo), :] = jnp.zeros((halo, cqkv), F32)
    xc_sc[pl.ds(halo, lt), :] = qkv_ref[...]

    cblk = 2 * LANES
    row_in_seq = lax.broadcasted_iota(jnp.int32, (lt, cblk), 0) & (blk - 1)
    for c0 in range(0, cqkv, cblk):
        cs = slice(c0, c0 + cblk)
        y = None
        for j in range(CONV_W):
            d = CONV_W - 1 - j
            term = xc_sc[pl.ds(halo - d, lt), cs]
            if not chained and d > 0:
                term = jnp.where(row_in_seq >= d, term, bs_sc[pl.ds(halo - d, lt), cs])
            term = cw_ref[j:j + 1, cs] * term
            y = term if y is None else y + term
        act_sc[:, cs] = _silu(y)

    ba = ba_ref[...]
    beta_all = jax.nn.sigmoid(ba)
    g_all = -jnp.exp(gp_ref[0:1, :]) * jax.nn.softplus(ba + gp_ref[1:2, :])

    i, j, same = _block_masks(lt, blk)
    incl = same & (i >= j)
    strict = same & (i > j)
    hi = lax.Precision.HIGHEST
    gc_all = _dot(incl.astype(F32), g_all, hi)
    gl_all = _dot((j == (i | (blk - 1))).astype(F32), gc_all, hi)
    gc_t = gc_all.T
    eye = (i == j).astype(F32)
    pair_masks = []
    for lvl in range(int(math.log2(blk))):
        pair_masks.append(((i >> (lvl + 1)) == (j >> (lvl + 1))) & ((i >> lvl) != (j >> lvl)))
    nblk = lt // blk

    for h in range(nh):
        q = act_sc[:, h * hd:(h + 1) * hd]
        k = act_sc[:, (nh + h) * hd:(nh + h + 1) * hd]
        v = act_sc[:, (2 * nh + h) * hd:(2 * nh + h + 1) * hd]
        q = q * lax.rsqrt(jnp.sum(q * q, axis=-1, keepdims=True) + RMS_EPS) * dk_scale
        k = k * lax.rsqrt(jnp.sum(k * k, axis=-1, keepdims=True) + RMS_EPS)
        beta = beta_all[:, h:h + 1]
        gcc = gc_all[:, nh + h:nh + h + 1]
        gcr = gc_t[nh + h:nh + h + 1, :]
        glc = gl_all[:, nh + h:nh + h + 1]

        decay = jnp.exp(jnp.where(incl, gcc - gcr, -jnp.inf))
        kb = k * beta
        k16 = k.astype(BF16)
        m = _dot_nt(kb.astype(BF16), k16) * jnp.where(strict, decay, 0.0)
        egc = jnp.exp(gcc)
        dinv = eye - jnp.where(pair_masks[0], m, 0.0)
        for pm in pair_masks[1:]:
            d16 = dinv.astype(BF16)
            c16 = jnp.where(pm, m, 0.0).astype(BF16)
            dinv = dinv - _dot(_dot(d16, c16).astype(BF16), d16)
        x = _dot(dinv.astype(BF16), jnp.concatenate([v * beta, kb * egc], axis=1).astype(BF16))
        u = x[:, :hd]
        w = x[:, hd:]
        qk16 = (_dot_nt(q.astype(BF16), k16) * decay).astype(BF16)
        qd = q * egc
        kd = k * jnp.exp(glc - gcc)

        vn_parts, os_parts = [], []
        for b in range(nblk):
            rs = slice(b * blk, (b + 1) * blk)
            if chained:
                s0 = st_ref[0, h]
            else:
                s0 = sin_ref[b, h]
            s16 = s0.astype(BF16)
            r = _dot(jnp.concatenate([w[rs], qd[rs]], axis=0).astype(BF16), s16)
            vn = u[rs] - r[:blk]
            s_new = s0 * jnp.exp(glc[b * blk:b * blk + 1, :]) + _dot_tn(kd[rs].astype(BF16), vn.astype(BF16))
            if chained:
                st_ref[0, h] = s_new
            else:
                st_ref[b, h] = s_new
            vn_parts.append(vn)
            os_parts.append(r[blk:])
        vn_all = jnp.concatenate(vn_parts, axis=0)
        o = jnp.concatenate(os_parts, axis=0) + _dot(qk16, vn_all.astype(BF16))
        o = o * lax.rsqrt(jnp.mean(o * o, axis=-1, keepdims=True) + RMS_EPS) * ng_ref[...]
        o = o * _silu(z_ref[:, h * hd:(h + 1) * hd])
        o_ref[:, h * hd:(h + 1) * hd] = o.astype(BF16)


def _gdn(proj, conv_w, gate_par, norm_g, *, n_seq, n_rows, blk, conv_buf=None, s0=None):
    t_rows = proj.shape[0]
    nh = DN_HEADS
    hd = LANES
    cqkv = 3 * nh * hd
    lt = SEQ_TILE
    assert lt % blk == 0 and blk & (blk - 1) == 0 and blk >= 2
    chained = s0 is None
    n_tiles = t_rows // lt
    ba_col = proj.shape[1] // LANES - 1
    common_in = [
        pl.BlockSpec((lt, cqkv), lambda t: (t, 0)),
        pl.BlockSpec((lt, nh * hd), lambda t: (t, 3)),
        pl.BlockSpec((lt, LANES), lambda t: (t, ba_col)),
    ]
    par_in = [
        pl.BlockSpec(conv_w.shape, lambda t: (0, 0)),
        pl.BlockSpec(gate_par.shape, lambda t: (0, 0)),
        pl.BlockSpec((1, hd), lambda t: (0, 0)),
    ]
    o_spec = pl.BlockSpec((lt, nh * hd), lambda t: (t, 0))
    o_shape = jax.ShapeDtypeStruct((t_rows, nh * hd), BF16)
    scratch = [pltpu.VMEM((lt + SUBLANES, cqkv), F32)]
    if chained:
        assert n_rows % lt == 0 and blk == DN_CHUNK
        tps = n_rows // lt
        in_specs = common_in + par_in
        args = (proj, proj, proj, conv_w, gate_par, norm_g)
        st_spec = pl.BlockSpec((1, nh, hd, hd), lambda t: (t // tps, 0, 0, 0))
    else:
        assert blk == n_rows and conv_buf is not None
        tps = 0
        spt = lt // n_rows
        in_specs = common_in + [
            pl.BlockSpec((lt, cqkv), lambda t: (t, 0)),
            pl.BlockSpec((spt, nh, hd, hd), lambda t: (t, 0, 0, 0)),
        ] + par_in
        args = (proj, proj, proj, conv_buf, s0, conv_w, gate_par, norm_g)
        st_spec = pl.BlockSpec((spt, nh, hd, hd), lambda t: (t, 0, 0, 0))
        scratch.append(pltpu.VMEM((lt + SUBLANES, cqkv), F32))
    scratch.append(pltpu.VMEM((lt, cqkv), F32))
    return pl.pallas_call(
        functools.partial(_gdn_body, blk=blk, chained=chained, tiles_per_seq=tps,
                          dk_scale=float(hd) ** -0.5),
        name="gdn_chained" if chained else "gdn_stateful",
        grid=(n_tiles,),
        in_specs=in_specs,
        out_specs=[o_spec, st_spec],
        out_shape=[o_shape, jax.ShapeDtypeStruct((n_seq, nh, hd, hd), F32)],
        scratch_shapes=scratch,
        compiler_params=pltpu.CompilerParams(
            dimension_semantics=("arbitrary",), vmem_limit_bytes=56 * MIB),
    )(*args)


def _mlp_body(u_ref, v_ref, ws_ref, bst_ref, lng_ref, lnb_ref, o_ref, vr_ref, *, blk):
    lt = u_ref.shape[0]
    gd = u_ref.shape[1] // MLP_GROUPS
    i, j, same = _block_masks(lt, blk)
    incl = same & (i >= j)
    hi = lax.Precision.HIGHEST
    if blk != lt:
        rep = (j == (i & (blk - 1))).astype(F32)
        bias_all = _dot(rep, bst_ref[...], hi)
    else:
        bias_all = bst_ref[...]
    for g in range(MLP_GROUPS):
        cs = slice(g * gd, (g + 1) * gd)
        uu = jax.nn.gelu(u_ref[:, cs])
        vv = _layer_norm(jax.nn.gelu(v_ref[:, cs]), lng_ref[:, cs], lnb_ref[:, cs])
        vr_ref[:, cs] = vv
        wsp = ws_ref[g]
        if blk != lt:
            wsp = _dot_nt(_dot(rep, wsp, hi), rep, hi)
        wsp = jnp.where(incl, wsp, 0.0).astype(BF16)
        s = _dot(wsp, vv.astype(BF16)) + bias_all[:, g:g + 1]
        o_ref[:, cs] = (uu * s).astype(BF16)


def _mlp(proj, w_spatial, b_spatial, ln_g, ln_b, *, blk):
    t_rows = proj.shape[0]
    lt = SEQ_TILE
    width = ln_g.size
    assert lt == MLP_CHUNK == w_spatial.shape[1] and lt % blk == 0
    u_col = (4 * DN_HEADS * LANES) // width
    bst = jnp.zeros((lt, LANES), F32).at[:, :MLP_GROUPS].set(b_spatial.T)
    return pl.pallas_call(
        functools.partial(_mlp_body, blk=blk),
        name="spatial_mlp",
        grid=(t_rows // lt,),
        in_specs=[
            pl.BlockSpec((lt, width), lambda t: (t, u_col)),
            pl.BlockSpec((lt, width), lambda t: (t, u_col + 1)),
            pl.BlockSpec(w_spatial.shape, lambda t: (0, 0, 0)),
            pl.BlockSpec((lt, LANES), lambda t: (0, 0)),
            pl.BlockSpec((1, width), lambda t: (0, 0)),
            pl.BlockSpec((1, width), lambda t: (0, 0)),
        ],
        out_specs=[
            pl.BlockSpec((lt, width), lambda t: (t, 0)),
            pl.BlockSpec((lt, width), lambda t: (t, 0)),
        ],
        out_shape=[
            jax.ShapeDtypeStruct((t_rows, width), BF16),
            jax.ShapeDtypeStruct((t_rows, width), F32),
        ],
        compiler_params=pltpu.CompilerParams(
            dimension_semantics=("arbitrary",), vmem_limit_bytes=32 * MIB),
    )(proj, proj, w_spatial, bst, ln_g.reshape(1, width), ln_b.reshape(1, width))


def _mix_body(od_ref, om_ref, x_ref, gt_ref, w_ref, lng_ref, lnb_ref, o_ref, *, seq_tiles, alpha):
    tm, d = o_ref.shape
    kd = od_ref.shape[1]
    mix = _dot(od_ref[...], w_ref[:kd, :]) + _dot(om_ref[...], w_ref[kd:, :])
    x = x_ref[...]
    gate = _mod_rows(gt_ref, seq_tiles)[:, None, :]
    y = alpha * x + gate * mix.reshape(x.shape)
    o_ref[...] = _layer_norm(y.reshape(tm, d), lng_ref[...], lnb_ref[...])


def _mix(o_dn, o_mlp, x3, mod, sub, w_out, ln_g, ln_b, alpha):
    n_seq, n_rows, d = x3.shape
    grp = _Group(n_seq, n_rows, MIX_TM)
    tm = grp.tm
    out = pl.pallas_call(
        functools.partial(_mix_body, seq_tiles=grp.seq_tiles, alpha=alpha),
        name="out_mix_ln",
        grid=(grp.n_tiles,),
        in_specs=[
            pl.BlockSpec((tm, o_dn.shape[1]), lambda m: (m, 0)),
            pl.BlockSpec((tm, o_mlp.shape[1]), lambda m: (m, 0)),
            grp.x_spec(d, 1),
            grp.mod_spec(mod, d, sub * 3 + 2, 1),
            pl.BlockSpec(w_out.shape, lambda m: (0, 0), pipeline_mode=pl.Buffered(1)),
            pl.BlockSpec((1, d), lambda m: (0, 0)),
            pl.BlockSpec((1, d), lambda m: (0, 0)),
        ],
        out_specs=pl.BlockSpec((tm, d), lambda m: (m, 0)),
        out_shape=jax.ShapeDtypeStruct((n_seq * n_rows, d), F32),
        compiler_params=pltpu.CompilerParams(
            dimension_semantics=("arbitrary",), vmem_limit_bytes=48 * MIB),
    )(o_dn, o_mlp, x3, mod, w_out, ln_g.reshape(1, d), ln_b.reshape(1, d))
    return out.reshape(n_seq, n_rows, d)


def _trunk_layer(x3, mod, wts, alpha, conv_buf, s0):
    n_seq, n_rows, d = x3.shape
    qkv_dim = 3 * DN_HEADS * LANES
    x3 = _ffn(x3, mod, 0, wts["wg"][0], wts["wu"][0], wts["wd"][0], wts["ln_g"][0], wts["ln_b"][0], alpha)
    proj = _proj(x3, mod, 1, wts["w_cat"])
    dn_blk = DN_CHUNK if n_rows % DN_CHUNK == 0 else n_rows
    if s0 is None:
        o_dn, s_new = _gdn(proj, wts["conv_w"], wts["gate_par"], wts["dn_norm_g"],
                           n_seq=n_seq, n_rows=n_rows, blk=dn_blk)
    else:
        pad = jnp.zeros((n_seq, SUBLANES - (CONV_W - 1), qkv_dim), F32)
        buf = jnp.concatenate([pad, conv_buf], axis=1).reshape(n_seq * SUBLANES, qkv_dim)
        assert n_rows == SUBLANES
        o_dn, s_new = _gdn(proj, wts["conv_w"], wts["gate_par"], wts["dn_norm_g"],
                           n_seq=n_seq, n_rows=n_rows, blk=dn_blk, conv_buf=buf, s0=s0)
    o_mlp, v_rows = _mlp(proj, wts["w_spatial"], wts["b_spatial"], wts["mlp_ln_g"], wts["mlp_ln_b"],
                         blk=min(n_rows, MLP_CHUNK))
    x3 = _mix(o_dn, o_mlp, x3, mod, 1, wts["w_out"], wts["ln_g"][1], wts["ln_b"][1], alpha)
    x3 = _ffn(x3, mod, 2, wts["wg"][1], wts["wu"][1], wts["wd"][1], wts["ln_g"][2], wts["ln_b"][2], alpha)
    new_buf = proj.reshape(n_seq, n_rows, -1)[:, n_rows - (CONV_W - 1):, :qkv_dim]
    return x3, new_buf, s_new, v_rows.reshape(n_seq, n_rows, -1)


def kernel(x_prompt, x_sample, c_prompt, c_sample, state_delta, state_conv, w_ada, b_ada, ln_g, ln_b,
           ffn_wg, ffn_wu, ffn_wd, w_in, conv_w, a_log, dt_bias, dn_norm_g, mlp_ln_g, mlp_ln_b,
           w_spatial, b_spatial, w_out):
    depth = w_ada.shape[0]
    bp, _, d = x_prompt.shape
    bs = x_sample.shape[0]
    nh = DN_HEADS
    alpha = (2.0 * depth) ** 0.25
    qkvz = 4 * nh * LANES
    gates = 2 * nh

    pad_rows = (-(bs + bp)) % SUBLANES
    c_all = jnp.concatenate([c_sample, c_prompt, jnp.zeros((pad_rows, d), F32)], axis=0)

    y_p, y_s = x_prompt, x_sample
    delta_p, conv_p, delta_s, conv_s, vrows_s = [], [], [], [], []
    for layer in range(depth):
        mod_s, mod_p = _ada(c_all, w_ada[layer], b_ada[layer], bs)
        wi = w_in[layer]
        w_cat = jnp.concatenate(
            [wi[:, :qkvz], wi[:, qkvz + gates:], wi[:, qkvz:qkvz + gates],
             jnp.zeros((d, LANES - gates), F32)], axis=1).astype(BF16)
        gate_par = jnp.zeros((SUBLANES, LANES), F32)
        gate_par = gate_par.at[0, nh:gates].set(a_log[layer]).at[1, nh:gates].set(dt_bias[layer])
        wts = dict(
            wg=ffn_wg[layer].astype(BF16), wu=ffn_wu[layer].astype(BF16), wd=ffn_wd[layer].astype(BF16),
            ln_g=ln_g[layer], ln_b=ln_b[layer], w_cat=w_cat, conv_w=conv_w[layer], gate_par=gate_par,
            dn_norm_g=dn_norm_g[layer].reshape(1, LANES), mlp_ln_g=mlp_ln_g[layer], mlp_ln_b=mlp_ln_b[layer],
            w_spatial=w_spatial[layer], b_spatial=b_spatial[layer], w_out=w_out[layer].astype(BF16),
        )
        y_p, cb_p, ds_p, _ = _trunk_layer(y_p, mod_p, wts, alpha, None, None)
        y_s, cb_s, ds_s, vr_s = _trunk_layer(y_s, mod_s, wts, alpha, state_conv[layer], state_delta[layer])
        delta_p.append(ds_p)
        conv_p.append(cb_p)
        delta_s.append(ds_s)
        conv_s.append(cb_s)
        vrows_s.append(vr_s)
    return (y_p, y_s, jnp.stack(delta_p), jnp.stack(conv_p), jnp.stack(delta_s), jnp.stack(conv_s),
            jnp.stack(vrows_s))
```

```python
import functools
import math

import jax
import jax.numpy as jnp
from jax import lax
from jax.experimental import pallas as pl
from jax.experimental.pallas import tpu as pltpu

F32 = jnp.float32
BF16 = jnp.bfloat16

DN_HEADS = 8
DN_CHUNK = 64
CONV_W = 4
MLP_GROUPS = 4
MLP_CHUNK = 128
N_SUB = 3
LN_EPS = 1e-5
RMS_EPS = 1e-6

LANES = 128
SUBLANES = 8
MIB = 2 ** 20

FFN_TM = 1024
FFN_TF = 512
PROJ_TN = 896
MIX_TM = 512
ADA_TN = 1024
SEQ_TILE = 128


def _dot(a, b, precision=None):
    return jnp.dot(a, b, preferred_element_type=F32, precision=precision)


def _dot_nt(a, b, precision=None):
    return lax.dot_general(a, b, (((1,), (1,)), ((), ())),
                           preferred_element_type=F32, precision=precision)


def _dot_tn(a, b):
    return lax.dot_general(a, b, (((0,), (0,)), ((), ())), preferred_element_type=F32)


def _layer_norm(y, g, b):
    mu = jnp.mean(y, axis=-1, keepdims=True)
    yc = y - mu
    var = jnp.mean(yc * yc, axis=-1, keepdims=True)
    return yc * lax.rsqrt(var + LN_EPS) * g + b


def _silu(x):
    return x * jax.nn.sigmoid(x)


def _mod_rows(ref, seq_tiles):
    if seq_tiles:
        return ref[pl.ds(pl.program_id(0) // seq_tiles, 1), :]
    return ref[...]


def _ada_body(c_ref, w_ref, b_ref, ms_ref, mp_ref):
    c = c_ref[...]
    a = _silu(c).astype(BF16)
    y = _dot(a, w_ref[...].astype(BF16)) + b_ref[...]
    ns = ms_ref.shape[0]
    ms_ref[...] = y[:ns]
    mp_ref[...] = y[ns:]


def _ada(c_all, w_ada, b_ada, n_sample):
    rows, d = c_all.shape
    n = w_ada.shape[1]
    return pl.pallas_call(
        _ada_body,
        name="ada_mod",
        grid=(n // ADA_TN,),
        in_specs=[
            pl.BlockSpec((rows, d), lambda j: (0, 0)),
            pl.BlockSpec((d, ADA_TN), lambda j: (0, j)),
            pl.BlockSpec((1, ADA_TN), lambda j: (0, j)),
        ],
        out_specs=[
            pl.BlockSpec((n_sample, ADA_TN), lambda j: (0, j)),
            pl.BlockSpec((rows - n_sample, ADA_TN), lambda j: (0, j)),
        ],
        out_shape=[
            jax.ShapeDtypeStruct((n_sample, n), F32),
            jax.ShapeDtypeStruct((rows - n_sample, n), F32),
        ],
        compiler_params=pltpu.CompilerParams(
            dimension_semantics=("arbitrary",), vmem_limit_bytes=40 * MIB),
    )(c_all, w_ada, b_ada.reshape(1, n))


class _Group:
    def __init__(self, n_seq, n_rows, tm):
        if n_rows >= tm:
            assert n_rows % tm == 0
            self.sb, self.rb = 1, tm
        else:
            assert tm % n_rows == 0 and n_seq % (tm // n_rows) == 0
            self.sb, self.rb = tm // n_rows, n_rows
        self.n_seq, self.n_rows = n_seq, n_rows
        self.tiles_r = n_rows // self.rb
        self.tm = self.sb * self.rb
        self.n_tiles = (n_seq // self.sb) * self.tiles_r
        self.seq_tiles = self.tiles_r if self.sb == 1 else 0

    def x_spec(self, d, ngrid, **kw):
        tr = self.tiles_r
        if ngrid == 2:
            return pl.BlockSpec((self.sb, self.rb, d), lambda m, f: (m // tr, m % tr, 0), **kw)
        return pl.BlockSpec((self.sb, self.rb, d), lambda m: (m // tr, m % tr, 0), **kw)

    def mod_spec(self, mod, d, col, ngrid):
        if self.sb == 1:
            rows, row_blk = mod.shape[0], (lambda m: 0)
        else:
            rows, row_blk = self.sb, (lambda m: m)
        if ngrid == 2:
            return pl.BlockSpec((rows, d), lambda m, f: (row_blk(m), col))
        return pl.BlockSpec((rows, d), lambda m: (row_blk(m), col))


def _ffn_body(x_ref, sh_ref, sc_ref, gt_ref, wg_ref, wu_ref, wd_ref, lng_ref, lnb_ref,
              o_ref, h_sc, *, seq_tiles, alpha):
    f = pl.program_id(1)
    tm, d = o_ref.shape

    @pl.when(f == 0)
    def _():
        x = x_ref[...]
        h = x * (1.0 + _mod_rows(sc_ref, seq_tiles)[:, None, :]) + _mod_rows(sh_ref, seq_tiles)[:, None, :]
        h_sc[...] = h.reshape(tm, d).astype(BF16)
        o_ref[...] = jnp.zeros_like(o_ref)

    h = h_sc[...]
    g = _dot(h, wg_ref[...])
    u = _dot(h, wu_ref[...])
    a = (_silu(g) * u).astype(BF16)
    o_ref[...] += _dot(a, wd_ref[...])

    @pl.when(f == pl.num_programs(1) - 1)
    def _():
        x = x_ref[...]
        gate = _mod_rows(gt_ref, seq_tiles)[:, None, :]
        y = alpha * x + (0.5 * gate) * o_ref[...].reshape(x.shape)
        o_ref[...] = _layer_norm(y.reshape(tm, d), lng_ref[...], lnb_ref[...])


def _ffn(x3, mod, sub, wg, wu, wd, ln_g, ln_b, alpha):
    n_seq, n_rows, d = x3.shape
    dff = wg.shape[1]
    grp = _Group(n_seq, n_rows, FFN_TM)
    tm = grp.tm
    out = pl.pallas_call(
        functools.partial(_ffn_body, seq_tiles=grp.seq_tiles, alpha=alpha),
        name="swiglu_ln",
        grid=(grp.n_tiles, dff // FFN_TF),
        in_specs=[
            grp.x_spec(d, 2, pipeline_mode=pl.Buffered(1)),
            grp.mod_spec(mod, d, sub * 3 + 0, 2),
            grp.mod_spec(mod, d, sub * 3 + 1, 2),
            grp.mod_spec(mod, d, sub * 3 + 2, 2),
            pl.BlockSpec((d, FFN_TF), lambda m, f: (0, f)),
            pl.BlockSpec((d, FFN_TF), lambda m, f: (0, f)),
            pl.BlockSpec((FFN_TF, d), lambda m, f: (f, 0)),
            pl.BlockSpec((1, d), lambda m, f: (0, 0)),
            pl.BlockSpec((1, d), lambda m, f: (0, 0)),
        ],
        out_specs=pl.BlockSpec((tm, d), lambda m, f: (m, 0)),
        out_shape=jax.ShapeDtypeStruct((n_seq * n_rows, d), F32),
        scratch_shapes=[pltpu.VMEM((tm, d), BF16)],
        compiler_params=pltpu.CompilerParams(
            dimension_semantics=("arbitrary", "arbitrary"), vmem_limit_bytes=56 * MIB),
    )(x3, mod, mod, mod, wg, wu, wd, ln_g.reshape(1, d), ln_b.reshape(1, d))
    return out.reshape(n_seq, n_rows, d)


def _proj_body(x_ref, sh_ref, sc_ref, w_ref, o_ref, h_sc, *, seq_tiles):
    tm, d = h_sc.shape

    @pl.when(pl.program_id(1) == 0)
    def _():
        x = x_ref[...]
        h = x * (1.0 + _mod_rows(sc_ref, seq_tiles)[:, None, :]) + _mod_rows(sh_ref, seq_tiles)[:, None, :]
        h_sc[...] = h.reshape(tm, d).astype(BF16)

    o_ref[...] = _dot(h_sc[...], w_ref[...])


def _proj(x3, mod, sub, w_cat):
    n_seq, n_rows, d = x3.shape
    n = w_cat.shape[1]
    grp = _Group(n_seq, n_rows, FFN_TM)
    tm = grp.tm
    return pl.pallas_call(
        functools.partial(_proj_body, seq_tiles=grp.seq_tiles),
        name="in_proj",
        grid=(grp.n_tiles, n // PROJ_TN),
        in_specs=[
            grp.x_spec(d, 2, pipeline_mode=pl.Buffered(1)),
            grp.mod_spec(mod, d, sub * 3 + 0, 2),
            grp.mod_spec(mod, d, sub * 3 + 1, 2),
            pl.BlockSpec((d, PROJ_TN), lambda m, j: (0, j)),
        ],
        out_specs=pl.BlockSpec((tm, PROJ_TN), lambda m, j: (m, j)),
        out_shape=jax.ShapeDtypeStruct((n_seq * n_rows, n), F32),
        scratch_shapes=[pltpu.VMEM((tm, d), BF16)],
        compiler_params=pltpu.CompilerParams(
            dimension_semantics=("arbitrary", "arbitrary"), vmem_limit_bytes=48 * MIB),
    )(x3, mod, mod, w_cat)


def _block_masks(n, blk):
    shift = int(math.log2(blk))
    i = lax.broadcasted_iota(jnp.int32, (n, n), 0)
    j = lax.broadcasted_iota(jnp.int32, (n, n), 1)
    same = (i >> shift) == (j >> shift)
    return i, j, same


def _gdn_body(*refs, blk, chained, tiles_per_seq, dk_scale):
    if chained:
        (qkv_ref, z_ref, ba_ref, cw_ref, gp_ref, ng_ref, o_ref, st_ref, xc_sc, act_sc) = refs
    else:
        (qkv_ref, z_ref, ba_ref, buf_ref, sin_ref, cw_ref, gp_ref, ng_ref, o_ref, st_ref,
         xc_sc, bs_sc, act_sc) = refs
    lt, cqkv = qkv_ref.shape
    hd = LANES
    nh = cqkv // (3 * hd)
    t = pl.program_id(0)
    halo = SUBLANES

    if chained:
        first = (t % tiles_per_seq) == 0

        @pl.when(first)
        def _():
            xc_sc[pl.ds(0, halo), :] = jnp.zeros((halo, cqkv), F32)
            st_ref[...] = jnp.zeros_like(st_ref)

        @pl.when(jnp.logical_not(first))
        def _():
            xc_sc[pl.ds(0, halo), :] = xc_sc[pl.ds(lt, halo), :]
    else:
        xc_sc[pl.ds(0, halo), :] = jnp.zeros((halo, cqkv), F32)
        bs_sc[pl.ds(0, lt), :] = buf_ref[...]
        bs_sc[pl.ds(lt, halo), :] = jnp.zeros((halo, cqkv), F32)
    xc_sc[pl.ds(halo, lt), :] = qkv_ref[...]

    cblk = 2 * LANES
    row_in_seq = lax.broadcasted_iota(jnp.int32, (lt, cblk), 0) & (blk - 1)
    for c0 in range(0, cqkv, cblk):
        cs = slice(c0, c0 + cblk)
        y = None
        for j in range(CONV_W):
            d = CONV_W - 1 - j
            term = xc_sc[pl.ds(halo - d, lt), cs]
            if not chained and d > 0:
                term = jnp.where(row_in_seq >= d, term, bs_sc[pl.ds(halo - d, lt), cs])
            term = cw_ref[j:j + 1, cs] * term
            y = term if y is None else y + term
        act_sc[:, cs] = _silu(y)

    ba = ba_ref[...]
    beta_all = jax.nn.sigmoid(ba)
    g_all = -jnp.exp(gp_ref[0:1, :]) * jax.nn.softplus(ba + gp_ref[1:2, :])

    i, j, same = _block_masks(lt, blk)
    incl = same & (i >= j)
    strict = same & (i > j)
    hi = lax.Precision.HIGHEST
    gc_all = _dot(incl.astype(F32), g_all, hi)
    gl_all = _dot((j == (i | (blk - 1))).astype(F32), gc_all, hi)
    gc_t = gc_all.T
    eye = (i == j).astype(F32)
    pair_masks = []
    for lvl in range(int(math.log2(blk))):
        pair_masks.append(((i >> (lvl + 1)) == (j >> (lvl + 1))) & ((i >> lvl) != (j >> lvl)))
    nblk = lt // blk

    heads = range(nh)
    hp = []
    for h in heads:
        q = act_sc[:, h * hd:(h + 1) * hd]
        k = act_sc[:, (nh + h) * hd:(nh + h + 1) * hd]
        v = act_sc[:, (2 * nh + h) * hd:(2 * nh + h + 1) * hd]
        q = q * lax.rsqrt(jnp.sum(q * q, axis=-1, keepdims=True) + RMS_EPS) * dk_scale
        k = k * lax.rsqrt(jnp.sum(k * k, axis=-1, keepdims=True) + RMS_EPS)
        beta = beta_all[:, h:h + 1]
        gcc = gc_all[:, nh + h:nh + h + 1]
        gcr = gc_t[nh + h:nh + h + 1, :]
        glc = gl_all[:, nh + h:nh + h + 1]
        decay = jnp.exp(jnp.where(incl, gcc - gcr, -jnp.inf))
        kb = k * beta
        k16 = k.astype(BF16)
        egc = jnp.exp(gcc)
        hp.append(dict(
            decay=decay, k16=k16, kb16=kb.astype(BF16), q16=q.astype(BF16), glc=glc,
            rhs16=jnp.concatenate([v * beta, kb * egc], axis=1).astype(BF16),
            qd=q * egc, kd=k * jnp.exp(glc - gcc)))

    ms = [_dot_nt(p["kb16"], p["k16"]) * jnp.where(strict, p["decay"], 0.0) for p in hp]
    dinvs = [eye - jnp.where(pair_masks[0], m, 0.0) for m in ms]
    for pm in pair_masks[1:]:
        d16s = [d.astype(BF16) for d in dinvs]
        t16s = [_dot(d16, jnp.where(pm, m, 0.0).astype(BF16)).astype(BF16) for d16, m in zip(d16s, ms)]
        dinvs = [d - _dot(t16, d16) for d, t16, d16 in zip(dinvs, t16s, d16s)]
    xs = [_dot(d.astype(BF16), p["rhs16"]) for d, p in zip(dinvs, hp)]
    qk16s = [(_dot_nt(p["q16"], p["k16"]) * p["decay"]).astype(BF16) for p in hp]

    vn_parts = [[] for _ in heads]
    os_parts = [[] for _ in heads]
    for b in range(nblk):
        rs = slice(b * blk, (b + 1) * blk)
        s0s = [st_ref[0, h] if chained else sin_ref[b, h] for h in heads]
        rr = [_dot(jnp.concatenate([x[rs, hd:], p["qd"][rs]], axis=0).astype(BF16), s0.astype(BF16))
              for x, p, s0 in zip(xs, hp, s0s)]
        vns = [x[rs, :hd] - r[:blk] for x, r in zip(xs, rr)]
        for h in heads:
            p = hp[h]
            s_new = (s0s[h] * jnp.exp(p["glc"][b * blk:b * blk + 1, :])
                     + _dot_tn(p["kd"][rs].astype(BF16), vns[h].astype(BF16)))
            if chained:
                st_ref[0, h] = s_new
            else:
                st_ref[b, h] = s_new
            vn_parts[h].append(vns[h])
            os_parts[h].append(rr[h][blk:])
    for h in heads:
        vn_all = jnp.concatenate(vn_parts[h], axis=0)
        o = jnp.concatenate(os_parts[h], axis=0) + _dot(qk16s[h], vn_all.astype(BF16))
        o = o * lax.rsqrt(jnp.mean(o * o, axis=-1, keepdims=True) + RMS_EPS) * ng_ref[...]
        o = o * _silu(z_ref[:, h * hd:(h + 1) * hd])
        o_ref[:, h * hd:(h + 1) * hd] = o.astype(BF16)


def _gdn(proj, conv_w, gate_par, norm_g, *, n_seq, n_rows, blk, conv_buf=None, s0=None):
    t_rows = proj.shape[0]
    nh = DN_HEADS
    hd = LANES
    cqkv = 3 * nh * hd
    lt = SEQ_TILE
    assert lt % blk == 0 and blk & (blk - 1) == 0 and blk >= 2
    chained = s0 is None
    n_tiles = t_rows // lt
    ba_col = proj.shape[1] // LANES - 1
    common_in = [
        pl.BlockSpec((lt, cqkv), lambda t: (t, 0)),
        pl.BlockSpec((lt, nh * hd), lambda t: (t, 3)),
        pl.BlockSpec((lt, LANES), lambda t: (t, ba_col)),
    ]
    par_in = [
        pl.BlockSpec(conv_w.shape, lambda t: (0, 0)),
        pl.BlockSpec(gate_par.shape, lambda t: (0, 0)),
        pl.BlockSpec((1, hd), lambda t: (0, 0)),
    ]
    o_spec = pl.BlockSpec((lt, nh * hd), lambda t: (t, 0))
    o_shape = jax.ShapeDtypeStruct((t_rows, nh * hd), BF16)
    scratch = [pltpu.VMEM((lt + SUBLANES, cqkv), F32)]
    if chained:
        assert n_rows % lt == 0 and blk == DN_CHUNK
        tps = n_rows // lt
        in_specs = common_in + par_in
        args = (proj, proj, proj, conv_w, gate_par, norm_g)
        st_spec = pl.BlockSpec((1, nh, hd, hd), lambda t: (t // tps, 0, 0, 0))
    else:
        assert blk == n_rows and conv_buf is not None
        tps = 0
        spt = lt // n_rows
        in_specs = common_in + [
            pl.BlockSpec((lt, cqkv), lambda t: (t, 0)),
            pl.BlockSpec((spt, nh, hd, hd), lambda t: (t, 0, 0, 0)),
        ] + par_in
        args = (proj, proj, proj, conv_buf, s0, conv_w, gate_par, norm_g)
        st_spec = pl.BlockSpec((spt, nh, hd, hd), lambda t: (t, 0, 0, 0))
        scratch.append(pltpu.VMEM((lt + SUBLANES, cqkv), F32))
    scratch.append(pltpu.VMEM((lt, cqkv), F32))
    return pl.pallas_call(
        functools.partial(_gdn_body, blk=blk, chained=chained, tiles_per_seq=tps,
                          dk_scale=float(hd) ** -0.5),
        name="gdn_chained" if chained else "gdn_stateful",
        grid=(n_tiles,),
        in_specs=in_specs,
        out_specs=[o_spec, st_spec],
        out_shape=[o_shape, jax.ShapeDtypeStruct((n_seq, nh, hd, hd), F32)],
        scratch_shapes=scratch,
        compiler_params=pltpu.CompilerParams(
            dimension_semantics=("arbitrary",), vmem_limit_bytes=56 * MIB),
    )(*args)


def _mlp_body(u_ref, v_ref, ws_ref, bst_ref, lng_ref, lnb_ref, o_ref, vr_ref, *, blk):
    lt = u_ref.shape[0]
    gd = u_ref.shape[1] // MLP_GROUPS
    i, j, same = _block_masks(lt, blk)
    incl = same & (i >= j)
    hi = lax.Precision.HIGHEST
    if blk != lt:
        rep = (j == (i & (blk - 1))).astype(F32)
        bias_all = _dot(rep, bst_ref[...], hi)
    else:
        bias_all = bst_ref[...]
    for g in range(MLP_GROUPS):
        cs = slice(g * gd, (g + 1) * gd)
        uu = jax.nn.gelu(u_ref[:, cs])
        vv = _layer_norm(jax.nn.gelu(v_ref[:, cs]), lng_ref[:, cs], lnb_ref[:, cs])
        vr_ref[:, cs] = vv
        wsp = ws_ref[g]
        if blk != lt:
            wsp = _dot_nt(_dot(rep, wsp, hi), rep, hi)
        wsp = jnp.where(incl, wsp, 0.0).astype(BF16)
        s = _dot(wsp, vv.astype(BF16)) + bias_all[:, g:g + 1]
        o_ref[:, cs] = (uu * s).astype(BF16)


def _mlp(proj, w_spatial, b_spatial, ln_g, ln_b, *, blk):
    t_rows = proj.shape[0]
    lt = SEQ_TILE
    width = ln_g.size
    assert lt == MLP_CHUNK == w_spatial.shape[1] and lt % blk == 0
    u_col = (4 * DN_HEADS * LANES) // width
    bst = jnp.zeros((lt, LANES), F32).at[:, :MLP_GROUPS].set(b_spatial.T)
    return pl.pallas_call(
        functools.partial(_mlp_body, blk=blk),
        name="spatial_mlp",
        grid=(t_rows // lt,),
        in_specs=[
            pl.BlockSpec((lt, width), lambda t: (t, u_col)),
            pl.BlockSpec((lt, width), lambda t: (t, u_col + 1)),
            pl.BlockSpec(w_spatial.shape, lambda t: (0, 0, 0)),
            pl.BlockSpec((lt, LANES), lambda t: (0, 0)),
            pl.BlockSpec((1, width), lambda t: (0, 0)),
            pl.BlockSpec((1, width), lambda t: (0, 0)),
        ],
        out_specs=[
            pl.BlockSpec((lt, width), lambda t: (t, 0)),
            pl.BlockSpec((lt, width), lambda t: (t, 0)),
        ],
        out_shape=[
            jax.ShapeDtypeStruct((t_rows, width), BF16),
            jax.ShapeDtypeStruct((t_rows, width), F32),
        ],
        compiler_params=pltpu.CompilerParams(
            dimension_semantics=("arbitrary",), vmem_limit_bytes=32 * MIB),
    )(proj, proj, w_spatial, bst, ln_g.reshape(1, width), ln_b.reshape(1, width))


def _mix_body(od_ref, om_ref, x_ref, gt_ref, w_ref, lng_ref, lnb_ref, o_ref, *, seq_tiles, alpha):
    tm, d = o_ref.shape
    kd = od_ref.shape[1]
    mix = _dot(od_ref[...], w_ref[:kd, :]) + _dot(om_ref[...], w_ref[kd:, :])
    x = x_ref[...]
    gate = _mod_rows(gt_ref, seq_tiles)[:, None, :]
    y = alpha * x + gate * mix.reshape(x.shape)
    o_ref[...] = _layer_norm(y.reshape(tm, d), lng_ref[...], lnb_ref[...])


def _mix(o_dn, o_mlp, x3, mod, sub, w_out, ln_g, ln_b, alpha):
    n_seq, n_rows, d = x3.shape
    grp = _Group(n_seq, n_rows, MIX_TM)
    tm = grp.tm
    out = pl.pallas_call(
        functools.partial(_mix_body, seq_tiles=grp.seq_tiles, alpha=alpha),
        name="out_mix_ln",
        grid=(grp.n_tiles,),
        in_specs=[
            pl.BlockSpec((tm, o_dn.shape[1]), lambda m: (m, 0)),
            pl.BlockSpec((tm, o_mlp.shape[1]), lambda m: (m, 0)),
            grp.x_spec(d, 1),
            grp.mod_spec(mod, d, sub * 3 + 2, 1),
            pl.BlockSpec(w_out.shape, lambda m: (0, 0), pipeline_mode=pl.Buffered(1)),
            pl.BlockSpec((1, d), lambda m: (0, 0)),
            pl.BlockSpec((1, d), lambda m: (0, 0)),
        ],
        out_specs=pl.BlockSpec((tm, d), lambda m: (m, 0)),
        out_shape=jax.ShapeDtypeStruct((n_seq * n_rows, d), F32),
        compiler_params=pltpu.CompilerParams(
            dimension_semantics=("arbitrary",), vmem_limit_bytes=48 * MIB),
    )(o_dn, o_mlp, x3, mod, w_out, ln_g.reshape(1, d), ln_b.reshape(1, d))
    return out.reshape(n_seq, n_rows, d)


def _trunk_layer(x3, mod, wts, alpha, conv_buf, s0):
    n_seq, n_rows, d = x3.shape
    qkv_dim = 3 * DN_HEADS * LANES
    x3 = _ffn(x3, mod, 0, wts["wg"][0], wts["wu"][0], wts["wd"][0], wts["ln_g"][0], wts["ln_b"][0], alpha)
    proj = _proj(x3, mod, 1, wts["w_cat"])
    dn_blk = DN_CHUNK if n_rows % DN_CHUNK == 0 else n_rows
    if s0 is None:
        o_dn, s_new = _gdn(proj, wts["conv_w"], wts["gate_par"], wts["dn_norm_g"],
                           n_seq=n_seq, n_rows=n_rows, blk=dn_blk)
    else:
        pad = jnp.zeros((n_seq, SUBLANES - (CONV_W - 1), qkv_dim), F32)
        buf = jnp.concatenate([pad, conv_buf], axis=1).reshape(n_seq * SUBLANES, qkv_dim)
        assert n_rows == SUBLANES
        o_dn, s_new = _gdn(proj, wts["conv_w"], wts["gate_par"], wts["dn_norm_g"],
                           n_seq=n_seq, n_rows=n_rows, blk=dn_blk, conv_buf=buf, s0=s0)
    o_mlp, v_rows = _mlp(proj, wts["w_spatial"], wts["b_spatial"], wts["mlp_ln_g"], wts["mlp_ln_b"],
                         blk=min(n_rows, MLP_CHUNK))
    x3 = _mix(o_dn, o_mlp, x3, mod, 1, wts["w_out"], wts["ln_g"][1], wts["ln_b"][1], alpha)
    x3 = _ffn(x3, mod, 2, wts["wg"][1], wts["wu"][1], wts["wd"][1], wts["ln_g"][2], wts["ln_b"][2], alpha)
    new_buf = proj.reshape(n_seq, n_rows, -1)[:, n_rows - (CONV_W - 1):, :qkv_dim]
    return x3, new_buf, s_new, v_rows.reshape(n_seq, n_rows, -1)


def kernel(x_prompt, x_sample, c_prompt, c_sample, state_delta, state_conv, w_ada, b_ada, ln_g, ln_b,
           ffn_wg, ffn_wu, ffn_wd, w_in, conv_w, a_log, dt_bias, dn_norm_g, mlp_ln_g, mlp_ln_b,
           w_spatial, b_spatial, w_out):
    depth = w_ada.shape[0]
    bp, _, d = x_prompt.shape
    bs = x_sample.shape[0]
    nh = DN_HEADS
    alpha = (2.0 * depth) ** 0.25
    qkvz = 4 * nh * LANES
    gates = 2 * nh

    pad_rows = (-(bs + bp)) % SUBLANES
    c_all = jnp.concatenate([c_sample, c_prompt, jnp.zeros((pad_rows, d), F32)], axis=0)

    y_p, y_s = x_prompt, x_sample
    delta_p, conv_p, delta_s, conv_s, vrows_s = [], [], [], [], []
    for layer in range(depth):
        mod_s, mod_p = _ada(c_all, w_ada[layer], b_ada[layer], bs)
        wi = w_in[layer]
        w_cat = jnp.concatenate(
            [wi[:, :qkvz], wi[:, qkvz + gates:], wi[:, qkvz:qkvz + gates],
             jnp.zeros((d, LANES - gates), F32)], axis=1).astype(BF16)
        gate_par = jnp.zeros((SUBLANES, LANES), F32)
        gate_par = gate_par.at[0, nh:gates].set(a_log[layer]).at[1, nh:gates].set(dt_bias[layer])
        wts = dict(
            wg=[ffn_wg[layer, s].astype(BF16) for s in range(2)],
            wu=[ffn_wu[layer, s].astype(BF16) for s in range(2)],
            wd=[ffn_wd[layer, s].astype(BF16) for s in range(2)],
            ln_g=ln_g[layer], ln_b=ln_b[layer], w_cat=w_cat, conv_w=conv_w[layer], gate_par=gate_par,
            dn_norm_g=dn_norm_g[layer].reshape(1, LANES), mlp_ln_g=mlp_ln_g[layer], mlp_ln_b=mlp_ln_b[layer],
            w_spatial=w_spatial[layer], b_spatial=b_spatial[layer], w_out=w_out[layer].astype(BF16),
        )
        y_p, cb_p, ds_p, _ = _trunk_layer(y_p, mod_p, wts, alpha, None, None)
        y_s, cb_s, ds_s, vr_s = _trunk_layer(y_s, mod_s, wts, alpha, state_conv[layer], state_delta[layer])
        delta_p.append(ds_p)
        conv_p.append(cb_p)
        delta_s.append(ds_s)
        conv_s.append(cb_s)
        vrows_s.append(vr_s)
    return (y_p, y_s, jnp.stack(delta_p), jnp.stack(conv_p), jnp.stack(delta_s), jnp.stack(conv_s),
            jnp.stack(vrows_s))
```

```python
import functools
import math

import jax
import jax.numpy as jnp
from jax import lax
from jax.experimental import pallas as pl
from jax.experimental.pallas import tpu as pltpu

F32 = jnp.float32
BF16 = jnp.bfloat16

DN_HEADS = 8
DN_CHUNK = 64
CONV_W = 4
MLP_GROUPS = 4
MLP_CHUNK = 128
N_SUB = 3
LN_EPS = 1e-5
RMS_EPS = 1e-6

LANES = 128
SUBLANES = 8
MIB = 2 ** 20

FFN_TM = 1024
FFN_TF = 256
PROJ_TN = 1280
PROJ_N = 6400
PREP_TR = 256
MIX_TM = 512
ADA_TN = 1024
SEQ_TILE = 128


def _dot(a, b, precision=None):
    return jnp.dot(a, b, preferred_element_type=F32, precision=precision)


def _dot_nt(a, b, precision=None):
    return lax.dot_general(a, b, (((1,), (1,)), ((), ())),
                           preferred_element_type=F32, precision=precision)


def _dot_tn(a, b):
    return lax.dot_general(a, b, (((0,), (0,)), ((), ())), preferred_element_type=F32)


def _layer_norm(y, g, b):
    mu = jnp.mean(y, axis=-1, keepdims=True)
    yc = y - mu
    var = jnp.mean(yc * yc, axis=-1, keepdims=True)
    return yc * lax.rsqrt(var + LN_EPS) * g + b


def _silu(x):
    return x * jax.nn.sigmoid(x)


def _mod_rows(ref, seq_tiles):
    if seq_tiles:
        return ref[pl.ds(pl.program_id(0) // seq_tiles, 1), :]
    return ref[...]


def _ada_body(c_ref, w_ref, b_ref, ms_ref, mp_ref):
    c = c_ref[...]
    a = _silu(c).astype(BF16)
    y = _dot(a, w_ref[...].astype(BF16)) + b_ref[...]
    ns = ms_ref.shape[0]
    ms_ref[...] = y[:ns]
    mp_ref[...] = y[ns:]


def _ada(c_all, w_ada, b_ada, n_sample):
    rows, d = c_all.shape
    n = w_ada.shape[1]
    return pl.pallas_call(
        _ada_body,
        name="ada_mod",
        grid=(n // ADA_TN,),
        in_specs=[
            pl.BlockSpec((rows, d), lambda j: (0, 0)),
            pl.BlockSpec((d, ADA_TN), lambda j: (0, j)),
            pl.BlockSpec((1, ADA_TN), lambda j: (0, j)),
        ],
        out_specs=[
            pl.BlockSpec((n_sample, ADA_TN), lambda j: (0, j)),
            pl.BlockSpec((rows - n_sample, ADA_TN), lambda j: (0, j)),
        ],
        out_shape=[
            jax.ShapeDtypeStruct((n_sample, n), F32),
            jax.ShapeDtypeStruct((rows - n_sample, n), F32),
        ],
        compiler_params=pltpu.CompilerParams(
            dimension_semantics=("arbitrary",), vmem_limit_bytes=40 * MIB),
    )(c_all, w_ada, b_ada.reshape(1, n))


class _Group:
    def __init__(self, n_seq, n_rows, tm):
        if n_rows >= tm:
            assert n_rows % tm == 0
            self.sb, self.rb = 1, tm
        else:
            assert tm % n_rows == 0 and n_seq % (tm // n_rows) == 0
            self.sb, self.rb = tm // n_rows, n_rows
        self.n_seq, self.n_rows = n_seq, n_rows
        self.tiles_r = n_rows // self.rb
        self.tm = self.sb * self.rb
        self.n_tiles = (n_seq // self.sb) * self.tiles_r
        self.seq_tiles = self.tiles_r if self.sb == 1 else 0

    def x_spec(self, d, ngrid, **kw):
        tr = self.tiles_r
        if ngrid == 2:
            return pl.BlockSpec((self.sb, self.rb, d), lambda m, f: (m // tr, m % tr, 0), **kw)
        return pl.BlockSpec((self.sb, self.rb, d), lambda m: (m // tr, m % tr, 0), **kw)

    def mod_spec(self, mod, d, col, ngrid):
        if self.sb == 1:
            rows, row_blk = mod.shape[0], (lambda m: 0)
        else:
            rows, row_blk = self.sb, (lambda m: m)
        if ngrid == 2:
            return pl.BlockSpec((rows, d), lambda m, f: (row_blk(m), col))
        return pl.BlockSpec((rows, d), lambda m: (row_blk(m), col))


def _ffn_body(x_ref, sh_ref, sc_ref, gt_ref, wg_ref, wu_ref, wd_ref, lng_ref, lnb_ref,
              o_ref, h_sc, *, seq_tiles, alpha):
    f = pl.program_id(1)
    tm, d = o_ref.shape

    @pl.when(f == 0)
    def _():
        x = x_ref[...]
        h = x * (1.0 + _mod_rows(sc_ref, seq_tiles)[:, None, :]) + _mod_rows(sh_ref, seq_tiles)[:, None, :]
        h_sc[...] = h.reshape(tm, d).astype(BF16)
        o_ref[...] = jnp.zeros_like(o_ref)

    h = h_sc[...]
    g = _dot(h, wg_ref[...].astype(BF16))
    u = _dot(h, wu_ref[...].astype(BF16))
    a = (_silu(g) * u).astype(BF16)
    o_ref[...] += _dot(a, wd_ref[...].astype(BF16))

    @pl.when(f == pl.num_programs(1) - 1)
    def _():
        x = x_ref[...]
        gate = _mod_rows(gt_ref, seq_tiles)[:, None, :]
        y = alpha * x + (0.5 * gate) * o_ref[...].reshape(x.shape)
        o_ref[...] = _layer_norm(y.reshape(tm, d), lng_ref[...], lnb_ref[...])


def _ffn(x3, mod, sub, wg, wu, wd, layer, slot, ln_g, ln_b, alpha):
    n_seq, n_rows, d = x3.shape
    dff = wg.shape[-1]
    grp = _Group(n_seq, n_rows, FFN_TM)
    tm = grp.tm
    out = pl.pallas_call(
        functools.partial(_ffn_body, seq_tiles=grp.seq_tiles, alpha=alpha),
        name="swiglu_ln",
        grid=(grp.n_tiles, dff // FFN_TF),
        in_specs=[
            grp.x_spec(d, 2, pipeline_mode=pl.Buffered(1)),
            grp.mod_spec(mod, d, sub * 3 + 0, 2),
            grp.mod_spec(mod, d, sub * 3 + 1, 2),
            grp.mod_spec(mod, d, sub * 3 + 2, 2),
            pl.BlockSpec((None, None, d, FFN_TF), lambda m, f: (layer, slot, 0, f)),
            pl.BlockSpec((None, None, d, FFN_TF), lambda m, f: (layer, slot, 0, f)),
            pl.BlockSpec((None, None, FFN_TF, d), lambda m, f: (layer, slot, f, 0)),
            pl.BlockSpec((1, d), lambda m, f: (0, 0)),
            pl.BlockSpec((1, d), lambda m, f: (0, 0)),
        ],
        out_specs=pl.BlockSpec((tm, d), lambda m, f: (m, 0)),
        out_shape=jax.ShapeDtypeStruct((n_seq * n_rows, d), F32),
        scratch_shapes=[pltpu.VMEM((tm, d), BF16)],
        compiler_params=pltpu.CompilerParams(
            dimension_semantics=("arbitrary", "arbitrary"), vmem_limit_bytes=56 * MIB),
    )(x3, mod, mod, mod, wg, wu, wd, ln_g.reshape(1, d), ln_b.reshape(1, d))
    return out.reshape(n_seq, n_rows, d)


def _wprep_body(w_ref, o_ref, *, qkvz, gates):
    rows, n_in = w_ref.shape
    n_out = o_ref.shape[1]
    mlp = n_in - qkvz - gates
    o_ref[:, 0:qkvz] = w_ref[:, 0:qkvz].astype(BF16)
    o_ref[:, qkvz:qkvz + mlp] = w_ref[:, qkvz + gates:n_in].astype(BF16)
    tail = jnp.concatenate(
        [w_ref[:, qkvz:qkvz + gates], jnp.zeros((rows, n_out - qkvz - mlp - gates), F32)], axis=1)
    o_ref[:, qkvz + mlp:n_out] = tail.astype(BF16)


def _wprep(w_in, layer, qkvz, gates):
    _, d, n_in = w_in.shape
    assert n_in <= PROJ_N and (n_in - gates) % LANES == 0
    return pl.pallas_call(
        functools.partial(_wprep_body, qkvz=qkvz, gates=gates),
        name="proj_weight_prep",
        grid=(d // PREP_TR,),
        in_specs=[pl.BlockSpec((None, PREP_TR, n_in), lambda i: (layer, i, 0))],
        out_specs=pl.BlockSpec((PREP_TR, PROJ_N), lambda i: (i, 0)),
        out_shape=jax.ShapeDtypeStruct((d, PROJ_N), BF16),
        compiler_params=pltpu.CompilerParams(
            dimension_semantics=("arbitrary",), vmem_limit_bytes=40 * MIB),
    )(w_in)


def _proj_body(x_ref, sh_ref, sc_ref, w_ref, o_ref, h_sc, *, seq_tiles):
    tm, d = h_sc.shape

    @pl.when(pl.program_id(1) == 0)
    def _():
        x = x_ref[...]
        h = x * (1.0 + _mod_rows(sc_ref, seq_tiles)[:, None, :]) + _mod_rows(sh_ref, seq_tiles)[:, None, :]
        h_sc[...] = h.reshape(tm, d).astype(BF16)

    o_ref[...] = _dot(h_sc[...], w_ref[...])


def _proj(x3, mod, sub, w_cat):
    n_seq, n_rows, d = x3.shape
    n = w_cat.shape[1]
    grp = _Group(n_seq, n_rows, FFN_TM)
    tm = grp.tm
    return pl.pallas_call(
        functools.partial(_proj_body, seq_tiles=grp.seq_tiles),
        name="in_proj",
        grid=(grp.n_tiles, n // PROJ_TN),
        in_specs=[
            grp.x_spec(d, 2, pipeline_mode=pl.Buffered(1)),
            grp.mod_spec(mod, d, sub * 3 + 0, 2),
            grp.mod_spec(mod, d, sub * 3 + 1, 2),
            pl.BlockSpec((d, PROJ_TN), lambda m, j: (0, j)),
        ],
        out_specs=pl.BlockSpec((tm, PROJ_TN), lambda m, j: (m, j)),
        out_shape=jax.ShapeDtypeStruct((n_seq * n_rows, n), F32),
        scratch_shapes=[pltpu.VMEM((tm, d), BF16)],
        compiler_params=pltpu.CompilerParams(
            dimension_semantics=("arbitrary", "arbitrary"), vmem_limit_bytes=48 * MIB),
    )(x3, mod, mod, w_cat)


def _block_masks(n, blk):
    shift = int(math.log2(blk))
    i = lax.broadcasted_iota(jnp.int32, (n, n), 0)
    j = lax.broadcasted_iota(jnp.int32, (n, n), 1)
    same = (i >> shift) == (j >> shift)
    return i, j, same


def _gdn_body(*refs, blk, chained, tiles_per_seq, dk_scale):
    if chained:
        (qkv_ref, z_ref, ba_ref, cw_ref, gp_ref, ng_ref, o_ref, st_ref, xc_sc, act_sc) = refs
    else:
        (qkv_ref, z_ref, ba_ref, buf_ref, sin_ref, cw_ref, gp_ref, ng_ref, o_ref, st_ref,
         xc_sc, bs_sc, act_sc) = refs
    lt, cqkv = qkv_ref.shape
    hd = LANES
    nh = cqkv // (3 * hd)
    t = pl.program_id(0)
    halo = SUBLANES

    if chained:
        first = (t % tiles_per_seq) == 0

        @pl.when(first)
        def _():
            xc_sc[pl.ds(0, halo), :] = jnp.zeros((halo, cqkv), F32)
            st_ref[...] = jnp.zeros_like(st_ref)

        @pl.when(jnp.logical_not(first))
        def _():
            xc_sc[pl.ds(0, halo), :] = xc_sc[pl.ds(lt, halo), :]
    else:
        xc_sc[pl.ds(0, halo), :] = jnp.zeros((halo, cqkv), F32)
        bs_sc[pl.ds(0, lt), :] = buf_ref[...]
        bs_sc[pl.ds(lt, halo), :] = jnp.zeros((halo, cqkv), F32)
    xc_sc[pl.ds(halo, lt), :] = qkv_ref[...]

    cblk = 2 * LANES
    row_in_seq = lax.broadcasted_iota(jnp.int32, (lt, cblk), 0) & (blk - 1)
    for c0 in range(0, cqkv, cblk):
        cs = slice(c0, c0 + cblk)
        y = None
        for j in range(CONV_W):
            d = CONV_W - 1 - j
            term = xc_sc[pl.ds(halo - d, lt), cs]
            if not chained and d > 0:
                term = jnp.where(row_in_seq >= d, term, bs_sc[pl.ds(halo - d, lt), cs])
            term = cw_ref[j:j + 1, cs] * term
            y = term if y is None else y + term
        act_sc[:, cs] = _silu(y)

    ba = ba_ref[...]
    beta_all = jax.nn.sigmoid(ba)
    g_all = -jnp.exp(gp_ref[0:1, :]) * jax.nn.softplus(ba + gp_ref[1:2, :])

    i, j, same = _block_masks(lt, blk)
    incl = same & (i >= j)
    strict = same & (i > j)
    hi = lax.Precision.HIGHEST
    gc_all = _dot(incl.astype(F32), g_all, hi)
    gl_all = _dot((j == (i | (blk - 1))).astype(F32), gc_all, hi)
    gc_t = gc_all.T
    eye = (i == j).astype(F32)
    pair_masks = []
    for lvl in range(int(math.log2(blk))):
        pair_masks.append(((i >> (lvl + 1)) == (j >> (lvl + 1))) & ((i >> lvl) != (j >> lvl)))
    nblk = lt // blk

    heads = range(nh)
    hp = []
    for h in heads:
        q = act_sc[:, h * hd:(h + 1) * hd]
        k = act_sc[:, (nh + h) * hd:(nh + h + 1) * hd]
        v = act_sc[:, (2 * nh + h) * hd:(2 * nh + h + 1) * hd]
        q = q * lax.rsqrt(jnp.sum(q * q, axis=-1, keepdims=True) + RMS_EPS) * dk_scale
        k = k * lax.rsqrt(jnp.sum(k * k, axis=-1, keepdims=True) + RMS_EPS)
        beta = beta_all[:, h:h + 1]
        gcc = gc_all[:, nh + h:nh + h + 1]
        gcr = gc_t[nh + h:nh + h + 1, :]
        glc = gl_all[:, nh + h:nh + h + 1]
        decay = jnp.exp(jnp.where(incl, gcc - gcr, -jnp.inf))
        kb = k * beta
        k16 = k.astype(BF16)
        egc = jnp.exp(gcc)
        hp.append(dict(
            decay=decay, k16=k16, kb16=kb.astype(BF16), q16=q.astype(BF16), glc=glc,
            rhs16=jnp.concatenate([v * beta, kb * egc], axis=1).astype(BF16),
            qd=q * egc, kd=k * jnp.exp(glc - gcc)))

    ms = [_dot_nt(p["kb16"], p["k16"]) * jnp.where(strict, p["decay"], 0.0) for p in hp]
    dinvs = [eye - jnp.where(pair_masks[0], m, 0.0) for m in ms]
    for pm in pair_masks[1:]:
        d16s = [d.astype(BF16) for d in dinvs]
        t16s = [_dot(d16, jnp.where(pm, m, 0.0).astype(BF16)).astype(BF16) for d16, m in zip(d16s, ms)]
        dinvs = [d - _dot(t16, d16) for d, t16, d16 in zip(dinvs, t16s, d16s)]
    xs = [_dot(d.astype(BF16), p["rhs16"]) for d, p in zip(dinvs, hp)]
    qk16s = [(_dot_nt(p["q16"], p["k16"]) * p["decay"]).astype(BF16) for p in hp]

    vn_parts = [[] for _ in heads]
    os_parts = [[] for _ in heads]
    for b in range(nblk):
        rs = slice(b * blk, (b + 1) * blk)
        s0s = [st_ref[0, h] if chained else sin_ref[b, h] for h in heads]
        rr = [_dot(jnp.concatenate([x[rs, hd:], p["qd"][rs]], axis=0).astype(BF16), s0.astype(BF16))
              for x, p, s0 in zip(xs, hp, s0s)]
        vns = [x[rs, :hd] - r[:blk] for x, r in zip(xs, rr)]
        for h in heads:
            p = hp[h]
            s_new = (s0s[h] * jnp.exp(p["glc"][b * blk:b * blk + 1, :])
                     + _dot_tn(p["kd"][rs].astype(BF16), vns[h].astype(BF16)))
            if chained:
                st_ref[0, h] = s_new
            else:
                st_ref[b, h] = s_new
            vn_parts[h].append(vns[h])
            os_parts[h].append(rr[h][blk:])
    for h in heads:
        vn_all = jnp.concatenate(vn_parts[h], axis=0)
        o = jnp.concatenate(os_parts[h], axis=0) + _dot(qk16s[h], vn_all.astype(BF16))
        o = o * lax.rsqrt(jnp.mean(o * o, axis=-1, keepdims=True) + RMS_EPS) * ng_ref[...]
        o = o * _silu(z_ref[:, h * hd:(h + 1) * hd])
        o_ref[:, h * hd:(h + 1) * hd] = o.astype(BF16)


def _gdn(proj, conv_w, gate_par, norm_g, *, n_seq, n_rows, blk, ba_col, conv_buf=None, s0=None):
    t_rows = proj.shape[0]
    nh = DN_HEADS
    hd = LANES
    cqkv = 3 * nh * hd
    lt = SEQ_TILE
    assert lt % blk == 0 and blk & (blk - 1) == 0 and blk >= 2
    chained = s0 is None
    n_tiles = t_rows // lt
    common_in = [
        pl.BlockSpec((lt, cqkv), lambda t: (t, 0)),
        pl.BlockSpec((lt, nh * hd), lambda t: (t, 3)),
        pl.BlockSpec((lt, LANES), lambda t: (t, ba_col)),
    ]
    par_in = [
        pl.BlockSpec(conv_w.shape, lambda t: (0, 0)),
        pl.BlockSpec(gate_par.shape, lambda t: (0, 0)),
        pl.BlockSpec((1, hd), lambda t: (0, 0)),
    ]
    o_spec = pl.BlockSpec((lt, nh * hd), lambda t: (t, 0))
    o_shape = jax.ShapeDtypeStruct((t_rows, nh * hd), BF16)
    scratch = [pltpu.VMEM((lt + SUBLANES, cqkv), F32)]
    if chained:
        assert n_rows % lt == 0 and blk == DN_CHUNK
        tps = n_rows // lt
        in_specs = common_in + par_in
        args = (proj, proj, proj, conv_w, gate_par, norm_g)
        st_spec = pl.BlockSpec((1, nh, hd, hd), lambda t: (t // tps, 0, 0, 0))
    else:
        assert blk == n_rows and conv_buf is not None
        tps = 0
        spt = lt // n_rows
        in_specs = common_in + [
            pl.BlockSpec((lt, cqkv), lambda t: (t, 0)),
            pl.BlockSpec((spt, nh, hd, hd), lambda t: (t, 0, 0, 0)),
        ] + par_in
        args = (proj, proj, proj, conv_buf, s0, conv_w, gate_par, norm_g)
        st_spec = pl.BlockSpec((spt, nh, hd, hd), lambda t: (t, 0, 0, 0))
        scratch.append(pltpu.VMEM((lt + SUBLANES, cqkv), F32))
    scratch.append(pltpu.VMEM((lt, cqkv), F32))
    return pl.pallas_call(
        functools.partial(_gdn_body, blk=blk, chained=chained, tiles_per_seq=tps,
                          dk_scale=float(hd) ** -0.5),
        name="gdn_chained" if chained else "gdn_stateful",
        grid=(n_tiles,),
        in_specs=in_specs,
        out_specs=[o_spec, st_spec],
        out_shape=[o_shape, jax.ShapeDtypeStruct((n_seq, nh, hd, hd), F32)],
        scratch_shapes=scratch,
        compiler_params=pltpu.CompilerParams(
            dimension_semantics=("arbitrary",), vmem_limit_bytes=56 * MIB),
    )(*args)


def _mlp_body(u_ref, v_ref, ws_ref, bst_ref, lng_ref, lnb_ref, o_ref, vr_ref, *, blk):
    lt = u_ref.shape[0]
    gd = u_ref.shape[1] // MLP_GROUPS
    i, j, same = _block_masks(lt, blk)
    incl = same & (i >= j)
    hi = lax.Precision.HIGHEST
    if blk != lt:
        rep = (j == (i & (blk - 1))).astype(F32)
        bias_all = _dot(rep, bst_ref[...], hi)
    else:
        bias_all = bst_ref[...]
    for g in range(MLP_GROUPS):
        cs = slice(g * gd, (g + 1) * gd)
        uu = jax.nn.gelu(u_ref[:, cs])
        vv = _layer_norm(jax.nn.gelu(v_ref[:, cs]), lng_ref[:, cs], lnb_ref[:, cs])
        vr_ref[:, cs] = vv
        wsp = ws_ref[g]
        if blk != lt:
            wsp = _dot_nt(_dot(rep, wsp, hi), rep, hi)
        wsp = jnp.where(incl, wsp, 0.0).astype(BF16)
        s = _dot(wsp, vv.astype(BF16)) + bias_all[:, g:g + 1]
        o_ref[:, cs] = (uu * s).astype(BF16)


def _mlp(proj, w_spatial, b_spatial, ln_g, ln_b, *, blk):
    t_rows = proj.shape[0]
    lt = SEQ_TILE
    width = ln_g.size
    assert lt == MLP_CHUNK == w_spatial.shape[1] and lt % blk == 0
    u_col = (4 * DN_HEADS * LANES) // width
    bst = jnp.zeros((lt, LANES), F32).at[:, :MLP_GROUPS].set(b_spatial.T)
    return pl.pallas_call(
        functools.partial(_mlp_body, blk=blk),
        name="spatial_mlp",
        grid=(t_rows // lt,),
        in_specs=[
            pl.BlockSpec((lt, width), lambda t: (t, u_col)),
            pl.BlockSpec((lt, width), lambda t: (t, u_col + 1)),
            pl.BlockSpec(w_spatial.shape, lambda t: (0, 0, 0)),
            pl.BlockSpec((lt, LANES), lambda t: (0, 0)),
            pl.BlockSpec((1, width), lambda t: (0, 0)),
            pl.BlockSpec((1, width), lambda t: (0, 0)),
        ],
        out_specs=[
            pl.BlockSpec((lt, width), lambda t: (t, 0)),
            pl.BlockSpec((lt, width), lambda t: (t, 0)),
        ],
        out_shape=[
            jax.ShapeDtypeStruct((t_rows, width), BF16),
            jax.ShapeDtypeStruct((t_rows, width), F32),
        ],
        compiler_params=pltpu.CompilerParams(
            dimension_semantics=("arbitrary",), vmem_limit_bytes=32 * MIB),
    )(proj, proj, w_spatial, bst, ln_g.reshape(1, width), ln_b.reshape(1, width))


def _mix_body(od_ref, om_ref, x_ref, gt_ref, w_ref, lng_ref, lnb_ref, o_ref, w_sc, *, seq_tiles, alpha):
    tm, d = o_ref.shape
    kd = od_ref.shape[1]

    @pl.when(pl.program_id(0) == 0)
    def _():
        w_sc[...] = w_ref[...].astype(BF16)

    mix = _dot(od_ref[...], w_sc[:kd, :]) + _dot(om_ref[...], w_sc[kd:, :])
    x = x_ref[...]
    gate = _mod_rows(gt_ref, seq_tiles)[:, None, :]
    y = alpha * x + gate * mix.reshape(x.shape)
    o_ref[...] = _layer_norm(y.reshape(tm, d), lng_ref[...], lnb_ref[...])


def _mix(o_dn, o_mlp, x3, mod, sub, w_out, layer, ln_g, ln_b, alpha):
    n_seq, n_rows, d = x3.shape
    grp = _Group(n_seq, n_rows, MIX_TM)
    tm = grp.tm
    out = pl.pallas_call(
        functools.partial(_mix_body, seq_tiles=grp.seq_tiles, alpha=alpha),
        name="out_mix_ln",
        grid=(grp.n_tiles,),
        in_specs=[
            pl.BlockSpec((tm, o_dn.shape[1]), lambda m: (m, 0)),
            pl.BlockSpec((tm, o_mlp.shape[1]), lambda m: (m, 0)),
            grp.x_spec(d, 1),
            grp.mod_spec(mod, d, sub * 3 + 2, 1),
            pl.BlockSpec((None,) + w_out.shape[1:], lambda m: (layer, 0, 0), pipeline_mode=pl.Buffered(1)),
            pl.BlockSpec((1, d), lambda m: (0, 0)),
            pl.BlockSpec((1, d), lambda m: (0, 0)),
        ],
        out_specs=pl.BlockSpec((tm, d), lambda m: (m, 0)),
        out_shape=jax.ShapeDtypeStruct((n_seq * n_rows, d), F32),
        scratch_shapes=[pltpu.VMEM(w_out.shape[1:], BF16)],
        compiler_params=pltpu.CompilerParams(
            dimension_semantics=("arbitrary",), vmem_limit_bytes=56 * MIB),
    )(o_dn, o_mlp, x3, mod, w_out, ln_g.reshape(1, d), ln_b.reshape(1, d))
    return out.reshape(n_seq, n_rows, d)


def _trunk_layer(x3, mod, wts, layer, alpha, conv_buf, s0):
    n_seq, n_rows, d = x3.shape
    qkv_dim = 3 * DN_HEADS * LANES
    ffn_w = (wts["wg"], wts["wu"], wts["wd"], layer)
    x3 = _ffn(x3, mod, 0, *ffn_w, 0, wts["ln_g"][0], wts["ln_b"][0], alpha)
    proj = _proj(x3, mod, 1, wts["w_cat"])
    dn_blk = DN_CHUNK if n_rows % DN_CHUNK == 0 else n_rows
    gdn_kw = dict(n_seq=n_seq, n_rows=n_rows, blk=dn_blk, ba_col=wts["ba_col"])
    if s0 is None:
        o_dn, s_new = _gdn(proj, wts["conv_w"], wts["gate_par"], wts["dn_norm_g"], **gdn_kw)
    else:
        pad = jnp.zeros((n_seq, SUBLANES - (CONV_W - 1), qkv_dim), F32)
        buf = jnp.concatenate([pad, conv_buf], axis=1).reshape(n_seq * SUBLANES, qkv_dim)
        assert n_rows == SUBLANES
        o_dn, s_new = _gdn(proj, wts["conv_w"], wts["gate_par"], wts["dn_norm_g"],
                           conv_buf=buf, s0=s0, **gdn_kw)
    o_mlp, v_rows = _mlp(proj, wts["w_spatial"], wts["b_spatial"], wts["mlp_ln_g"], wts["mlp_ln_b"],
                         blk=min(n_rows, MLP_CHUNK))
    x3 = _mix(o_dn, o_mlp, x3, mod, 1, wts["w_out"], layer, wts["ln_g"][1], wts["ln_b"][1], alpha)
    x3 = _ffn(x3, mod, 2, *ffn_w, 1, wts["ln_g"][2], wts["ln_b"][2], alpha)
    new_buf = proj.reshape(n_seq, n_rows, -1)[:, n_rows - (CONV_W - 1):, :qkv_dim]
    return x3, new_buf, s_new, v_rows.reshape(n_seq, n_rows, -1)


def kernel(x_prompt, x_sample, c_prompt, c_sample, state_delta, state_conv, w_ada, b_ada, ln_g, ln_b,
           ffn_wg, ffn_wu, ffn_wd, w_in, conv_w, a_log, dt_bias, dn_norm_g, mlp_ln_g, mlp_ln_b,
           w_spatial, b_spatial, w_out):
    depth = w_ada.shape[0]
    bp, _, d = x_prompt.shape
    bs = x_sample.shape[0]
    nh = DN_HEADS
    alpha = (2.0 * depth) ** 0.25
    qkvz = 4 * nh * LANES
    gates = 2 * nh

    pad_rows = (-(bs + bp)) % SUBLANES
    c_all = jnp.concatenate([c_sample, c_prompt, jnp.zeros((pad_rows, d), F32)], axis=0)

    y_p, y_s = x_prompt, x_sample
    delta_p, conv_p, delta_s, conv_s, vrows_s = [], [], [], [], []
    for layer in range(depth):
        mod_s, mod_p = _ada(c_all, w_ada[layer], b_ada[layer], bs)
        gate_par = jnp.zeros((SUBLANES, LANES), F32)
        gate_par = gate_par.at[0, nh:gates].set(a_log[layer]).at[1, nh:gates].set(dt_bias[layer])
        wts = dict(
            wg=ffn_wg, wu=ffn_wu, wd=ffn_wd, ln_g=ln_g[layer], ln_b=ln_b[layer],
            w_cat=_wprep(w_in, layer, qkvz, gates), ba_col=(w_in.shape[-1] - gates) // LANES,
            conv_w=conv_w[layer], gate_par=gate_par,
            dn_norm_g=dn_norm_g[layer].reshape(1, LANES), mlp_ln_g=mlp_ln_g[layer], mlp_ln_b=mlp_ln_b[layer],
            w_spatial=w_spatial[layer], b_spatial=b_spatial[layer], w_out=w_out,
        )
        y_p, cb_p, ds_p, _ = _trunk_layer(y_p, mod_p, wts, layer, alpha, None, None)
        y_s, cb_s, ds_s, vr_s = _trunk_layer(y_s, mod_s, wts, layer, alpha, state_conv[layer],
                                             state_delta[layer])
        delta_p.append(ds_p)
        conv_p.append(cb_p)
        delta_s.append(ds_s)
        conv_s.append(cb_s)
        vrows_s.append(vr_s)
    return (y_p, y_s, jnp.stack(delta_p), jnp.stack(conv_p), jnp.stack(delta_s), jnp.stack(conv_s),
            jnp.stack(vrows_s))
```

```python
import functools
import math

import jax
import jax.numpy as jnp
from jax import lax
from jax.experimental import pallas as pl
from jax.experimental.pallas import tpu as pltpu

F32 = jnp.float32
BF16 = jnp.bfloat16

DN_HEADS = 8
DN_CHUNK = 64
CONV_W = 4
MLP_GROUPS = 4
MLP_CHUNK = 128
N_SUB = 3
LN_EPS = 1e-5
RMS_EPS = 1e-6

LANES = 128
SUBLANES = 8
MIB = 2 ** 20

FFN_TM = 1024
FFN_TF = 256
PROJ_TN = 1280
PROJ_N = 6400
PREP_TC = 256
MIX_TM = 512
ADA_TN = 1024
SEQ_TILE = 128
MLP_TILE = 512


def _dot(a, b, precision=None):
    return jnp.dot(a, b, preferred_element_type=F32, precision=precision)


def _dot_nt(a, b, precision=None):
    return lax.dot_general(a, b, (((1,), (1,)), ((), ())),
                           preferred_element_type=F32, precision=precision)


def _dot_tn(a, b):
    return lax.dot_general(a, b, (((0,), (0,)), ((), ())), preferred_element_type=F32)


def _layer_norm(y, g, b):
    mu = jnp.mean(y, axis=-1, keepdims=True)
    yc = y - mu
    var = jnp.mean(yc * yc, axis=-1, keepdims=True)
    return yc * lax.rsqrt(var + LN_EPS) * g + b


def _silu(x):
    return x * jax.nn.sigmoid(x)


def _mod_rows(ref, seq_tiles):
    if seq_tiles:
        return ref[pl.ds(pl.program_id(0) // seq_tiles, 1), :]
    return ref[...]


def _ada_body(c_ref, w_ref, b_ref, ms_ref, mp_ref):
    c = c_ref[...]
    a = _silu(c).astype(BF16)
    y = _dot(a, w_ref[...].astype(BF16)) + b_ref[...]
    ns = ms_ref.shape[0]
    ms_ref[...] = y[:ns]
    mp_ref[...] = y[ns:]


def _ada(c_all, w_ada, b_ada, n_sample):
    rows, d = c_all.shape
    n = w_ada.shape[1]
    return pl.pallas_call(
        _ada_body,
        name="ada_mod",
        grid=(n // ADA_TN,),
        in_specs=[
            pl.BlockSpec((rows, d), lambda j: (0, 0)),
            pl.BlockSpec((d, ADA_TN), lambda j: (0, j)),
            pl.BlockSpec((1, ADA_TN), lambda j: (0, j)),
        ],
        out_specs=[
            pl.BlockSpec((n_sample, ADA_TN), lambda j: (0, j)),
            pl.BlockSpec((rows - n_sample, ADA_TN), lambda j: (0, j)),
        ],
        out_shape=[
            jax.ShapeDtypeStruct((n_sample, n), F32),
            jax.ShapeDtypeStruct((rows - n_sample, n), F32),
        ],
        compiler_params=pltpu.CompilerParams(
            dimension_semantics=("arbitrary",), vmem_limit_bytes=40 * MIB),
    )(c_all, w_ada, b_ada.reshape(1, n))


class _Group:
    def __init__(self, n_seq, n_rows, tm):
        if n_rows >= tm:
            assert n_rows % tm == 0
            self.sb, self.rb = 1, tm
        else:
            assert tm % n_rows == 0 and n_seq % (tm // n_rows) == 0
            self.sb, self.rb = tm // n_rows, n_rows
        self.n_seq, self.n_rows = n_seq, n_rows
        self.tiles_r = n_rows // self.rb
        self.tm = self.sb * self.rb
        self.n_tiles = (n_seq // self.sb) * self.tiles_r
        self.seq_tiles = self.tiles_r if self.sb == 1 else 0

    def x_spec(self, d, ngrid, **kw):
        tr = self.tiles_r
        if ngrid == 2:
            return pl.BlockSpec((self.sb, self.rb, d), lambda m, f: (m // tr, m % tr, 0), **kw)
        return pl.BlockSpec((self.sb, self.rb, d), lambda m: (m // tr, m % tr, 0), **kw)

    def mod_spec(self, mod, d, col, ngrid):
        if self.sb == 1:
            rows, row_blk = mod.shape[0], (lambda m: 0)
        else:
            rows, row_blk = self.sb, (lambda m: m)
        if ngrid == 2:
            return pl.BlockSpec((rows, d), lambda m, f: (row_blk(m), col))
        return pl.BlockSpec((rows, d), lambda m: (row_blk(m), col))


def _ffn_body(x_ref, sh_ref, sc_ref, gt_ref, wg_ref, wu_ref, wd_ref, lng_ref, lnb_ref,
              o_ref, h_sc, *, seq_tiles, alpha):
    f = pl.program_id(1)
    tm, d = o_ref.shape

    @pl.when(f == 0)
    def _():
        x = x_ref[...]
        h = x * (1.0 + _mod_rows(sc_ref, seq_tiles)[:, None, :]) + _mod_rows(sh_ref, seq_tiles)[:, None, :]
        h_sc[...] = h.reshape(tm, d).astype(BF16)
        o_ref[...] = jnp.zeros_like(o_ref)

    h = h_sc[...]
    g = _dot(h, wg_ref[...].astype(BF16))
    u = _dot(h, wu_ref[...].astype(BF16))
    a = (_silu(g) * u).astype(BF16)
    o_ref[...] += _dot(a, wd_ref[...].astype(BF16))

    @pl.when(f == pl.num_programs(1) - 1)
    def _():
        x = x_ref[...]
        gate = _mod_rows(gt_ref, seq_tiles)[:, None, :]
        y = alpha * x + (0.5 * gate) * o_ref[...].reshape(x.shape)
        o_ref[...] = _layer_norm(y.reshape(tm, d), lng_ref[...], lnb_ref[...])


def _ffn(x3, mod, sub, wg, wu, wd, layer, slot, ln_g, ln_b, alpha):
    n_seq, n_rows, d = x3.shape
    dff = wg.shape[-1]
    grp = _Group(n_seq, n_rows, FFN_TM)
    tm = grp.tm
    out = pl.pallas_call(
        functools.partial(_ffn_body, seq_tiles=grp.seq_tiles, alpha=alpha),
        name="swiglu_ln",
        grid=(grp.n_tiles, dff // FFN_TF),
        in_specs=[
            grp.x_spec(d, 2, pipeline_mode=pl.Buffered(1)),
            grp.mod_spec(mod, d, sub * 3 + 0, 2),
            grp.mod_spec(mod, d, sub * 3 + 1, 2),
            grp.mod_spec(mod, d, sub * 3 + 2, 2),
            pl.BlockSpec((None, None, d, FFN_TF), lambda m, f: (layer, slot, 0, f)),
            pl.BlockSpec((None, None, d, FFN_TF), lambda m, f: (layer, slot, 0, f)),
            pl.BlockSpec((None, None, FFN_TF, d), lambda m, f: (layer, slot, f, 0)),
            pl.BlockSpec((1, d), lambda m, f: (0, 0)),
            pl.BlockSpec((1, d), lambda m, f: (0, 0)),
        ],
        out_specs=pl.BlockSpec((tm, d), lambda m, f: (m, 0)),
        out_shape=jax.ShapeDtypeStruct((n_seq * n_rows, d), F32),
        scratch_shapes=[pltpu.VMEM((tm, d), BF16)],
        compiler_params=pltpu.CompilerParams(
            dimension_semantics=("arbitrary", "arbitrary"), vmem_limit_bytes=56 * MIB),
    )(x3, mod, mod, mod, wg, wu, wd, ln_g.reshape(1, d), ln_b.reshape(1, d))
    return out.reshape(n_seq, n_rows, d)


def _wprep_body(w_ref, o_ref, *, qkvz, gates):
    n_in, cols = w_ref.shape
    n_out = o_ref.shape[0]
    mlp = n_in - qkvz - gates
    o_ref[0:qkvz, :] = w_ref[0:qkvz, :].astype(BF16)
    o_ref[qkvz:qkvz + mlp, :] = w_ref[qkvz + gates:n_in, :].astype(BF16)
    o_ref[qkvz + mlp:n_in, :] = w_ref[qkvz:qkvz + gates, :].astype(BF16)
    o_ref[n_in:n_out, :] = jnp.zeros((n_out - n_in, cols), BF16)


def _wprep(w_in_t, layer, qkvz, gates):
    _, n_in, d = w_in_t.shape
    assert n_in <= PROJ_N and (n_in - gates) % LANES == 0
    return pl.pallas_call(
        functools.partial(_wprep_body, qkvz=qkvz, gates=gates),
        name="proj_weight_prep",
        grid=(d // PREP_TC,),
        in_specs=[pl.BlockSpec((None, n_in, PREP_TC), lambda i: (layer, 0, i))],
        out_specs=pl.BlockSpec((PROJ_N, PREP_TC), lambda i: (0, i)),
        out_shape=jax.ShapeDtypeStruct((PROJ_N, d), BF16),
        compiler_params=pltpu.CompilerParams(
            dimension_semantics=("arbitrary",), vmem_limit_bytes=40 * MIB),
    )(w_in_t)


def _proj_body(x_ref, sh_ref, sc_ref, w_ref, o_ref, h_sc, *, seq_tiles):
    tm, d = h_sc.shape

    @pl.when(pl.program_id(1) == 0)
    def _():
        x = x_ref[...]
        h = x * (1.0 + _mod_rows(sc_ref, seq_tiles)[:, None, :]) + _mod_rows(sh_ref, seq_tiles)[:, None, :]
        h_sc[...] = h.reshape(tm, d).astype(BF16)

    o_ref[...] = _dot_nt(h_sc[...], w_ref[...])


def _proj(x3, mod, sub, w_cat_t):
    n_seq, n_rows, d = x3.shape
    n = w_cat_t.shape[0]
    grp = _Group(n_seq, n_rows, FFN_TM)
    tm = grp.tm
    return pl.pallas_call(
        functools.partial(_proj_body, seq_tiles=grp.seq_tiles),
        name="in_proj",
        grid=(grp.n_tiles, n // PROJ_TN),
        in_specs=[
            grp.x_spec(d, 2, pipeline_mode=pl.Buffered(1)),
            grp.mod_spec(mod, d, sub * 3 + 0, 2),
            grp.mod_spec(mod, d, sub * 3 + 1, 2),
            pl.BlockSpec((PROJ_TN, d), lambda m, j: (j, 0)),
        ],
        out_specs=pl.BlockSpec((tm, PROJ_TN), lambda m, j: (m, j)),
        out_shape=jax.ShapeDtypeStruct((n_seq * n_rows, n), F32),
        scratch_shapes=[pltpu.VMEM((tm, d), BF16)],
        compiler_params=pltpu.CompilerParams(
            dimension_semantics=("arbitrary", "arbitrary"), vmem_limit_bytes=48 * MIB),
    )(x3, mod, mod, w_cat_t)


def _block_masks(n, blk):
    shift = int(math.log2(blk))
    i = lax.broadcasted_iota(jnp.int32, (n, n), 0)
    j = lax.broadcasted_iota(jnp.int32, (n, n), 1)
    same = (i >> shift) == (j >> shift)
    return i, j, same


def _gdn_body(*refs, blk, chained, tiles_per_seq, dk_scale):
    if chained:
        (qkv_ref, z_ref, ba_ref, cw_ref, gp_ref, ng_ref, o_ref, st_ref, xc_sc, act_sc) = refs
    else:
        (qkv_ref, z_ref, ba_ref, buf_ref, sin_ref, cw_ref, gp_ref, ng_ref, o_ref, st_ref,
         xc_sc, bs_sc, act_sc) = refs
    lt, cqkv = qkv_ref.shape
    hd = LANES
    nh = cqkv // (3 * hd)
    t = pl.program_id(0)
    halo = SUBLANES

    if chained:
        first = (t % tiles_per_seq) == 0

        @pl.when(first)
        def _():
            xc_sc[pl.ds(0, halo), :] = jnp.zeros((halo, cqkv), F32)
            st_ref[...] = jnp.zeros_like(st_ref)

        @pl.when(jnp.logical_not(first))
        def _():
            xc_sc[pl.ds(0, halo), :] = xc_sc[pl.ds(lt, halo), :]
    else:
        xc_sc[pl.ds(0, halo), :] = jnp.zeros((halo, cqkv), F32)
        bs_sc[pl.ds(0, lt), :] = buf_ref[...]
        bs_sc[pl.ds(lt, halo), :] = jnp.zeros((halo, cqkv), F32)
    xc_sc[pl.ds(halo, lt), :] = qkv_ref[...]

    cblk = 2 * LANES
    row_in_seq = lax.broadcasted_iota(jnp.int32, (lt, cblk), 0) & (blk - 1)
    for c0 in range(0, cqkv, cblk):
        cs = slice(c0, c0 + cblk)
        y = None
        for j in range(CONV_W):
            d = CONV_W - 1 - j
            term = xc_sc[pl.ds(halo - d, lt), cs]
            if not chained and d > 0:
                term = jnp.where(row_in_seq >= d, term, bs_sc[pl.ds(halo - d, lt), cs])
            term = cw_ref[j:j + 1, cs] * term
            y = term if y is None else y + term
        act_sc[:, cs] = _silu(y)

    ba = ba_ref[...]
    beta_all = jax.nn.sigmoid(ba)
    g_all = -jnp.exp(gp_ref[0:1, :]) * jax.nn.softplus(ba + gp_ref[1:2, :])

    i, j, same = _block_masks(lt, blk)
    incl = same & (i >= j)
    strict = same & (i > j)
    hi = lax.Precision.HIGHEST
    gc_all = _dot(incl.astype(F32), g_all, hi)
    gl_all = _dot((j == (i | (blk - 1))).astype(F32), gc_all, hi)
    gc_t = gc_all.T
    eye = (i == j).astype(F32)
    pair_masks = []
    for lvl in range(int(math.log2(blk))):
        pair_masks.append(((i >> (lvl + 1)) == (j >> (lvl + 1))) & ((i >> lvl) != (j >> lvl)))
    nblk = lt // blk

    heads = range(nh)
    hp = []
    for h in heads:
        q = act_sc[:, h * hd:(h + 1) * hd]
        k = act_sc[:, (nh + h) * hd:(nh + h + 1) * hd]
        v = act_sc[:, (2 * nh + h) * hd:(2 * nh + h + 1) * hd]
        q = q * lax.rsqrt(jnp.sum(q * q, axis=-1, keepdims=True) + RMS_EPS) * dk_scale
        k = k * lax.rsqrt(jnp.sum(k * k, axis=-1, keepdims=True) + RMS_EPS)
        beta = beta_all[:, h:h + 1]
        gcc = gc_all[:, nh + h:nh + h + 1]
        gcr = gc_t[nh + h:nh + h + 1, :]
        glc = gl_all[:, nh + h:nh + h + 1]
        decay = jnp.exp(jnp.where(incl, gcc - gcr, -jnp.inf))
        kb = k * beta
        k16 = k.astype(BF16)
        egc = jnp.exp(gcc)
        hp.append(dict(
            decay=decay, k16=k16, kb16=kb.astype(BF16), q16=q.astype(BF16), glc=glc,
            rhs16=jnp.concatenate([v * beta, kb * egc], axis=1).astype(BF16),
            qd=q * egc, kd=k * jnp.exp(glc - gcc)))

    ms = [_dot_nt(p["kb16"], p["k16"]) * jnp.where(strict, p["decay"], 0.0) for p in hp]
    dinvs = [eye - jnp.where(pair_masks[0], m, 0.0) for m in ms]
    for pm in pair_masks[1:]:
        d16s = [d.astype(BF16) for d in dinvs]
        t16s = [_dot(d16, jnp.where(pm, m, 0.0).astype(BF16)).astype(BF16) for d16, m in zip(d16s, ms)]
        dinvs = [d - _dot(t16, d16) for d, t16, d16 in zip(dinvs, t16s, d16s)]
    xs = [_dot(d.astype(BF16), p["rhs16"]) for d, p in zip(dinvs, hp)]
    qk16s = [(_dot_nt(p["q16"], p["k16"]) * p["decay"]).astype(BF16) for p in hp]

    vn_parts = [[] for _ in heads]
    os_parts = [[] for _ in heads]
    for b in range(nblk):
        rs = slice(b * blk, (b + 1) * blk)
        s0s = [st_ref[0, h] if chained else sin_ref[b, h] for h in heads]
        rr = [_dot(jnp.concatenate([x[rs, hd:], p["qd"][rs]], axis=0).astype(BF16), s0.astype(BF16))
              for x, p, s0 in zip(xs, hp, s0s)]
        vns = [x[rs, :hd] - r[:blk] for x, r in zip(xs, rr)]
        for h in heads:
            p = hp[h]
            s_new = (s0s[h] * jnp.exp(p["glc"][b * blk:b * blk + 1, :])
                     + _dot_tn(p["kd"][rs].astype(BF16), vns[h].astype(BF16)))
            if chained:
                st_ref[0, h] = s_new
            else:
                st_ref[b, h] = s_new
            vn_parts[h].append(vns[h])
            os_parts[h].append(rr[h][blk:])
    for h in heads:
        vn_all = jnp.concatenate(vn_parts[h], axis=0)
        o = jnp.concatenate(os_parts[h], axis=0) + _dot(qk16s[h], vn_all.astype(BF16))
        o = o * lax.rsqrt(jnp.mean(o * o, axis=-1, keepdims=True) + RMS_EPS) * ng_ref[...]
        o = o * _silu(z_ref[:, h * hd:(h + 1) * hd])
        o_ref[:, h * hd:(h + 1) * hd] = o.astype(BF16)


def _gdn(proj, conv_w, gate_par, norm_g, *, n_seq, n_rows, blk, ba_col, conv_buf=None, s0=None):
    t_rows = proj.shape[0]
    nh = DN_HEADS
    hd = LANES
    cqkv = 3 * nh * hd
    lt = SEQ_TILE
    assert lt % blk == 0 and blk & (blk - 1) == 0 and blk >= 2
    chained = s0 is None
    n_tiles = t_rows // lt
    common_in = [
        pl.BlockSpec((lt, cqkv), lambda t: (t, 0)),
        pl.BlockSpec((lt, nh * hd), lambda t: (t, 3)),
        pl.BlockSpec((lt, LANES), lambda t: (t, ba_col)),
    ]
    par_in = [
        pl.BlockSpec(conv_w.shape, lambda t: (0, 0)),
        pl.BlockSpec(gate_par.shape, lambda t: (0, 0)),
        pl.BlockSpec((1, hd), lambda t: (0, 0)),
    ]
    o_spec = pl.BlockSpec((lt, nh * hd), lambda t: (t, 0))
    o_shape = jax.ShapeDtypeStruct((t_rows, nh * hd), BF16)
    scratch = [pltpu.VMEM((lt + SUBLANES, cqkv), F32)]
    if chained:
        assert n_rows % lt == 0 and blk == DN_CHUNK
        tps = n_rows // lt
        in_specs = common_in + par_in
        args = (proj, proj, proj, conv_w, gate_par, norm_g)
        st_spec = pl.BlockSpec((1, nh, hd, hd), lambda t: (t // tps, 0, 0, 0))
    else:
        assert blk == n_rows and conv_buf is not None
        tps = 0
        spt = lt // n_rows
        in_specs = common_in + [
            pl.BlockSpec((lt, cqkv), lambda t: (t, 0)),
            pl.BlockSpec((spt, nh, hd, hd), lambda t: (t, 0, 0, 0)),
        ] + par_in
        args = (proj, proj, proj, conv_buf, s0, conv_w, gate_par, norm_g)
        st_spec = pl.BlockSpec((spt, nh, hd, hd), lambda t: (t, 0, 0, 0))
        scratch.append(pltpu.VMEM((lt + SUBLANES, cqkv), F32))
    scratch.append(pltpu.VMEM((lt, cqkv), F32))
    return pl.pallas_call(
        functools.partial(_gdn_body, blk=blk, chained=chained, tiles_per_seq=tps,
                          dk_scale=float(hd) ** -0.5),
        name="gdn_chained" if chained else "gdn_stateful",
        grid=(n_tiles,),
        in_specs=in_specs,
        out_specs=[o_spec, st_spec],
        out_shape=[o_shape, jax.ShapeDtypeStruct((n_seq, nh, hd, hd), F32)],
        scratch_shapes=scratch,
        compiler_params=pltpu.CompilerParams(
            dimension_semantics=("arbitrary",), vmem_limit_bytes=56 * MIB),
    )(*args)


def _mlp_body(u_ref, v_ref, ws_ref, bst_ref, lng_ref, lnb_ref, o_ref, *maybe_vr_ref, blk):
    ck = ws_ref.shape[1]
    gd = u_ref.shape[1] // MLP_GROUPS
    i, j, same = _block_masks(ck, blk)
    incl = same & (i >= j)
    hi = lax.Precision.HIGHEST
    if blk != ck:
        rep = (j == (i & (blk - 1))).astype(F32)
        bias_all = _dot(rep, bst_ref[...], hi)
    else:
        bias_all = bst_ref[...]
    for g in range(MLP_GROUPS):
        cs = slice(g * gd, (g + 1) * gd)
        wsp = ws_ref[g]
        if blk != ck:
            wsp = _dot_nt(_dot(rep, wsp, hi), rep, hi)
        wsp = jnp.where(incl, wsp, 0.0).astype(BF16)
        for c in range(u_ref.shape[0] // ck):
            rs = slice(c * ck, (c + 1) * ck)
            uu = jax.nn.gelu(u_ref[rs, cs])
            vv = _layer_norm(jax.nn.gelu(v_ref[rs, cs]), lng_ref[:, cs], lnb_ref[:, cs])
            if maybe_vr_ref:
                maybe_vr_ref[0][rs, cs] = vv
            s = _dot(wsp, vv.astype(BF16)) + bias_all[:, g:g + 1]
            o_ref[rs, cs] = (uu * s).astype(BF16)


def _mlp(proj, w_spatial, b_spatial, ln_g, ln_b, *, blk, want_v_rows):
    t_rows = proj.shape[0]
    lt = MLP_TILE
    ck = w_spatial.shape[1]
    width = ln_g.size
    assert ck == MLP_CHUNK and lt % ck == 0 and ck % blk == 0 and t_rows % lt == 0
    u_col = (4 * DN_HEADS * LANES) // width
    bst = jnp.zeros((ck, LANES), F32).at[:, :MLP_GROUPS].set(b_spatial.T)
    o_spec = pl.BlockSpec((lt, width), lambda t: (t, 0))
    out_specs = [o_spec]
    out_shape = [jax.ShapeDtypeStruct((t_rows, width), BF16)]
    if want_v_rows:
        out_specs.append(o_spec)
        out_shape.append(jax.ShapeDtypeStruct((t_rows, width), F32))
    return pl.pallas_call(
        functools.partial(_mlp_body, blk=blk),
        name="spatial_mlp",
        grid=(t_rows // lt,),
        in_specs=[
            pl.BlockSpec((lt, width), lambda t: (t, u_col)),
            pl.BlockSpec((lt, width), lambda t: (t, u_col + 1)),
            pl.BlockSpec(w_spatial.shape, lambda t: (0, 0, 0)),
            pl.BlockSpec((ck, LANES), lambda t: (0, 0)),
            pl.BlockSpec((1, width), lambda t: (0, 0)),
            pl.BlockSpec((1, width), lambda t: (0, 0)),
        ],
        out_specs=out_specs,
        out_shape=out_shape,
        compiler_params=pltpu.CompilerParams(
            dimension_semantics=("arbitrary",), vmem_limit_bytes=32 * MIB),
    )(proj, proj, w_spatial, bst, ln_g.reshape(1, width), ln_b.reshape(1, width))


def _mix_body(od_ref, om_ref, x_ref, gt_ref, w_ref, lng_ref, lnb_ref, o_ref, w_sc, *, seq_tiles, alpha):
    tm, d = o_ref.shape
    kd = od_ref.shape[1]

    @pl.when(pl.program_id(0) == 0)
    def _():
        w_sc[...] = w_ref[...].astype(BF16)

    mix = _dot(od_ref[...], w_sc[:kd, :]) + _dot(om_ref[...], w_sc[kd:, :])
    x = x_ref[...]
    gate = _mod_rows(gt_ref, seq_tiles)[:, None, :]
    y = alpha * x + gate * mix.reshape(x.shape)
    o_ref[...] = _layer_norm(y.reshape(tm, d), lng_ref[...], lnb_ref[...])


def _mix(o_dn, o_mlp, x3, mod, sub, w_out, layer, ln_g, ln_b, alpha):
    n_seq, n_rows, d = x3.shape
    grp = _Group(n_seq, n_rows, MIX_TM)
    tm = grp.tm
    out = pl.pallas_call(
        functools.partial(_mix_body, seq_tiles=grp.seq_tiles, alpha=alpha),
        name="out_mix_ln",
        grid=(grp.n_tiles,),
        in_specs=[
            pl.BlockSpec((tm, o_dn.shape[1]), lambda m: (m, 0)),
            pl.BlockSpec((tm, o_mlp.shape[1]), lambda m: (m, 0)),
            grp.x_spec(d, 1),
            grp.mod_spec(mod, d, sub * 3 + 2, 1),
            pl.BlockSpec((None,) + w_out.shape[1:], lambda m: (layer, 0, 0), pipeline_mode=pl.Buffered(1)),
            pl.BlockSpec((1, d), lambda m: (0, 0)),
            pl.BlockSpec((1, d), lambda m: (0, 0)),
        ],
        out_specs=pl.BlockSpec((tm, d), lambda m: (m, 0)),
        out_shape=jax.ShapeDtypeStruct((n_seq * n_rows, d), F32),
        scratch_shapes=[pltpu.VMEM(w_out.shape[1:], BF16)],
        compiler_params=pltpu.CompilerParams(
            dimension_semantics=("arbitrary",), vmem_limit_bytes=56 * MIB),
    )(o_dn, o_mlp, x3, mod, w_out, ln_g.reshape(1, d), ln_b.reshape(1, d))
    return out.reshape(n_seq, n_rows, d)


def _trunk_layer(x3, mod, wts, layer, alpha, conv_buf, s0):
    n_seq, n_rows, d = x3.shape
    qkv_dim = 3 * DN_HEADS * LANES
    ffn_w = (wts["wg"], wts["wu"], wts["wd"], layer)
    x3 = _ffn(x3, mod, 0, *ffn_w, 0, wts["ln_g"][0], wts["ln_b"][0], alpha)
    proj = _proj(x3, mod, 1, wts["w_cat"])
    dn_blk = DN_CHUNK if n_rows % DN_CHUNK == 0 else n_rows
    gdn_kw = dict(n_seq=n_seq, n_rows=n_rows, blk=dn_blk, ba_col=wts["ba_col"])
    if s0 is None:
        o_dn, s_new = _gdn(proj, wts["conv_w"], wts["gate_par"], wts["dn_norm_g"], **gdn_kw)
    else:
        pad = jnp.zeros((n_seq, SUBLANES - (CONV_W - 1), qkv_dim), F32)
        buf = jnp.concatenate([pad, conv_buf], axis=1).reshape(n_seq * SUBLANES, qkv_dim)
        assert n_rows == SUBLANES
        o_dn, s_new = _gdn(proj, wts["conv_w"], wts["gate_par"], wts["dn_norm_g"],
                           conv_buf=buf, s0=s0, **gdn_kw)
    mlp_out = _mlp(proj, wts["w_spatial"], wts["b_spatial"], wts["mlp_ln_g"], wts["mlp_ln_b"],
                   blk=min(n_rows, MLP_CHUNK), want_v_rows=s0 is not None)
    o_mlp = mlp_out[0]
    v_rows = mlp_out[1].reshape(n_seq, n_rows, -1) if s0 is not None else None
    x3 = _mix(o_dn, o_mlp, x3, mod, 1, wts["w_out"], layer, wts["ln_g"][1], wts["ln_b"][1], alpha)
    x3 = _ffn(x3, mod, 2, *ffn_w, 1, wts["ln_g"][2], wts["ln_b"][2], alpha)
    new_buf = proj.reshape(n_seq, n_rows, -1)[:, n_rows - (CONV_W - 1):, :qkv_dim]
    return x3, new_buf, s_new, v_rows


def kernel(x_prompt, x_sample, c_prompt, c_sample, state_delta, state_conv, w_ada, b_ada, ln_g, ln_b,
           ffn_wg, ffn_wu, ffn_wd, w_in, conv_w, a_log, dt_bias, dn_norm_g, mlp_ln_g, mlp_ln_b,
           w_spatial, b_spatial, w_out):
    depth = w_ada.shape[0]
    bp, _, d = x_prompt.shape
    bs = x_sample.shape[0]
    nh = DN_HEADS
    alpha = (2.0 * depth) ** 0.25
    qkvz = 4 * nh * LANES
    gates = 2 * nh

    pad_rows = (-(bs + bp)) % SUBLANES
    c_all = jnp.concatenate([c_sample, c_prompt, jnp.zeros((pad_rows, d), F32)], axis=0)
    w_in_t = jnp.swapaxes(w_in, 1, 2)

    y_p, y_s = x_prompt, x_sample
    delta_p, conv_p, delta_s, conv_s, vrows_s = [], [], [], [], []
    for layer in range(depth):
        mod_s, mod_p = _ada(c_all, w_ada[layer], b_ada[layer], bs)
        gate_par = jnp.zeros((SUBLANES, LANES), F32)
        gate_par = gate_par.at[0, nh:gates].set(a_log[layer]).at[1, nh:gates].set(dt_bias[layer])
        wts = dict(
            wg=ffn_wg, wu=ffn_wu, wd=ffn_wd, ln_g=ln_g[layer], ln_b=ln_b[layer],
            w_cat=_wprep(w_in_t, layer, qkvz, gates), ba_col=(w_in.shape[-1] - gates) // LANES,
            conv_w=conv_w[layer], gate_par=gate_par,
            dn_norm_g=dn_norm_g[layer].reshape(1, LANES), mlp_ln_g=mlp_ln_g[layer], mlp_ln_b=mlp_ln_b[layer],
            w_spatial=w_spatial[layer], b_spatial=b_spatial[layer], w_out=w_out,
        )
        y_p, cb_p, ds_p, _ = _trunk_layer(y_p, mod_p, wts, layer, alpha, None, None)
        y_s, cb_s, ds_s, vr_s = _trunk_layer(y_s, mod_s, wts, layer, alpha, state_conv[layer],
                                             state_delta[layer])
        delta_p.append(ds_p)
        conv_p.append(cb_p)
        delta_s.append(ds_s)
        conv_s.append(cb_s)
        vrows_s.append(vr_s)
    return (y_p, y_s, jnp.stack(delta_p), jnp.stack(conv_p), jnp.stack(delta_s), jnp.stack(conv_s),
            jnp.stack(vrows_s))
```

```python
import functools
import math

import jax
import jax.numpy as jnp
from jax import lax
from jax.experimental import pallas as pl
from jax.experimental.pallas import tpu as pltpu

F32 = jnp.float32
BF16 = jnp.bfloat16

DN_HEADS = 8
DN_CHUNK = 64
CONV_W = 4
MLP_GROUPS = 4
MLP_CHUNK = 128
N_SUB = 3
LN_EPS = 1e-5
RMS_EPS = 1e-6

LANES = 128
SUBLANES = 8
MIB = 2 ** 20

FFN_TM = 1024
FFN_TF = 256
FFN_RC = 256
PROJ_TN = 1280
PROJ_N = 6400
PREP_TC = 256
MIX_TM = 512
ADA_TN = 1024
SEQ_TILE = 128
MLP_TILE = 512


def _dot(a, b, precision=None):
    return jnp.dot(a, b, preferred_element_type=F32, precision=precision)


def _dot_nt(a, b, precision=None):
    return lax.dot_general(a, b, (((1,), (1,)), ((), ())),
                           preferred_element_type=F32, precision=precision)


def _dot_tn(a, b):
    return lax.dot_general(a, b, (((0,), (0,)), ((), ())), preferred_element_type=F32)


def _layer_norm(y, g, b):
    mu = jnp.mean(y, axis=-1, keepdims=True)
    yc = y - mu
    var = jnp.mean(yc * yc, axis=-1, keepdims=True)
    return yc * lax.rsqrt(var + LN_EPS) * g + b


def _silu(x):
    return x * jax.nn.sigmoid(x)


def _mod_rows(ref, seq_tiles):
    if seq_tiles:
        return ref[pl.ds(pl.program_id(0) // seq_tiles, 1), :]
    return ref[...]


def _ada_body(c_ref, w_ref, b_ref, ms_ref, mp_ref):
    c = c_ref[...]
    a = _silu(c).astype(BF16)
    y = _dot(a, w_ref[...].astype(BF16)) + b_ref[...]
    ns = ms_ref.shape[0]
    ms_ref[...] = y[:ns]
    mp_ref[...] = y[ns:]


def _ada(c_all, w_ada, b_ada, n_sample):
    rows, d = c_all.shape
    n = w_ada.shape[1]
    return pl.pallas_call(
        _ada_body,
        name="ada_mod",
        grid=(n // ADA_TN,),
        in_specs=[
            pl.BlockSpec((rows, d), lambda j: (0, 0)),
            pl.BlockSpec((d, ADA_TN), lambda j: (0, j)),
            pl.BlockSpec((1, ADA_TN), lambda j: (0, j)),
        ],
        out_specs=[
            pl.BlockSpec((n_sample, ADA_TN), lambda j: (0, j)),
            pl.BlockSpec((rows - n_sample, ADA_TN), lambda j: (0, j)),
        ],
        out_shape=[
            jax.ShapeDtypeStruct((n_sample, n), F32),
            jax.ShapeDtypeStruct((rows - n_sample, n), F32),
        ],
        compiler_params=pltpu.CompilerParams(
            dimension_semantics=("arbitrary",), vmem_limit_bytes=40 * MIB),
    )(c_all, w_ada, b_ada.reshape(1, n))


class _Group:
    def __init__(self, n_seq, n_rows, tm):
        if n_rows >= tm:
            assert n_rows % tm == 0
            self.sb, self.rb = 1, tm
        else:
            assert tm % n_rows == 0 and n_seq % (tm // n_rows) == 0
            self.sb, self.rb = tm // n_rows, n_rows
        self.n_seq, self.n_rows = n_seq, n_rows
        self.tiles_r = n_rows // self.rb
        self.tm = self.sb * self.rb
        self.n_tiles = (n_seq // self.sb) * self.tiles_r
        self.seq_tiles = self.tiles_r if self.sb == 1 else 0

    def x_spec(self, d, ngrid, **kw):
        tr = self.tiles_r
        if ngrid == 2:
            return pl.BlockSpec((self.sb, self.rb, d), lambda m, f: (m // tr, m % tr, 0), **kw)
        return pl.BlockSpec((self.sb, self.rb, d), lambda m: (m // tr, m % tr, 0), **kw)

    def mod_spec(self, mod, d, col, ngrid):
        if self.sb == 1:
            rows, row_blk = mod.shape[0], (lambda m: 0)
        else:
            rows, row_blk = self.sb, (lambda m: m)
        if ngrid == 2:
            return pl.BlockSpec((rows, d), lambda m, f: (row_blk(m), col))
        return pl.BlockSpec((rows, d), lambda m: (row_blk(m), col))


def _ffn_body(x_ref, sh_ref, sc_ref, gt_ref, wg_ref, wu_ref, wd_ref, lng_ref, lnb_ref,
              o_ref, h_sc, *, seq_tiles, alpha):
    f = pl.program_id(1)
    nf = pl.num_programs(1)
    tm, d = o_ref.shape
    sb, rb, _ = x_ref.shape
    rc = min(FFN_RC, tm)
    n_chunks = tm // rc

    def x_chunk(c):
        if sb == 1:
            return x_ref[:, c * rc:(c + 1) * rc, :]
        return x_ref[c * (rc // rb):(c + 1) * (rc // rb)]

    def mod_chunk(ref, c):
        if seq_tiles:
            return _mod_rows(ref, seq_tiles)
        return ref[c * (rc // rb):(c + 1) * (rc // rb), :]

    def weights():
        return wg_ref[...].astype(BF16), wu_ref[...].astype(BF16), wd_ref[...].astype(BF16)

    def gate_up(h16, wg16, wu16):
        return (_silu(_dot(h16, wg16)) * _dot(h16, wu16)).astype(BF16)

    @pl.when(f == 0)
    def _():
        wg16, wu16, wd16 = weights()
        for c in range(n_chunks):
            rs = slice(c * rc, (c + 1) * rc)
            x = x_chunk(c)
            h = x * (1.0 + mod_chunk(sc_ref, c)[:, None, :]) + mod_chunk(sh_ref, c)[:, None, :]
            h16 = h.reshape(rc, d).astype(BF16)
            h_sc[rs, :] = h16
            o_ref[rs, :] = _dot(gate_up(h16, wg16, wu16), wd16)

    @pl.when(jnp.logical_and(f > 0, f < nf - 1))
    def _():
        wg16, wu16, wd16 = weights()
        o_ref[...] += _dot(gate_up(h_sc[...], wg16, wu16), wd16)

    @pl.when(f == nf - 1)
    def _():
        wg16, wu16, wd16 = weights()
        for c in range(n_chunks):
            rs = slice(c * rc, (c + 1) * rc)
            acc = o_ref[rs, :] + _dot(gate_up(h_sc[rs, :], wg16, wu16), wd16)
            x = x_chunk(c)
            gate = mod_chunk(gt_ref, c)[:, None, :]
            y = alpha * x + (0.5 * gate) * acc.reshape(x.shape)
            o_ref[rs, :] = _layer_norm(y.reshape(rc, d), lng_ref[...], lnb_ref[...])


def _ffn(x3, mod, sub, wg, wu, wd, layer, slot, ln_g, ln_b, alpha):
    n_seq, n_rows, d = x3.shape
    dff = wg.shape[-1]
    grp = _Group(n_seq, n_rows, FFN_TM)
    tm = grp.tm
    rc = min(FFN_RC, tm)
    assert dff // FFN_TF >= 2 and tm % rc == 0 and (grp.rb % rc == 0 if grp.sb == 1 else rc % grp.rb == 0)
    out = pl.pallas_call(
        functools.partial(_ffn_body, seq_tiles=grp.seq_tiles, alpha=alpha),
        name="swiglu_ln",
        grid=(grp.n_tiles, dff // FFN_TF),
        in_specs=[
            grp.x_spec(d, 2, pipeline_mode=pl.Buffered(1)),
            grp.mod_spec(mod, d, sub * 3 + 0, 2),
            grp.mod_spec(mod, d, sub * 3 + 1, 2),
            grp.mod_spec(mod, d, sub * 3 + 2, 2),
            pl.BlockSpec((None, None, d, FFN_TF), lambda m, f: (layer, slot, 0, f)),
            pl.BlockSpec((None, None, d, FFN_TF), lambda m, f: (layer, slot, 0, f)),
            pl.BlockSpec((None, None, FFN_TF, d), lambda m, f: (layer, slot, f, 0)),
            pl.BlockSpec((1, d), lambda m, f: (0, 0)),
            pl.BlockSpec((1, d), lambda m, f: (0, 0)),
        ],
        out_specs=pl.BlockSpec((tm, d), lambda m, f: (m, 0)),
        out_shape=jax.ShapeDtypeStruct((n_seq * n_rows, d), F32),
        scratch_shapes=[pltpu.VMEM((tm, d), BF16)],
        compiler_params=pltpu.CompilerParams(
            dimension_semantics=("arbitrary", "arbitrary"), vmem_limit_bytes=56 * MIB),
    )(x3, mod, mod, mod, wg, wu, wd, ln_g.reshape(1, d), ln_b.reshape(1, d))
    return out.reshape(n_seq, n_rows, d)


def _wprep_body(w_ref, o_ref, *, qkvz, gates):
    n_in, cols = w_ref.shape
    n_out = o_ref.shape[0]
    mlp = n_in - qkvz - gates
    o_ref[0:qkvz, :] = w_ref[0:qkvz, :].astype(BF16)
    o_ref[qkvz:qkvz + mlp, :] = w_ref[qkvz + gates:n_in, :].astype(BF16)
    o_ref[qkvz + mlp:n_in, :] = w_ref[qkvz:qkvz + gates, :].astype(BF16)
    o_ref[n_in:n_out, :] = jnp.zeros((n_out - n_in, cols), BF16)


def _wprep(w_in_t, layer, qkvz, gates):
    _, n_in, d = w_in_t.shape
    assert n_in <= PROJ_N and (n_in - gates) % LANES == 0
    return pl.pallas_call(
        functools.partial(_wprep_body, qkvz=qkvz, gates=gates),
        name="proj_weight_prep",
        grid=(d // PREP_TC,),
        in_specs=[pl.BlockSpec((None, n_in, PREP_TC), lambda i: (layer, 0, i))],
        out_specs=pl.BlockSpec((PROJ_N, PREP_TC), lambda i: (0, i)),
        out_shape=jax.ShapeDtypeStruct((PROJ_N, d), BF16),
        compiler_params=pltpu.CompilerParams(
            dimension_semantics=("arbitrary",), vmem_limit_bytes=40 * MIB),
    )(w_in_t)


def _proj_body(x_ref, sh_ref, sc_ref, w_ref, o_ref, h_sc, *, seq_tiles):
    tm, d = h_sc.shape

    @pl.when(pl.program_id(1) == 0)
    def _():
        x = x_ref[...]
        h = x * (1.0 + _mod_rows(sc_ref, seq_tiles)[:, None, :]) + _mod_rows(sh_ref, seq_tiles)[:, None, :]
        h_sc[...] = h.reshape(tm, d).astype(BF16)

    o_ref[...] = _dot_nt(h_sc[...], w_ref[...])


def _proj(x3, mod, sub, w_cat_t):
    n_seq, n_rows, d = x3.shape
    n = w_cat_t.shape[0]
    grp = _Group(n_seq, n_rows, FFN_TM)
    tm = grp.tm
    return pl.pallas_call(
        functools.partial(_proj_body, seq_tiles=grp.seq_tiles),
        name="in_proj",
        grid=(grp.n_tiles, n // PROJ_TN),
        in_specs=[
            grp.x_spec(d, 2, pipeline_mode=pl.Buffered(1)),
            grp.mod_spec(mod, d, sub * 3 + 0, 2),
            grp.mod_spec(mod, d, sub * 3 + 1, 2),
            pl.BlockSpec((PROJ_TN, d), lambda m, j: (j, 0)),
        ],
        out_specs=pl.BlockSpec((tm, PROJ_TN), lambda m, j: (m, j)),
        out_shape=jax.ShapeDtypeStruct((n_seq * n_rows, n), F32),
        scratch_shapes=[pltpu.VMEM((tm, d), BF16)],
        compiler_params=pltpu.CompilerParams(
            dimension_semantics=("arbitrary", "arbitrary"), vmem_limit_bytes=48 * MIB),
    )(x3, mod, mod, w_cat_t)


def _block_masks(n, blk):
    shift = int(math.log2(blk))
    i = lax.broadcasted_iota(jnp.int32, (n, n), 0)
    j = lax.broadcasted_iota(jnp.int32, (n, n), 1)
    same = (i >> shift) == (j >> shift)
    return i, j, same


def _gdn_body(*refs, blk, chained, tiles_per_seq, dk_scale):
    if chained:
        (qkv_ref, z_ref, ba_ref, cw_ref, gp_ref, ng_ref, o_ref, st_ref, xc_sc, act_sc) = refs
    else:
        (qkv_ref, z_ref, ba_ref, buf_ref, sin_ref, cw_ref, gp_ref, ng_ref, o_ref, st_ref,
         xc_sc, bs_sc, act_sc) = refs
    lt, cqkv = qkv_ref.shape
    hd = LANES
    nh = cqkv // (3 * hd)
    t = pl.program_id(0)
    halo = SUBLANES

    if chained:
        first = (t % tiles_per_seq) == 0

        @pl.when(first)
        def _():
            xc_sc[pl.ds(0, halo), :] = jnp.zeros((halo, cqkv), F32)
            st_ref[...] = jnp.zeros_like(st_ref)

        @pl.when(jnp.logical_not(first))
        def _():
            xc_sc[pl.ds(0, halo), :] = xc_sc[pl.ds(lt, halo), :]
    else:
        xc_sc[pl.ds(0, halo), :] = jnp.zeros((halo, cqkv), F32)
        bs_sc[pl.ds(0, lt), :] = buf_ref[...]
        bs_sc[pl.ds(lt, halo), :] = jnp.zeros((halo, cqkv), F32)
    xc_sc[pl.ds(halo, lt), :] = qkv_ref[...]

    cblk = 2 * LANES
    row_in_seq = lax.broadcasted_iota(jnp.int32, (lt, cblk), 0) & (blk - 1)
    for c0 in range(0, cqkv, cblk):
        cs = slice(c0, c0 + cblk)
        y = None
        for j in range(CONV_W):
            d = CONV_W - 1 - j
            term = xc_sc[pl.ds(halo - d, lt), cs]
            if not chained and d > 0:
                term = jnp.where(row_in_seq >= d, term, bs_sc[pl.ds(halo - d, lt), cs])
            term = cw_ref[j:j + 1, cs] * term
            y = term if y is None else y + term
        act_sc[:, cs] = _silu(y)

    ba = ba_ref[...]
    beta_all = jax.nn.sigmoid(ba)
    g_all = -jnp.exp(gp_ref[0:1, :]) * jax.nn.softplus(ba + gp_ref[1:2, :])

    i, j, same = _block_masks(lt, blk)
    incl = same & (i >= j)
    strict = same & (i > j)
    hi = lax.Precision.HIGHEST
    gc_all = _dot(incl.astype(F32), g_all, hi)
    gl_all = _dot((j == (i | (blk - 1))).astype(F32), gc_all, hi)
    gc_t = gc_all.T
    eye = (i == j).astype(F32)
    pair_masks = []
    for lvl in range(int(math.log2(blk))):
        pair_masks.append(((i >> (lvl + 1)) == (j >> (lvl + 1))) & ((i >> lvl) != (j >> lvl)))
    nblk = lt // blk

    heads = range(nh)
    hp = []
    for h in heads:
        q = act_sc[:, h * hd:(h + 1) * hd]
        k = act_sc[:, (nh + h) * hd:(nh + h + 1) * hd]
        v = act_sc[:, (2 * nh + h) * hd:(2 * nh + h + 1) * hd]
        q = q * lax.rsqrt(jnp.sum(q * q, axis=-1, keepdims=True) + RMS_EPS) * dk_scale
        k = k * lax.rsqrt(jnp.sum(k * k, axis=-1, keepdims=True) + RMS_EPS)
        beta = beta_all[:, h:h + 1]
        gcc = gc_all[:, nh + h:nh + h + 1]
        gcr = gc_t[nh + h:nh + h + 1, :]
        glc = gl_all[:, nh + h:nh + h + 1]
        decay = jnp.exp(jnp.where(incl, gcc - gcr, -jnp.inf))
        kb = k * beta
        k16 = k.astype(BF16)
        egc = jnp.exp(gcc)
        hp.append(dict(
            decay=decay, k16=k16, kb16=kb.astype(BF16), q16=q.astype(BF16), glc=glc,
            rhs16=jnp.concatenate([v * beta, kb * egc], axis=1).astype(BF16),
            qd=q * egc, kd=k * jnp.exp(glc - gcc)))

    ms = [_dot_nt(p["kb16"], p["k16"]) * jnp.where(strict, p["decay"], 0.0) for p in hp]
    dinvs = [eye - jnp.where(pair_masks[0], m, 0.0) for m in ms]
    for pm in pair_masks[1:]:
        d16s = [d.astype(BF16) for d in dinvs]
        t16s = [_dot(d16, jnp.where(pm, m, 0.0).astype(BF16)).astype(BF16) for d16, m in zip(d16s, ms)]
        dinvs = [d - _dot(t16, d16) for d, t16, d16 in zip(dinvs, t16s, d16s)]
    xs = [_dot(d.astype(BF16), p["rhs16"]) for d, p in zip(dinvs, hp)]
    qk16s = [(_dot_nt(p["q16"], p["k16"]) * p["decay"]).astype(BF16) for p in hp]

    vn_parts = [[] for _ in heads]
    os_parts = [[] for _ in heads]
    for b in range(nblk):
        rs = slice(b * blk, (b + 1) * blk)
        s0s = [st_ref[0, h] if chained else sin_ref[b, h] for h in heads]
        rr = [_dot(jnp.concatenate([x[rs, hd:], p["qd"][rs]], axis=0).astype(BF16), s0.astype(BF16))
              for x, p, s0 in zip(xs, hp, s0s)]
        vns = [x[rs, :hd] - r[:blk] for x, r in zip(xs, rr)]
        for h in heads:
            p = hp[h]
            s_new = (s0s[h] * jnp.exp(p["glc"][b * blk:b * blk + 1, :])
                     + _dot_tn(p["kd"][rs].astype(BF16), vns[h].astype(BF16)))
            if chained:
                st_ref[0, h] = s_new
            else:
                st_ref[b, h] = s_new
            vn_parts[h].append(vns[h])
            os_parts[h].append(rr[h][blk:])
    for h in heads:
        vn_all = jnp.concatenate(vn_parts[h], axis=0)
        o = jnp.concatenate(os_parts[h], axis=0) + _dot(qk16s[h], vn_all.astype(BF16))
        o = o * lax.rsqrt(jnp.mean(o * o, axis=-1, keepdims=True) + RMS_EPS) * ng_ref[...]
        o = o * _silu(z_ref[:, h * hd:(h + 1) * hd])
        o_ref[:, h * hd:(h + 1) * hd] = o.astype(BF16)


def _gdn(proj, conv_w, gate_par, norm_g, *, n_seq, n_rows, blk, ba_col, conv_buf=None, s0=None):
    t_rows = proj.shape[0]
    nh = DN_HEADS
    hd = LANES
    cqkv = 3 * nh * hd
    lt = SEQ_TILE
    assert lt % blk == 0 and blk & (blk - 1) == 0 and blk >= 2
    chained = s0 is None
    n_tiles = t_rows // lt
    common_in = [
        pl.BlockSpec((lt, cqkv), lambda t: (t, 0)),
        pl.BlockSpec((lt, nh * hd), lambda t: (t, 3)),
        pl.BlockSpec((lt, LANES), lambda t: (t, ba_col)),
    ]
    par_in = [
        pl.BlockSpec(conv_w.shape, lambda t: (0, 0)),
        pl.BlockSpec(gate_par.shape, lambda t: (0, 0)),
        pl.BlockSpec((1, hd), lambda t: (0, 0)),
    ]
    o_spec = pl.BlockSpec((lt, nh * hd), lambda t: (t, 0))
    o_shape = jax.ShapeDtypeStruct((t_rows, nh * hd), BF16)
    scratch = [pltpu.VMEM((lt + SUBLANES, cqkv), F32)]
    if chained:
        assert n_rows % lt == 0 and blk == DN_CHUNK
        tps = n_rows // lt
        in_specs = common_in + par_in
        args = (proj, proj, proj, conv_w, gate_par, norm_g)
        st_spec = pl.BlockSpec((1, nh, hd, hd), lambda t: (t // tps, 0, 0, 0))
    else:
        assert blk == n_rows and conv_buf is not None
        tps = 0
        spt = lt // n_rows
        in_specs = common_in + [
            pl.BlockSpec((lt, cqkv), lambda t: (t, 0)),
            pl.BlockSpec((spt, nh, hd, hd), lambda t: (t, 0, 0, 0)),
        ] + par_in
        args = (proj, proj, proj, conv_buf, s0, conv_w, gate_par, norm_g)
        st_spec = pl.BlockSpec((spt, nh, hd, hd), lambda t: (t, 0, 0, 0))
        scratch.append(pltpu.VMEM((lt + SUBLANES, cqkv), F32))
    scratch.append(pltpu.VMEM((lt, cqkv), F32))
    return pl.pallas_call(
        functools.partial(_gdn_body, blk=blk, chained=chained, tiles_per_seq=tps,
                          dk_scale=float(hd) ** -0.5),
        name="gdn_chained" if chained else "gdn_stateful",
        grid=(n_tiles,),
        in_specs=in_specs,
        out_specs=[o_spec, st_spec],
        out_shape=[o_shape, jax.ShapeDtypeStruct((n_seq, nh, hd, hd), F32)],
        scratch_shapes=scratch,
        compiler_params=pltpu.CompilerParams(
            dimension_semantics=("arbitrary",), vmem_limit_bytes=56 * MIB),
    )(*args)


def _mlp_body(u_ref, v_ref, ws_ref, bst_ref, lng_ref, lnb_ref, o_ref, *maybe_vr_ref, blk):
    ck = ws_ref.shape[1]
    gd = u_ref.shape[1] // MLP_GROUPS
    i, j, same = _block_masks(ck, blk)
    incl = same & (i >= j)
    hi = lax.Precision.HIGHEST
    if blk != ck:
        rep = (j == (i & (blk - 1))).astype(F32)
        bias_all = _dot(rep, bst_ref[...], hi)
    else:
        bias_all = bst_ref[...]
    for g in range(MLP_GROUPS):
        cs = slice(g * gd, (g + 1) * gd)
        wsp = ws_ref[g]
        if blk != ck:
            wsp = _dot_nt(_dot(rep, wsp, hi), rep, hi)
        wsp = jnp.where(incl, wsp, 0.0).astype(BF16)
        for c in range(u_ref.shape[0] // ck):
            rs = slice(c * ck, (c + 1) * ck)
            uu = jax.nn.gelu(u_ref[rs, cs])
            vv = _layer_norm(jax.nn.gelu(v_ref[rs, cs]), lng_ref[:, cs], lnb_ref[:, cs])
            if maybe_vr_ref:
                maybe_vr_ref[0][rs, cs] = vv
            s = _dot(wsp, vv.astype(BF16)) + bias_all[:, g:g + 1]
            o_ref[rs, cs] = (uu * s).astype(BF16)


def _mlp(proj, w_spatial, b_spatial, ln_g, ln_b, *, blk, want_v_rows):
    t_rows = proj.shape[0]
    lt = MLP_TILE
    ck = w_spatial.shape[1]
    width = ln_g.size
    assert ck == MLP_CHUNK and lt % ck == 0 and ck % blk == 0 and t_rows % lt == 0
    u_col = (4 * DN_HEADS * LANES) // width
    bst = jnp.zeros((ck, LANES), F32).at[:, :MLP_GROUPS].set(b_spatial.T)
    o_spec = pl.BlockSpec((lt, width), lambda t: (t, 0))
    out_specs = [o_spec]
    out_shape = [jax.ShapeDtypeStruct((t_rows, width), BF16)]
    if want_v_rows:
        out_specs.append(o_spec)
        out_shape.append(jax.ShapeDtypeStruct((t_rows, width), F32))
    return pl.pallas_call(
        functools.partial(_mlp_body, blk=blk),
        name="spatial_mlp",
        grid=(t_rows // lt,),
        in_specs=[
            pl.BlockSpec((lt, width), lambda t: (t, u_col)),
            pl.BlockSpec((lt, width), lambda t: (t, u_col + 1)),
            pl.BlockSpec(w_spatial.shape, lambda t: (0, 0, 0)),
            pl.BlockSpec((ck, LANES), lambda t: (0, 0)),
            pl.BlockSpec((1, width), lambda t: (0, 0)),
            pl.BlockSpec((1, width), lambda t: (0, 0)),
        ],
        out_specs=out_specs,
        out_shape=out_shape,
        compiler_params=pltpu.CompilerParams(
            dimension_semantics=("arbitrary",), vmem_limit_bytes=32 * MIB),
    )(proj, proj, w_spatial, bst, ln_g.reshape(1, width), ln_b.reshape(1, width))


def _mix_body(od_ref, om_ref, x_ref, gt_ref, w_ref, lng_ref, lnb_ref, o_ref, w_sc, *, seq_tiles, alpha):
    tm, d = o_ref.shape
    kd = od_ref.shape[1]

    @pl.when(pl.program_id(0) == 0)
    def _():
        w_sc[...] = w_ref[...].astype(BF16)

    sb, rb, _ = x_ref.shape
    rc = min(FFN_RC, tm)
    for c in range(tm // rc):
        rs = slice(c * rc, (c + 1) * rc)
        mix = _dot(od_ref[rs, :], w_sc[:kd, :]) + _dot(om_ref[rs, :], w_sc[kd:, :])
        if sb == 1:
            x = x_ref[:, rs, :]
            gate = _mod_rows(gt_ref, seq_tiles)[:, None, :]
        else:
            ss = slice(c * (rc // rb), (c + 1) * (rc // rb))
            x = x_ref[ss]
            gate = gt_ref[ss, :][:, None, :]
        y = alpha * x + gate * mix.reshape(x.shape)
        o_ref[rs, :] = _layer_norm(y.reshape(rc, d), lng_ref[...], lnb_ref[...])


def _mix(o_dn, o_mlp, x3, mod, sub, w_out, layer, ln_g, ln_b, alpha):
    n_seq, n_rows, d = x3.shape
    grp = _Group(n_seq, n_rows, MIX_TM)
    tm = grp.tm
    out = pl.pallas_call(
        functools.partial(_mix_body, seq_tiles=grp.seq_tiles, alpha=alpha),
        name="out_mix_ln",
        grid=(grp.n_tiles,),
        in_specs=[
            pl.BlockSpec((tm, o_dn.shape[1]), lambda m: (m, 0)),
            pl.BlockSpec((tm, o_mlp.shape[1]), lambda m: (m, 0)),
            grp.x_spec(d, 1),
            grp.mod_spec(mod, d, sub * 3 + 2, 1),
            pl.BlockSpec((None,) + w_out.shape[1:], lambda m: (layer, 0, 0), pipeline_mode=pl.Buffered(1)),
            pl.BlockSpec((1, d), lambda m: (0, 0)),
            pl.BlockSpec((1, d), lambda m: (0, 0)),
        ],
        out_specs=pl.BlockSpec((tm, d), lambda m: (m, 0)),
        out_shape=jax.ShapeDtypeStruct((n_seq * n_rows, d), F32),
        scratch_shapes=[pltpu.VMEM(w_out.shape[1:], BF16)],
        compiler_params=pltpu.CompilerParams(
            dimension_semantics=("arbitrary",), vmem_limit_bytes=56 * MIB),
    )(o_dn, o_mlp, x3, mod, w_out, ln_g.reshape(1, d), ln_b.reshape(1, d))
    return out.reshape(n_seq, n_rows, d)


def _trunk_layer(x3, mod, wts, layer, alpha, conv_buf, s0):
    n_seq, n_rows, d = x3.shape
    qkv_dim = 3 * DN_HEADS * LANES
    ffn_w = (wts["wg"], wts["wu"], wts["wd"], layer)
    x3 = _ffn(x3, mod, 0, *ffn_w, 0, wts["ln_g"][0], wts["ln_b"][0], alpha)
    proj = _proj(x3, mod, 1, wts["w_cat"])
    dn_blk = DN_CHUNK if n_rows % DN_CHUNK == 0 else n_rows
    gdn_kw = dict(n_seq=n_seq, n_rows=n_rows, blk=dn_blk, ba_col=wts["ba_col"])
    if s0 is None:
        o_dn, s_new = _gdn(proj, wts["conv_w"], wts["gate_par"], wts["dn_norm_g"], **gdn_kw)
    else:
        pad = jnp.zeros((n_seq, SUBLANES - (CONV_W - 1), qkv_dim), F32)
        buf = jnp.concatenate([pad, conv_buf], axis=1).reshape(n_seq * SUBLANES, qkv_dim)
        assert n_rows == SUBLANES
        o_dn, s_new = _gdn(proj, wts["conv_w"], wts["gate_par"], wts["dn_norm_g"],
                           conv_buf=buf, s0=s0, **gdn_kw)
    mlp_out = _mlp(proj, wts["w_spatial"], wts["b_spatial"], wts["mlp_ln_g"], wts["mlp_ln_b"],
                   blk=min(n_rows, MLP_CHUNK), want_v_rows=s0 is not None)
    o_mlp = mlp_out[0]
    v_rows = mlp_out[1].reshape(n_seq, n_rows, -1) if s0 is not None else None
    x3 = _mix(o_dn, o_mlp, x3, mod, 1, wts["w_out"], layer, wts["ln_g"][1], wts["ln_b"][1], alpha)
    x3 = _ffn(x3, mod, 2, *ffn_w, 1, wts["ln_g"][2], wts["ln_b"][2], alpha)
    new_buf = proj.reshape(n_seq, n_rows, -1)[:, n_rows - (CONV_W - 1):, :qkv_dim]
    return x3, new_buf, s_new, v_rows


def kernel(x_prompt, x_sample, c_prompt, c_sample, state_delta, state_conv, w_ada, b_ada, ln_g, ln_b,
           ffn_wg, ffn_wu, ffn_wd, w_in, conv_w, a_log, dt_bias, dn_norm_g, mlp_ln_g, mlp_ln_b,
           w_spatial, b_spatial, w_out):
    depth = w_ada.shape[0]
    bp, _, d = x_prompt.shape
    bs = x_sample.shape[0]
    nh = DN_HEADS
    alpha = (2.0 * depth) ** 0.25
    qkvz = 4 * nh * LANES
    gates = 2 * nh

    pad_rows = (-(bs + bp)) % SUBLANES
    c_all = jnp.concatenate([c_sample, c_prompt, jnp.zeros((pad_rows, d), F32)], axis=0)
    w_in_t = jnp.swapaxes(w_in, 1, 2)

    y_p, y_s = x_prompt, x_sample
    delta_p, conv_p, delta_s, conv_s, vrows_s = [], [], [], [], []
    for layer in range(depth):
        mod_s, mod_p = _ada(c_all, w_ada[layer], b_ada[layer], bs)
        gate_par = jnp.zeros((SUBLANES, LANES), F32)
        gate_par = gate_par.at[0, nh:gates].set(a_log[layer]).at[1, nh:gates].set(dt_bias[layer])
        wts = dict(
            wg=ffn_wg, wu=ffn_wu, wd=ffn_wd, ln_g=ln_g[layer], ln_b=ln_b[layer],
            w_cat=_wprep(w_in_t, layer, qkvz, gates), ba_col=(w_in.shape[-1] - gates) // LANES,
            conv_w=conv_w[layer], gate_par=gate_par,
            dn_norm_g=dn_norm_g[layer].reshape(1, LANES), mlp_ln_g=mlp_ln_g[layer], mlp_ln_b=mlp_ln_b[layer],
            w_spatial=w_spatial[layer], b_spatial=b_spatial[layer], w_out=w_out,
        )
        y_p, cb_p, ds_p, _ = _trunk_layer(y_p, mod_p, wts, layer, alpha, None, None)
        y_s, cb_s, ds_s, vr_s = _trunk_layer(y_s, mod_s, wts, layer, alpha, state_conv[layer],
                                             state_delta[layer])
        delta_p.append(ds_p)
        conv_p.append(cb_p)
        delta_s.append(ds_s)
        conv_s.append(cb_s)
        vrows_s.append(vr_s)
    return (y_p, y_s, jnp.stack(delta_p), jnp.stack(conv_p), jnp.stack(delta_s), jnp.stack(conv_s),
            jnp.stack(vrows_s))
```

```python
import functools
import math

import jax
import jax.numpy as jnp
from jax import lax
from jax.experimental import pallas as pl
from jax.experimental.pallas import tpu as pltpu

F32 = jnp.float32
BF16 = jnp.bfloat16

DN_HEADS = 8
DN_CHUNK = 64
CONV_W = 4
MLP_GROUPS = 4
MLP_CHUNK = 128
N_SUB = 3
LN_EPS = 1e-5
RMS_EPS = 1e-6

LANES = 128
SUBLANES = 8
MIB = 2 ** 20

FFN_TM = 1024
FFN_TF = 256
FFN_RC = 256
PROJ_TN = 1280
PROJ_N = 6400
PREP_TC = 256
MIX_TM = 512
ADA_TN = 1024
SEQ_TILE = 128
MLP_TILE = 512


def _dot(a, b, precision=None):
    return jnp.dot(a, b, preferred_element_type=F32, precision=precision)


def _dot_nt(a, b, precision=None):
    return lax.dot_general(a, b, (((1,), (1,)), ((), ())),
                           preferred_element_type=F32, precision=precision)


def _dot_tn(a, b):
    return lax.dot_general(a, b, (((0,), (0,)), ((), ())), preferred_element_type=F32)


def _layer_norm(y, g, b):
    mu = jnp.mean(y, axis=-1, keepdims=True)
    yc = y - mu
    var = jnp.mean(yc * yc, axis=-1, keepdims=True)
    return yc * lax.rsqrt(var + LN_EPS) * g + b


def _silu(x):
    return x * jax.nn.sigmoid(x)


def _mod_rows(ref, seq_tiles):
    if seq_tiles:
        return ref[pl.ds(pl.program_id(0) // seq_tiles, 1), :]
    return ref[...]


def _ada_body(c_ref, w_ref, b_ref, ms_ref, mp_ref):
    c = c_ref[...]
    a = _silu(c).astype(BF16)
    y = _dot(a, w_ref[...].astype(BF16)) + b_ref[...]
    ns = ms_ref.shape[0]
    ms_ref[...] = y[:ns]
    mp_ref[...] = y[ns:]


def _ada(c_all, w_ada, b_ada, n_sample):
    rows, d = c_all.shape
    n = w_ada.shape[1]
    return pl.pallas_call(
        _ada_body,
        name="ada_mod",
        grid=(n // ADA_TN,),
        in_specs=[
            pl.BlockSpec((rows, d), lambda j: (0, 0)),
            pl.BlockSpec((d, ADA_TN), lambda j: (0, j)),
            pl.BlockSpec((1, ADA_TN), lambda j: (0, j)),
        ],
        out_specs=[
            pl.BlockSpec((n_sample, ADA_TN), lambda j: (0, j)),
            pl.BlockSpec((rows - n_sample, ADA_TN), lambda j: (0, j)),
        ],
        out_shape=[
            jax.ShapeDtypeStruct((n_sample, n), F32),
            jax.ShapeDtypeStruct((rows - n_sample, n), F32),
        ],
        compiler_params=pltpu.CompilerParams(
            dimension_semantics=("arbitrary",), vmem_limit_bytes=40 * MIB),
    )(c_all, w_ada, b_ada.reshape(1, n))


class _Group:
    def __init__(self, n_seq, n_rows, tm):
        if n_rows >= tm:
            assert n_rows % tm == 0
            self.sb, self.rb = 1, tm
        else:
            assert tm % n_rows == 0 and n_seq % (tm // n_rows) == 0
            self.sb, self.rb = tm // n_rows, n_rows
        self.n_seq, self.n_rows = n_seq, n_rows
        self.tiles_r = n_rows // self.rb
        self.tm = self.sb * self.rb
        self.n_tiles = (n_seq // self.sb) * self.tiles_r
        self.seq_tiles = self.tiles_r if self.sb == 1 else 0

    def x_spec(self, d, ngrid, **kw):
        tr = self.tiles_r
        if ngrid == 2:
            return pl.BlockSpec((self.sb, self.rb, d), lambda m, f: (m // tr, m % tr, 0), **kw)
        return pl.BlockSpec((self.sb, self.rb, d), lambda m: (m // tr, m % tr, 0), **kw)

    def mod_spec(self, mod, d, col, ngrid):
        if self.sb == 1:
            rows, row_blk = mod.shape[0], (lambda m: 0)
        else:
            rows, row_blk = self.sb, (lambda m: m)
        if ngrid == 2:
            return pl.BlockSpec((rows, d), lambda m, f: (row_blk(m), col))
        return pl.BlockSpec((rows, d), lambda m: (row_blk(m), col))


def _ffn_body(x_ref, sh_ref, sc_ref, gt_ref, wg_ref, wu_ref, wd_ref, lng_ref, lnb_ref,
              o_ref, h_sc, *, seq_tiles, alpha):
    f = pl.program_id(1)
    nf = pl.num_programs(1)
    tm, d = o_ref.shape
    sb, rb, _ = x_ref.shape
    rc = min(FFN_RC, tm)
    n_chunks = tm // rc

    def x_chunk(c):
        if sb == 1:
            return x_ref[:, c * rc:(c + 1) * rc, :]
        return x_ref[c * (rc // rb):(c + 1) * (rc // rb)]

    def mod_chunk(ref, c):
        if seq_tiles:
            return _mod_rows(ref, seq_tiles)
        return ref[c * (rc // rb):(c + 1) * (rc // rb), :]

    def weights():
        return wg_ref[...].astype(BF16), wu_ref[...].astype(BF16), wd_ref[...].astype(BF16)

    def gate_up(h16, wg16, wu16):
        return (_silu(_dot(h16, wg16)) * _dot(h16, wu16)).astype(BF16)

    @pl.when(f == 0)
    def _():
        wg16, wu16, wd16 = weights()
        for c in range(n_chunks):
            rs = slice(c * rc, (c + 1) * rc)
            x = x_chunk(c)
            h = x * (1.0 + mod_chunk(sc_ref, c)[:, None, :]) + mod_chunk(sh_ref, c)[:, None, :]
            h16 = h.reshape(rc, d).astype(BF16)
            h_sc[rs, :] = h16
            o_ref[rs, :] = _dot(gate_up(h16, wg16, wu16), wd16)

    @pl.when(jnp.logical_and(f > 0, f < nf - 1))
    def _():
        wg16, wu16, wd16 = weights()
        o_ref[...] += _dot(gate_up(h_sc[...], wg16, wu16), wd16)

    @pl.when(f == nf - 1)
    def _():
        wg16, wu16, wd16 = weights()
        for c in range(n_chunks):
            rs = slice(c * rc, (c + 1) * rc)
            acc = o_ref[rs, :] + _dot(gate_up(h_sc[rs, :], wg16, wu16), wd16)
            x = x_chunk(c)
            gate = mod_chunk(gt_ref, c)[:, None, :]
            y = alpha * x + (0.5 * gate) * acc.reshape(x.shape)
            o_ref[rs, :] = _layer_norm(y.reshape(rc, d), lng_ref[...], lnb_ref[...])


def _ffn(x3, mod, sub, wg, wu, wd, layer, slot, ln_g, ln_b, alpha):
    n_seq, n_rows, d = x3.shape
    dff = wg.shape[-1]
    grp = _Group(n_seq, n_rows, FFN_TM)
    tm = grp.tm
    rc = min(FFN_RC, tm)
    assert dff // FFN_TF >= 2 and tm % rc == 0 and (grp.rb % rc == 0 if grp.sb == 1 else rc % grp.rb == 0)
    out = pl.pallas_call(
        functools.partial(_ffn_body, seq_tiles=grp.seq_tiles, alpha=alpha),
        name="swiglu_ln",
        grid=(grp.n_tiles, dff // FFN_TF),
        in_specs=[
            grp.x_spec(d, 2, pipeline_mode=pl.Buffered(1)),
            grp.mod_spec(mod, d, sub * 3 + 0, 2),
            grp.mod_spec(mod, d, sub * 3 + 1, 2),
            grp.mod_spec(mod, d, sub * 3 + 2, 2),
            pl.BlockSpec((None, None, d, FFN_TF), lambda m, f: (layer, slot, 0, f)),
            pl.BlockSpec((None, None, d, FFN_TF), lambda m, f: (layer, slot, 0, f)),
            pl.BlockSpec((None, None, FFN_TF, d), lambda m, f: (layer, slot, f, 0)),
            pl.BlockSpec((1, d), lambda m, f: (0, 0)),
            pl.BlockSpec((1, d), lambda m, f: (0, 0)),
        ],
        out_specs=pl.BlockSpec((tm, d), lambda m, f: (m, 0)),
        out_shape=jax.ShapeDtypeStruct((n_seq * n_rows, d), F32),
        scratch_shapes=[pltpu.VMEM((tm, d), BF16)],
        compiler_params=pltpu.CompilerParams(
            dimension_semantics=("arbitrary", "arbitrary"), vmem_limit_bytes=56 * MIB),
    )(x3, mod, mod, mod, wg, wu, wd, ln_g.reshape(1, d), ln_b.reshape(1, d))
    return out.reshape(n_seq, n_rows, d)


def _wprep_body(w_ref, o_ref, *, qkvz, gates):
    n_in, cols = w_ref.shape
    n_out = o_ref.shape[0]
    mlp = n_in - qkvz - gates
    o_ref[0:qkvz, :] = w_ref[0:qkvz, :].astype(BF16)
    o_ref[qkvz:qkvz + mlp, :] = w_ref[qkvz + gates:n_in, :].astype(BF16)
    o_ref[qkvz + mlp:n_in, :] = w_ref[qkvz:qkvz + gates, :].astype(BF16)
    o_ref[n_in:n_out, :] = jnp.zeros((n_out - n_in, cols), BF16)


def _wprep(w_in_t, layer, qkvz, gates):
    _, n_in, d = w_in_t.shape
    assert n_in <= PROJ_N and (n_in - gates) % LANES == 0
    return pl.pallas_call(
        functools.partial(_wprep_body, qkvz=qkvz, gates=gates),
        name="proj_weight_prep",
        grid=(d // PREP_TC,),
        in_specs=[pl.BlockSpec((None, n_in, PREP_TC), lambda i: (layer, 0, i))],
        out_specs=pl.BlockSpec((PROJ_N, PREP_TC), lambda i: (0, i)),
        out_shape=jax.ShapeDtypeStruct((PROJ_N, d), BF16),
        compiler_params=pltpu.CompilerParams(
            dimension_semantics=("arbitrary",), vmem_limit_bytes=40 * MIB),
    )(w_in_t)


def _proj_body(x_ref, sh_ref, sc_ref, w_ref, o_ref, h_sc, *, seq_tiles):
    tm, d = h_sc.shape

    @pl.when(pl.program_id(1) == 0)
    def _():
        x = x_ref[...]
        h = x * (1.0 + _mod_rows(sc_ref, seq_tiles)[:, None, :]) + _mod_rows(sh_ref, seq_tiles)[:, None, :]
        h_sc[...] = h.reshape(tm, d).astype(BF16)

    o_ref[...] = _dot_nt(h_sc[...], w_ref[...])


def _proj(x3, mod, sub, w_cat_t):
    n_seq, n_rows, d = x3.shape
    n = w_cat_t.shape[0]
    grp = _Group(n_seq, n_rows, FFN_TM)
    tm = grp.tm
    return pl.pallas_call(
        functools.partial(_proj_body, seq_tiles=grp.seq_tiles),
        name="in_proj",
        grid=(grp.n_tiles, n // PROJ_TN),
        in_specs=[
            grp.x_spec(d, 2),
            grp.mod_spec(mod, d, sub * 3 + 0, 2),
            grp.mod_spec(mod, d, sub * 3 + 1, 2),
            pl.BlockSpec((PROJ_TN, d), lambda m, j: (j, 0)),
        ],
        out_specs=pl.BlockSpec((tm, PROJ_TN), lambda m, j: (m, j)),
        out_shape=jax.ShapeDtypeStruct((n_seq * n_rows, n), F32),
        scratch_shapes=[pltpu.VMEM((tm, d), BF16)],
        compiler_params=pltpu.CompilerParams(
            dimension_semantics=("arbitrary", "arbitrary"), vmem_limit_bytes=48 * MIB),
    )(x3, mod, mod, w_cat_t)


def _block_masks(n, blk):
    shift = int(math.log2(blk))
    i = lax.broadcasted_iota(jnp.int32, (n, n), 0)
    j = lax.broadcasted_iota(jnp.int32, (n, n), 1)
    same = (i >> shift) == (j >> shift)
    return i, j, same


def _gdn_body(*refs, blk, chained, tiles_per_seq, dk_scale):
    if chained:
        (qkv_ref, z_ref, ba_ref, cw_ref, gp_ref, ng_ref, o_ref, st_ref, xc_sc, act_sc) = refs
    else:
        (qkv_ref, z_ref, ba_ref, buf_ref, sin_ref, cw_ref, gp_ref, ng_ref, o_ref, st_ref, act_sc) = refs
    lt, cqkv = qkv_ref.shape
    hd = LANES
    nh = cqkv // (3 * hd)
    t = pl.program_id(0)
    halo = SUBLANES
    ngrp = lt // SUBLANES

    if chained:
        first = (t % tiles_per_seq) == 0

        @pl.when(first)
        def _():
            xc_sc[pl.ds(0, halo), :] = jnp.zeros((halo, cqkv), F32)
            st_ref[...] = jnp.zeros_like(st_ref)

        @pl.when(jnp.logical_not(first))
        def _():
            xc_sc[pl.ds(0, halo), :] = xc_sc[pl.ds(lt, halo), :]

        xc_sc[pl.ds(halo, lt), :] = qkv_ref[...]
    else:
        assert blk == SUBLANES

    cblk = 2 * LANES
    row_in_grp = lax.broadcasted_iota(jnp.int32, (ngrp, SUBLANES, cblk), 1)
    for c0 in range(0, cqkv, cblk):
        cs = slice(c0, c0 + cblk)
        if chained:
            xe = xc_sc[:, cs].reshape(ngrp + 1, SUBLANES, cblk)
            cur = xe[1:]
        else:
            cur = qkv_ref[:, cs].reshape(ngrp, SUBLANES, cblk)
            prev = buf_ref[:, cs].reshape(ngrp, SUBLANES, cblk)
        y = None
        for j in range(CONV_W):
            d = CONV_W - 1 - j
            if d == 0:
                term = cur
            elif chained:
                rot = pltpu.roll(xe, d, axis=1)
                term = jnp.where(row_in_grp >= d, rot[1:], rot[:-1])
            else:
                term = jnp.where(row_in_grp >= d, pltpu.roll(cur, d, axis=1), pltpu.roll(prev, d, axis=1))
            term = cw_ref[j:j + 1, cs][None] * term
            y = term if y is None else y + term
        act_sc[:, cs] = _silu(y).reshape(lt, cblk)

    ba = ba_ref[...]
    beta_all = jax.nn.sigmoid(ba)
    g_all = -jnp.exp(gp_ref[0:1, :]) * jax.nn.softplus(ba + gp_ref[1:2, :])

    i, j, same = _block_masks(lt, blk)
    incl = same & (i >= j)
    strict = same & (i > j)
    hi = lax.Precision.HIGHEST
    gc_all = _dot(incl.astype(F32), g_all, hi)
    gl_all = _dot((j == (i | (blk - 1))).astype(F32), gc_all, hi)
    gc_t = gc_all.T
    eye = (i == j).astype(F32)
    pair_masks = []
    for lvl in range(int(math.log2(blk))):
        pair_masks.append(((i >> (lvl + 1)) == (j >> (lvl + 1))) & ((i >> lvl) != (j >> lvl)))
    nblk = lt // blk

    heads = range(nh)
    hp = []
    for h in heads:
        q = act_sc[:, h * hd:(h + 1) * hd]
        k = act_sc[:, (nh + h) * hd:(nh + h + 1) * hd]
        v = act_sc[:, (2 * nh + h) * hd:(2 * nh + h + 1) * hd]
        q = q * lax.rsqrt(jnp.sum(q * q, axis=-1, keepdims=True) + RMS_EPS) * dk_scale
        k = k * lax.rsqrt(jnp.sum(k * k, axis=-1, keepdims=True) + RMS_EPS)
        beta = beta_all[:, h:h + 1]
        gcc = gc_all[:, nh + h:nh + h + 1]
        gcr = gc_t[nh + h:nh + h + 1, :]
        glc = gl_all[:, nh + h:nh + h + 1]
        decay = jnp.exp(jnp.where(incl, gcc - gcr, -jnp.inf))
        kb = k * beta
        k16 = k.astype(BF16)
        egc = jnp.exp(gcc)
        hp.append(dict(
            decay=decay, k16=k16, kb16=kb.astype(BF16), q16=q.astype(BF16), glc=glc,
            rhs16=jnp.concatenate([v * beta, kb * egc], axis=1).astype(BF16),
            qd=q * egc, kd=k * jnp.exp(glc - gcc)))

    ms = [_dot_nt(p["kb16"], p["k16"]) * jnp.where(strict, p["decay"], 0.0) for p in hp]
    dinvs = [eye - jnp.where(pair_masks[0], m, 0.0) for m in ms]
    for pm in pair_masks[1:]:
        d16s = [d.astype(BF16) for d in dinvs]
        t16s = [_dot(d16, jnp.where(pm, m, 0.0).astype(BF16)).astype(BF16) for d16, m in zip(d16s, ms)]
        dinvs = [d - _dot(t16, d16) for d, t16, d16 in zip(dinvs, t16s, d16s)]
    xs = [_dot(d.astype(BF16), p["rhs16"]) for d, p in zip(dinvs, hp)]
    qk16s = [(_dot_nt(p["q16"], p["k16"]) * p["decay"]).astype(BF16) for p in hp]

    vn_parts = [[] for _ in heads]
    os_parts = [[] for _ in heads]
    for b in range(nblk):
        rs = slice(b * blk, (b + 1) * blk)
        s0s = [st_ref[0, h] if chained else sin_ref[b, h] for h in heads]
        rr = [_dot(jnp.concatenate([x[rs, hd:], p["qd"][rs]], axis=0).astype(BF16), s0.astype(BF16))
              for x, p, s0 in zip(xs, hp, s0s)]
        vns = [x[rs, :hd] - r[:blk] for x, r in zip(xs, rr)]
        for h in heads:
            p = hp[h]
            s_new = (s0s[h] * jnp.exp(p["glc"][b * blk:b * blk + 1, :])
                     + _dot_tn(p["kd"][rs].astype(BF16), vns[h].astype(BF16)))
            if chained:
                st_ref[0, h] = s_new
            else:
                st_ref[b, h] = s_new
            vn_parts[h].append(vns[h])
            os_parts[h].append(rr[h][blk:])
    for h in heads:
        vn_all = jnp.concatenate(vn_parts[h], axis=0)
        o = jnp.concatenate(os_parts[h], axis=0) + _dot(qk16s[h], vn_all.astype(BF16))
        o = o * lax.rsqrt(jnp.mean(o * o, axis=-1, keepdims=True) + RMS_EPS) * ng_ref[...]
        o = o * _silu(z_ref[:, h * hd:(h + 1) * hd])
        o_ref[:, h * hd:(h + 1) * hd] = o.astype(BF16)


def _gdn(proj, conv_w, gate_par, norm_g, *, n_seq, n_rows, blk, ba_col, conv_buf=None, s0=None):
    t_rows = proj.shape[0]
    nh = DN_HEADS
    hd = LANES
    cqkv = 3 * nh * hd
    lt = SEQ_TILE
    assert lt % blk == 0 and blk & (blk - 1) == 0 and blk >= 2
    chained = s0 is None
    n_tiles = t_rows // lt
    common_in = [
        pl.BlockSpec((lt, cqkv), lambda t: (t, 0)),
        pl.BlockSpec((lt, nh * hd), lambda t: (t, 3)),
        pl.BlockSpec((lt, LANES), lambda t: (t, ba_col)),
    ]
    par_in = [
        pl.BlockSpec(conv_w.shape, lambda t: (0, 0)),
        pl.BlockSpec(gate_par.shape, lambda t: (0, 0)),
        pl.BlockSpec((1, hd), lambda t: (0, 0)),
    ]
    o_spec = pl.BlockSpec((lt, nh * hd), lambda t: (t, 0))
    o_shape = jax.ShapeDtypeStruct((t_rows, nh * hd), BF16)
    scratch = []
    if chained:
        assert n_rows % lt == 0 and blk == DN_CHUNK
        tps = n_rows // lt
        in_specs = common_in + par_in
        args = (proj, proj, proj, conv_w, gate_par, norm_g)
        st_spec = pl.BlockSpec((1, nh, hd, hd), lambda t: (t // tps, 0, 0, 0))
        scratch.append(pltpu.VMEM((lt + SUBLANES, cqkv), F32))
    else:
        assert blk == n_rows and conv_buf is not None
        tps = 0
        spt = lt // n_rows
        in_specs = common_in + [
            pl.BlockSpec((lt, cqkv), lambda t: (t, 0)),
            pl.BlockSpec((spt, nh, hd, hd), lambda t: (t, 0, 0, 0)),
        ] + par_in
        args = (proj, proj, proj, conv_buf, s0, conv_w, gate_par, norm_g)
        st_spec = pl.BlockSpec((spt, nh, hd, hd), lambda t: (t, 0, 0, 0))
    scratch.append(pltpu.VMEM((lt, cqkv), F32))
    return pl.pallas_call(
        functools.partial(_gdn_body, blk=blk, chained=chained, tiles_per_seq=tps,
                          dk_scale=float(hd) ** -0.5),
        name="gdn_chained" if chained else "gdn_stateful",
        grid=(n_tiles,),
        in_specs=in_specs,
        out_specs=[o_spec, st_spec],
        out_shape=[o_shape, jax.ShapeDtypeStruct((n_seq, nh, hd, hd), F32)],
        scratch_shapes=scratch,
        compiler_params=pltpu.CompilerParams(
            dimension_semantics=("arbitrary",), vmem_limit_bytes=56 * MIB),
    )(*args)


def _mlp_body(u_ref, v_ref, ws_ref, bst_ref, lng_ref, lnb_ref, o_ref, *maybe_vr_ref, blk):
    ck = ws_ref.shape[1]
    gd = u_ref.shape[1] // MLP_GROUPS
    i, j, same = _block_masks(ck, blk)
    incl = same & (i >= j)
    hi = lax.Precision.HIGHEST
    if blk != ck:
        rep = (j == (i & (blk - 1))).astype(F32)
        bias_all = _dot(rep, bst_ref[...], hi)
    else:
        bias_all = bst_ref[...]
    for g in range(MLP_GROUPS):
        cs = slice(g * gd, (g + 1) * gd)
        wsp = ws_ref[g]
        if blk != ck:
            wsp = _dot_nt(_dot(rep, wsp, hi), rep, hi)
        wsp = jnp.where(incl, wsp, 0.0).astype(BF16)
        for c in range(u_ref.shape[0] // ck):
            rs = slice(c * ck, (c + 1) * ck)
            uu = jax.nn.gelu(u_ref[rs, cs])
            vv = _layer_norm(jax.nn.gelu(v_ref[rs, cs]), lng_ref[:, cs], lnb_ref[:, cs])
            if maybe_vr_ref:
                maybe_vr_ref[0][rs, cs] = vv
            s = _dot(wsp, vv.astype(BF16)) + bias_all[:, g:g + 1]
            o_ref[rs, cs] = (uu * s).astype(BF16)


def _mlp(proj, w_spatial, b_spatial, ln_g, ln_b, *, blk, want_v_rows):
    t_rows = proj.shape[0]
    lt = MLP_TILE
    ck = w_spatial.shape[1]
    width = ln_g.size
    assert ck == MLP_CHUNK and lt % ck == 0 and ck % blk == 0 and t_rows % lt == 0
    u_col = (4 * DN_HEADS * LANES) // width
    bst = jnp.zeros((ck, LANES), F32).at[:, :MLP_GROUPS].set(b_spatial.T)
    o_spec = pl.BlockSpec((lt, width), lambda t: (t, 0))
    out_specs = [o_spec]
    out_shape = [jax.ShapeDtypeStruct((t_rows, width), BF16)]
    if want_v_rows:
        out_specs.append(o_spec)
        out_shape.append(jax.ShapeDtypeStruct((t_rows, width), F32))
    return pl.pallas_call(
        functools.partial(_mlp_body, blk=blk),
        name="spatial_mlp",
        grid=(t_rows // lt,),
        in_specs=[
            pl.BlockSpec((lt, width), lambda t: (t, u_col)),
            pl.BlockSpec((lt, width), lambda t: (t, u_col + 1)),
            pl.BlockSpec(w_spatial.shape, lambda t: (0, 0, 0)),
            pl.BlockSpec((ck, LANES), lambda t: (0, 0)),
            pl.BlockSpec((1, width), lambda t: (0, 0)),
            pl.BlockSpec((1, width), lambda t: (0, 0)),
        ],
        out_specs=out_specs,
        out_shape=out_shape,
        compiler_params=pltpu.CompilerParams(
            dimension_semantics=("arbitrary",), vmem_limit_bytes=32 * MIB),
    )(proj, proj, w_spatial, bst, ln_g.reshape(1, width), ln_b.reshape(1, width))


def _mix_body(od_ref, om_ref, x_ref, gt_ref, w_ref, lng_ref, lnb_ref, o_ref, w_sc, *, seq_tiles, alpha):
    tm, d = o_ref.shape
    kd = od_ref.shape[1]

    @pl.when(pl.program_id(0) == 0)
    def _():
        w_sc[...] = w_ref[...].astype(BF16)

    sb, rb, _ = x_ref.shape
    rc = min(FFN_RC, tm)
    for c in range(tm // rc):
        rs = slice(c * rc, (c + 1) * rc)
        mix = _dot(od_ref[rs, :], w_sc[:kd, :]) + _dot(om_ref[rs, :], w_sc[kd:, :])
        if sb == 1:
            x = x_ref[:, rs, :]
            gate = _mod_rows(gt_ref, seq_tiles)[:, None, :]
        else:
            ss = slice(c * (rc // rb), (c + 1) * (rc // rb))
            x = x_ref[ss]
            gate = gt_ref[ss, :][:, None, :]
        y = alpha * x + gate * mix.reshape(x.shape)
        o_ref[rs, :] = _layer_norm(y.reshape(rc, d), lng_ref[...], lnb_ref[...])


def _mix(o_dn, o_mlp, x3, mod, sub, w_out, layer, ln_g, ln_b, alpha):
    n_seq, n_rows, d = x3.shape
    grp = _Group(n_seq, n_rows, MIX_TM)
    tm = grp.tm
    out = pl.pallas_call(
        functools.partial(_mix_body, seq_tiles=grp.seq_tiles, alpha=alpha),
        name="out_mix_ln",
        grid=(grp.n_tiles,),
        in_specs=[
            pl.BlockSpec((tm, o_dn.shape[1]), lambda m: (m, 0)),
            pl.BlockSpec((tm, o_mlp.shape[1]), lambda m: (m, 0)),
            grp.x_spec(d, 1),
            grp.mod_spec(mod, d, sub * 3 + 2, 1),
            pl.BlockSpec((None,) + w_out.shape[1:], lambda m: (layer, 0, 0), pipeline_mode=pl.Buffered(1)),
            pl.BlockSpec((1, d), lambda m: (0, 0)),
            pl.BlockSpec((1, d), lambda m: (0, 0)),
        ],
        out_specs=pl.BlockSpec((tm, d), lambda m: (m, 0)),
        out_shape=jax.ShapeDtypeStruct((n_seq * n_rows, d), F32),
        scratch_shapes=[pltpu.VMEM(w_out.shape[1:], BF16)],
        compiler_params=pltpu.CompilerParams(
            dimension_semantics=("arbitrary",), vmem_limit_bytes=56 * MIB),
    )(o_dn, o_mlp, x3, mod, w_out, ln_g.reshape(1, d), ln_b.reshape(1, d))
    return out.reshape(n_seq, n_rows, d)


def _trunk_layer(x3, mod, wts, layer, alpha, conv_buf, s0):
    n_seq, n_rows, d = x3.shape
    qkv_dim = 3 * DN_HEADS * LANES
    ffn_w = (wts["wg"], wts["wu"], wts["wd"], layer)
    x3 = _ffn(x3, mod, 0, *ffn_w, 0, wts["ln_g"][0], wts["ln_b"][0], alpha)
    proj = _proj(x3, mod, 1, wts["w_cat"])
    dn_blk = DN_CHUNK if n_rows % DN_CHUNK == 0 else n_rows
    gdn_kw = dict(n_seq=n_seq, n_rows=n_rows, blk=dn_blk, ba_col=wts["ba_col"])
    if s0 is None:
        o_dn, s_new = _gdn(proj, wts["conv_w"], wts["gate_par"], wts["dn_norm_g"], **gdn_kw)
    else:
        pad = jnp.zeros((n_seq, SUBLANES - (CONV_W - 1), qkv_dim), F32)
        buf = jnp.concatenate([pad, conv_buf], axis=1).reshape(n_seq * SUBLANES, qkv_dim)
        assert n_rows == SUBLANES
        o_dn, s_new = _gdn(proj, wts["conv_w"], wts["gate_par"], wts["dn_norm_g"],
                           conv_buf=buf, s0=s0, **gdn_kw)
    mlp_out = _mlp(proj, wts["w_spatial"], wts["b_spatial"], wts["mlp_ln_g"], wts["mlp_ln_b"],
                   blk=min(n_rows, MLP_CHUNK), want_v_rows=s0 is not None)
    o_mlp = mlp_out[0]
    v_rows = mlp_out[1].reshape(n_seq, n_rows, -1) if s0 is not None else None
    x3 = _mix(o_dn, o_mlp, x3, mod, 1, wts["w_out"], layer, wts["ln_g"][1], wts["ln_b"][1], alpha)
    x3 = _ffn(x3, mod, 2, *ffn_w, 1, wts["ln_g"][2], wts["ln_b"][2], alpha)
    new_buf = proj.reshape(n_seq, n_rows, -1)[:, n_rows - (CONV_W - 1):, :qkv_dim]
    return x3, new_buf, s_new, v_rows


def kernel(x_prompt, x_sample, c_prompt, c_sample, state_delta, state_conv, w_ada, b_ada, ln_g, ln_b,
           ffn_wg, ffn_wu, ffn_wd, w_in, conv_w, a_log, dt_bias, dn_norm_g, mlp_ln_g, mlp_ln_b,
           w_spatial, b_spatial, w_out):
    depth = w_ada.shape[0]
    bp, _, d = x_prompt.shape
    bs = x_sample.shape[0]
    nh = DN_HEADS
    alpha = (2.0 * depth) ** 0.25
    qkvz = 4 * nh * LANES
    gates = 2 * nh

    pad_rows = (-(bs + bp)) % SUBLANES
    c_all = jnp.concatenate([c_sample, c_prompt, jnp.zeros((pad_rows, d), F32)], axis=0)
    w_in_t = jnp.swapaxes(w_in, 1, 2)

    y_p, y_s = x_prompt, x_sample
    delta_p, conv_p, delta_s, conv_s, vrows_s = [], [], [], [], []
    for layer in range(depth):
        mod_s, mod_p = _ada(c_all, w_ada[layer], b_ada[layer], bs)
        gate_par = jnp.zeros((SUBLANES, LANES), F32)
        gate_par = gate_par.at[0, nh:gates].set(a_log[layer]).at[1, nh:gates].set(dt_bias[layer])
        wts = dict(
            wg=ffn_wg, wu=ffn_wu, wd=ffn_wd, ln_g=ln_g[layer], ln_b=ln_b[layer],
            w_cat=_wprep(w_in_t, layer, qkvz, gates), ba_col=(w_in.shape[-1] - gates) // LANES,
            conv_w=conv_w[layer], gate_par=gate_par,
            dn_norm_g=dn_norm_g[layer].reshape(1, LANES), mlp_ln_g=mlp_ln_g[layer], mlp_ln_b=mlp_ln_b[layer],
            w_spatial=w_spatial[layer], b_spatial=b_spatial[layer], w_out=w_out,
        )
        y_p, cb_p, ds_p, _ = _trunk_layer(y_p, mod_p, wts, layer, alpha, None, None)
        y_s, cb_s, ds_s, vr_s = _trunk_layer(y_s, mod_s, wts, layer, alpha, state_conv[layer],
                                             state_delta[layer])
        delta_p.append(ds_p)
        conv_p.append(cb_p)
        delta_s.append(ds_s)
        conv_s.append(cb_s)
        vrows_s.append(vr_s)
    return (y_p, y_s, jnp.stack(delta_p), jnp.stack(conv_p), jnp.stack(delta_s), jnp.stack(conv_s),
            jnp.stack(vrows_s))
```

```python
import functools
import math

import jax
import jax.numpy as jnp
from jax import lax
from jax.experimental import pallas as pl
from jax.experimental.pallas import tpu as pltpu

F32 = jnp.float32
BF16 = jnp.bfloat16

DN_HEADS = 8
DN_CHUNK = 64
CONV_W = 4
MLP_GROUPS = 4
MLP_CHUNK = 128
N_SUB = 3
LN_EPS = 1e-5
RMS_EPS = 1e-6

LANES = 128
SUBLANES = 8
MIB = 2 ** 20

FFN_TM = 1024
FFN_TF = 256
FFN_RC = 256
FFN_VMEM_LIMIT = 58 * MIB
PROJ_TN = 1280
PROJ_N = 6400
PREP_TC = 256
MIX_TM = 512
ADA_TN = 1024
SEQ_TILE = 128
MLP_TILE = 512


def _dot(a, b, precision=None):
    return jnp.dot(a, b, preferred_element_type=F32, precision=precision)


def _dot_nt(a, b, precision=None):
    return lax.dot_general(a, b, (((1,), (1,)), ((), ())),
                           preferred_element_type=F32, precision=precision)


def _dot_tn(a, b):
    return lax.dot_general(a, b, (((0,), (0,)), ((), ())), preferred_element_type=F32)


def _layer_norm(y, g, b):
    mu = jnp.mean(y, axis=-1, keepdims=True)
    yc = y - mu
    var = jnp.mean(yc * yc, axis=-1, keepdims=True)
    return yc * lax.rsqrt(var + LN_EPS) * g + b


def _silu(x):
    return x * jax.nn.sigmoid(x)


def _mod_rows(ref, seq_tiles):
    if seq_tiles:
        return ref[pl.ds(pl.program_id(0) // seq_tiles, 1), :]
    return ref[...]


def _ada_body(c_ref, w_ref, b_ref, ms_ref, mp_ref):
    c = c_ref[...]
    a = _silu(c).astype(BF16)
    y = _dot(a, w_ref[...].astype(BF16)) + b_ref[...]
    ns = ms_ref.shape[0]
    ms_ref[...] = y[:ns]
    mp_ref[...] = y[ns:]


def _ada(c_all, w_ada, b_ada, n_sample):
    rows, d = c_all.shape
    n = w_ada.shape[1]
    return pl.pallas_call(
        _ada_body,
        name="ada_mod",
        grid=(n // ADA_TN,),
        in_specs=[
            pl.BlockSpec((rows, d), lambda j: (0, 0)),
            pl.BlockSpec((d, ADA_TN), lambda j: (0, j)),
            pl.BlockSpec((1, ADA_TN), lambda j: (0, j)),
        ],
        out_specs=[
            pl.BlockSpec((n_sample, ADA_TN), lambda j: (0, j)),
            pl.BlockSpec((rows - n_sample, ADA_TN), lambda j: (0, j)),
        ],
        out_shape=[
            jax.ShapeDtypeStruct((n_sample, n), F32),
            jax.ShapeDtypeStruct((rows - n_sample, n), F32),
        ],
        compiler_params=pltpu.CompilerParams(
            dimension_semantics=("arbitrary",), vmem_limit_bytes=40 * MIB),
    )(c_all, w_ada, b_ada.reshape(1, n))


class _Group:
    def __init__(self, n_seq, n_rows, tm):
        if n_rows >= tm:
            assert n_rows % tm == 0
            self.sb, self.rb = 1, tm
        else:
            assert tm % n_rows == 0 and n_seq % (tm // n_rows) == 0
            self.sb, self.rb = tm // n_rows, n_rows
        self.n_seq, self.n_rows = n_seq, n_rows
        self.tiles_r = n_rows // self.rb
        self.tm = self.sb * self.rb
        self.n_tiles = (n_seq // self.sb) * self.tiles_r
        self.seq_tiles = self.tiles_r if self.sb == 1 else 0

    def x_spec(self, d, ngrid, **kw):
        tr = self.tiles_r
        if ngrid == 2:
            return pl.BlockSpec((self.sb, self.rb, d), lambda m, f: (m // tr, m % tr, 0), **kw)
        return pl.BlockSpec((self.sb, self.rb, d), lambda m: (m // tr, m % tr, 0), **kw)

    def mod_spec(self, mod, d, col, ngrid):
        if self.sb == 1:
            rows, row_blk = mod.shape[0], (lambda m: 0)
        else:
            rows, row_blk = self.sb, (lambda m: m)
        if ngrid == 2:
            return pl.BlockSpec((rows, d), lambda m, f: (row_blk(m), col))
        return pl.BlockSpec((rows, d), lambda m: (row_blk(m), col))


def _ffn_body(x_ref, sh_ref, sc_ref, gt_ref, wg_ref, wu_ref, wd_ref, lng_ref, lnb_ref,
              o_ref, h_sc, *, seq_tiles, alpha):
    f = pl.program_id(1)
    nf = pl.num_programs(1)
    tm, d = o_ref.shape
    sb, rb, _ = x_ref.shape
    rc = min(FFN_RC, tm)
    n_chunks = tm // rc

    def x_chunk(c):
        if sb == 1:
            return x_ref[:, c * rc:(c + 1) * rc, :]
        return x_ref[c * (rc // rb):(c + 1) * (rc // rb)]

    def mod_chunk(ref, c):
        if seq_tiles:
            return _mod_rows(ref, seq_tiles)
        return ref[c * (rc // rb):(c + 1) * (rc // rb), :]

    def weights():
        return wg_ref[...].astype(BF16), wu_ref[...].astype(BF16), wd_ref[...].astype(BF16)

    def gate_up(h16, wg16, wu16):
        return (_silu(_dot(h16, wg16)) * _dot(h16, wu16)).astype(BF16)

    @pl.when(f == 0)
    def _():
        wg16, wu16, wd16 = weights()
        for c in range(n_chunks):
            rs = slice(c * rc, (c + 1) * rc)
            x = x_chunk(c)
            h = x * (1.0 + mod_chunk(sc_ref, c)[:, None, :]) + mod_chunk(sh_ref, c)[:, None, :]
            h16 = h.reshape(rc, d).astype(BF16)
            h_sc[rs, :] = h16
            o_ref[rs, :] = _dot(gate_up(h16, wg16, wu16), wd16)

    @pl.when(jnp.logical_and(f > 0, f < nf - 1))
    def _():
        wg16, wu16, wd16 = weights()
        o_ref[...] += _dot(gate_up(h_sc[...], wg16, wu16), wd16)

    @pl.when(f == nf - 1)
    def _():
        wg16, wu16, wd16 = weights()
        for c in range(n_chunks):
            rs = slice(c * rc, (c + 1) * rc)
            acc = o_ref[rs, :] + _dot(gate_up(h_sc[rs, :], wg16, wu16), wd16)
            x = x_chunk(c)
            gate = mod_chunk(gt_ref, c)[:, None, :]
            y = alpha * x + (0.5 * gate) * acc.reshape(x.shape)
            o_ref[rs, :] = _layer_norm(y.reshape(rc, d), lng_ref[...], lnb_ref[...])


def _ffn(x3, mod, sub, wg, wu, wd, layer, slot, ln_g, ln_b, alpha):
    n_seq, n_rows, d = x3.shape
    dff = wg.shape[-1]
    grp = _Group(n_seq, n_rows, FFN_TM)
    tm = grp.tm
    rc = min(FFN_RC, tm)
    assert dff // FFN_TF >= 2 and tm % rc == 0 and (grp.rb % rc == 0 if grp.sb == 1 else rc % grp.rb == 0)
    out = pl.pallas_call(
        functools.partial(_ffn_body, seq_tiles=grp.seq_tiles, alpha=alpha),
        name="swiglu_ln",
        grid=(grp.n_tiles, dff // FFN_TF),
        in_specs=[
            grp.x_spec(d, 2),
            grp.mod_spec(mod, d, sub * 3 + 0, 2),
            grp.mod_spec(mod, d, sub * 3 + 1, 2),
            grp.mod_spec(mod, d, sub * 3 + 2, 2),
            pl.BlockSpec((None, None, d, FFN_TF), lambda m, f: (layer, slot, 0, f)),
            pl.BlockSpec((None, None, d, FFN_TF), lambda m, f: (layer, slot, 0, f)),
            pl.BlockSpec((None, None, FFN_TF, d), lambda m, f: (layer, slot, f, 0)),
            pl.BlockSpec((1, d), lambda m, f: (0, 0)),
            pl.BlockSpec((1, d), lambda m, f: (0, 0)),
        ],
        out_specs=pl.BlockSpec((tm, d), lambda m, f: (m, 0)),
        out_shape=jax.ShapeDtypeStruct((n_seq * n_rows, d), F32),
        scratch_shapes=[pltpu.VMEM((tm, d), BF16)],
        compiler_params=pltpu.CompilerParams(
            dimension_semantics=("arbitrary", "arbitrary"), vmem_limit_bytes=FFN_VMEM_LIMIT),
    )(x3, mod, mod, mod, wg, wu, wd, ln_g.reshape(1, d), ln_b.reshape(1, d))
    return out.reshape(n_seq, n_rows, d)


def _wprep_body(w_ref, o_ref, *, qkvz, gates):
    n_in, cols = w_ref.shape
    n_out = o_ref.shape[0]
    mlp = n_in - qkvz - gates
    o_ref[0:qkvz, :] = w_ref[0:qkvz, :].astype(BF16)
    o_ref[qkvz:qkvz + mlp, :] = w_ref[qkvz + gates:n_in, :].astype(BF16)
    o_ref[qkvz + mlp:n_in, :] = w_ref[qkvz:qkvz + gates, :].astype(BF16)
    o_ref[n_in:n_out, :] = jnp.zeros((n_out - n_in, cols), BF16)


def _wprep(w_in_t, layer, qkvz, gates):
    _, n_in, d = w_in_t.shape
    assert n_in <= PROJ_N and (n_in - gates) % LANES == 0
    return pl.pallas_call(
        functools.partial(_wprep_body, qkvz=qkvz, gates=gates),
        name="proj_weight_prep",
        grid=(d // PREP_TC,),
        in_specs=[pl.BlockSpec((None, n_in, PREP_TC), lambda i: (layer, 0, i))],
        out_specs=pl.BlockSpec((PROJ_N, PREP_TC), lambda i: (0, i)),
        out_shape=jax.ShapeDtypeStruct((PROJ_N, d), BF16),
        compiler_params=pltpu.CompilerParams(
            dimension_semantics=("arbitrary",), vmem_limit_bytes=40 * MIB),
    )(w_in_t)


def _proj_body(x_ref, sh_ref, sc_ref, w_ref, o_ref, h_sc, *, seq_tiles):
    tm, d = h_sc.shape

    @pl.when(pl.program_id(1) == 0)
    def _():
        x = x_ref[...]
        h = x * (1.0 + _mod_rows(sc_ref, seq_tiles)[:, None, :]) + _mod_rows(sh_ref, seq_tiles)[:, None, :]
        h_sc[...] = h.reshape(tm, d).astype(BF16)

    o_ref[...] = _dot_nt(h_sc[...], w_ref[...])


def _proj(x3, mod, sub, w_cat_t):
    n_seq, n_rows, d = x3.shape
    n = w_cat_t.shape[0]
    grp = _Group(n_seq, n_rows, FFN_TM)
    tm = grp.tm
    return pl.pallas_call(
        functools.partial(_proj_body, seq_tiles=grp.seq_tiles),
        name="in_proj",
        grid=(grp.n_tiles, n // PROJ_TN),
        in_specs=[
            grp.x_spec(d, 2),
            grp.mod_spec(mod, d, sub * 3 + 0, 2),
            grp.mod_spec(mod, d, sub * 3 + 1, 2),
            pl.BlockSpec((PROJ_TN, d), lambda m, j: (j, 0)),
        ],
        out_specs=pl.BlockSpec((tm, PROJ_TN), lambda m, j: (m, j)),
        out_shape=jax.ShapeDtypeStruct((n_seq * n_rows, n), F32),
        scratch_shapes=[pltpu.VMEM((tm, d), BF16)],
        compiler_params=pltpu.CompilerParams(
            dimension_semantics=("arbitrary", "arbitrary"), vmem_limit_bytes=48 * MIB),
    )(x3, mod, mod, w_cat_t)


def _block_masks(n, blk):
    shift = int(math.log2(blk))
    i = lax.broadcasted_iota(jnp.int32, (n, n), 0)
    j = lax.broadcasted_iota(jnp.int32, (n, n), 1)
    same = (i >> shift) == (j >> shift)
    return i, j, same


def _gdn_body(*refs, blk, chained, tiles_per_seq, dk_scale):
    if chained:
        (qkv_ref, z_ref, ba_ref, cw_ref, gp_ref, ng_ref, o_ref, st_ref, xc_sc, act_sc) = refs
    else:
        (qkv_ref, z_ref, ba_ref, buf_ref, sin_ref, cw_ref, gp_ref, ng_ref, o_ref, st_ref, act_sc) = refs
    lt, cqkv = qkv_ref.shape
    hd = LANES
    nh = cqkv // (3 * hd)
    t = pl.program_id(0)
    halo = SUBLANES
    ngrp = lt // SUBLANES

    if chained:
        first = (t % tiles_per_seq) == 0

        @pl.when(first)
        def _():
            xc_sc[pl.ds(0, halo), :] = jnp.zeros((halo, cqkv), F32)
            st_ref[...] = jnp.zeros_like(st_ref)

        @pl.when(jnp.logical_not(first))
        def _():
            xc_sc[pl.ds(0, halo), :] = xc_sc[pl.ds(lt, halo), :]

        xc_sc[pl.ds(halo, lt), :] = qkv_ref[...]
    else:
        assert blk == SUBLANES

    cblk = 2 * LANES
    row_in_grp = lax.broadcasted_iota(jnp.int32, (ngrp, SUBLANES, cblk), 1)
    for c0 in range(0, cqkv, cblk):
        cs = slice(c0, c0 + cblk)
        if chained:
            xe = xc_sc[:, cs].reshape(ngrp + 1, SUBLANES, cblk)
            cur = xe[1:]
        else:
            cur = qkv_ref[:, cs].reshape(ngrp, SUBLANES, cblk)
            prev = buf_ref[:, cs].reshape(ngrp, SUBLANES, cblk)
        y = None
        for j in range(CONV_W):
            d = CONV_W - 1 - j
            if d == 0:
                term = cur
            elif chained:
                rot = pltpu.roll(xe, d, axis=1)
                term = jnp.where(row_in_grp >= d, rot[1:], rot[:-1])
            else:
                term = jnp.where(row_in_grp >= d, pltpu.roll(cur, d, axis=1), pltpu.roll(prev, d, axis=1))
            term = cw_ref[j:j + 1, cs][None] * term
            y = term if y is None else y + term
        act_sc[:, cs] = _silu(y).reshape(lt, cblk)

    ba = ba_ref[...]
    beta_all = jax.nn.sigmoid(ba)
    g_all = -jnp.exp(gp_ref[0:1, :]) * jax.nn.softplus(ba + gp_ref[1:2, :])

    i, j, same = _block_masks(lt, blk)
    incl = same & (i >= j)
    strict = same & (i > j)
    hi = lax.Precision.HIGHEST
    gc_all = _dot(incl.astype(F32), g_all, hi)
    gl_all = _dot((j == (i | (blk - 1))).astype(F32), gc_all, hi)
    gc_t = gc_all.T
    eye = (i == j).astype(F32)
    pair_masks = []
    for lvl in range(int(math.log2(blk))):
        pair_masks.append(((i >> (lvl + 1)) == (j >> (lvl + 1))) & ((i >> lvl) != (j >> lvl)))
    nblk = lt // blk

    heads = range(nh)
    hp = []
    for h in heads:
        q = act_sc[:, h * hd:(h + 1) * hd]
        k = act_sc[:, (nh + h) * hd:(nh + h + 1) * hd]
        v = act_sc[:, (2 * nh + h) * hd:(2 * nh + h + 1) * hd]
        q = q * lax.rsqrt(jnp.sum(q * q, axis=-1, keepdims=True) + RMS_EPS) * dk_scale
        k = k * lax.rsqrt(jnp.sum(k * k, axis=-1, keepdims=True) + RMS_EPS)
        beta = beta_all[:, h:h + 1]
        gcc = gc_all[:, nh + h:nh + h + 1]
        gcr = gc_t[nh + h:nh + h + 1, :]
        glc = gl_all[:, nh + h:nh + h + 1]
        decay = jnp.exp(jnp.where(incl, gcc - gcr, -jnp.inf))
        kb = k * beta
        k16 = k.astype(BF16)
        egc = jnp.exp(gcc)
        hp.append(dict(
            decay=decay, k16=k16, kb16=kb.astype(BF16), q16=q.astype(BF16), glc=glc,
            rhs16=jnp.concatenate([v * beta, kb * egc], axis=1).astype(BF16),
            qd=q * egc, kd=k * jnp.exp(glc - gcc)))

    ms = [_dot_nt(p["kb16"], p["k16"]) * jnp.where(strict, p["decay"], 0.0) for p in hp]
    dinvs = [eye - jnp.where(pair_masks[0], m, 0.0) for m in ms]
    for pm in pair_masks[1:]:
        d16s = [d.astype(BF16) for d in dinvs]
        t16s = [_dot(d16, jnp.where(pm, m, 0.0).astype(BF16)).astype(BF16) for d16, m in zip(d16s, ms)]
        dinvs = [d - _dot(t16, d16) for d, t16, d16 in zip(dinvs, t16s, d16s)]
    xs = [_dot(d.astype(BF16), p["rhs16"]) for d, p in zip(dinvs, hp)]
    qk16s = [(_dot_nt(p["q16"], p["k16"]) * p["decay"]).astype(BF16) for p in hp]

    vn_parts = [[] for _ in heads]
    os_parts = [[] for _ in heads]
    for b in range(nblk):
        rs = slice(b * blk, (b + 1) * blk)
        s0s = [st_ref[0, h] if chained else sin_ref[b, h] for h in heads]
        rr = [_dot(jnp.concatenate([x[rs, hd:], p["qd"][rs]], axis=0).astype(BF16), s0.astype(BF16))
              for x, p, s0 in zip(xs, hp, s0s)]
        vns = [x[rs, :hd] - r[:blk] for x, r in zip(xs, rr)]
        for h in heads:
            p = hp[h]
            s_new = (s0s[h] * jnp.exp(p["glc"][b * blk:b * blk + 1, :])
                     + _dot_tn(p["kd"][rs].astype(BF16), vns[h].astype(BF16)))
            if chained:
                st_ref[0, h] = s_new
            else:
                st_ref[b, h] = s_new
            vn_parts[h].append(vns[h])
            os_parts[h].append(rr[h][blk:])
    for h in heads:
        vn_all = jnp.concatenate(vn_parts[h], axis=0)
        o = jnp.concatenate(os_parts[h], axis=0) + _dot(qk16s[h], vn_all.astype(BF16))
        o = o * lax.rsqrt(jnp.mean(o * o, axis=-1, keepdims=True) + RMS_EPS) * ng_ref[...]
        o = o * _silu(z_ref[:, h * hd:(h + 1) * hd])
        o_ref[:, h * hd:(h + 1) * hd] = o.astype(BF16)


def _gdn(proj, conv_w, gate_par, norm_g, *, n_seq, n_rows, blk, ba_col, conv_buf=None, s0=None):
    t_rows = proj.shape[0]
    nh = DN_HEADS
    hd = LANES
    cqkv = 3 * nh * hd
    lt = SEQ_TILE
    assert lt % blk == 0 and blk & (blk - 1) == 0 and blk >= 2
    chained = s0 is None
    n_tiles = t_rows // lt
    common_in = [
        pl.BlockSpec((lt, cqkv), lambda t: (t, 0)),
        pl.BlockSpec((lt, nh * hd), lambda t: (t, 3)),
        pl.BlockSpec((lt, LANES), lambda t: (t, ba_col)),
    ]
    par_in = [
        pl.BlockSpec(conv_w.shape, lambda t: (0, 0)),
        pl.BlockSpec(gate_par.shape, lambda t: (0, 0)),
        pl.BlockSpec((1, hd), lambda t: (0, 0)),
    ]
    o_spec = pl.BlockSpec((lt, nh * hd), lambda t: (t, 0))
    o_shape = jax.ShapeDtypeStruct((t_rows, nh * hd), BF16)
    scratch = []
    if chained:
        assert n_rows % lt == 0 and blk == DN_CHUNK
        tps = n_rows // lt
        in_specs = common_in + par_in
        args = (proj, proj, proj, conv_w, gate_par, norm_g)
        st_spec = pl.BlockSpec((1, nh, hd, hd), lambda t: (t // tps, 0, 0, 0))
        scratch.append(pltpu.VMEM((lt + SUBLANES, cqkv), F32))
    else:
        assert blk == n_rows and conv_buf is not None
        tps = 0
        spt = lt // n_rows
        in_specs = common_in + [
            pl.BlockSpec((lt, cqkv), lambda t: (t, 0)),
            pl.BlockSpec((spt, nh, hd, hd), lambda t: (t, 0, 0, 0)),
        ] + par_in
        args = (proj, proj, proj, conv_buf, s0, conv_w, gate_par, norm_g)
        st_spec = pl.BlockSpec((spt, nh, hd, hd), lambda t: (t, 0, 0, 0))
    scratch.append(pltpu.VMEM((lt, cqkv), F32))
    return pl.pallas_call(
        functools.partial(_gdn_body, blk=blk, chained=chained, tiles_per_seq=tps,
                          dk_scale=float(hd) ** -0.5),
        name="gdn_chained" if chained else "gdn_stateful",
        grid=(n_tiles,),
        in_specs=in_specs,
        out_specs=[o_spec, st_spec],
        out_shape=[o_shape, jax.ShapeDtypeStruct((n_seq, nh, hd, hd), F32)],
        scratch_shapes=scratch,
        compiler_params=pltpu.CompilerParams(
            dimension_semantics=("arbitrary",), vmem_limit_bytes=56 * MIB),
    )(*args)


def _mlp_body(u_ref, v_ref, ws_ref, bst_ref, lng_ref, lnb_ref, o_ref, *maybe_vr_ref, blk):
    ck = ws_ref.shape[1]
    gd = u_ref.shape[1] // MLP_GROUPS
    i, j, same = _block_masks(ck, blk)
    incl = same & (i >= j)
    hi = lax.Precision.HIGHEST
    if blk != ck:
        rep = (j == (i & (blk - 1))).astype(F32)
        bias_all = _dot(rep, bst_ref[...], hi)
    else:
        bias_all = bst_ref[...]
    for g in range(MLP_GROUPS):
        cs = slice(g * gd, (g + 1) * gd)
        wsp = ws_ref[g]
        if blk != ck:
            wsp = _dot_nt(_dot(rep, wsp, hi), rep, hi)
        wsp = jnp.where(incl, wsp, 0.0).astype(BF16)
        for c in range(u_ref.shape[0] // ck):
            rs = slice(c * ck, (c + 1) * ck)
            uu = jax.nn.gelu(u_ref[rs, cs])
            vv = _layer_norm(jax.nn.gelu(v_ref[rs, cs]), lng_ref[:, cs], lnb_ref[:, cs])
            if maybe_vr_ref:
                maybe_vr_ref[0][rs, cs] = vv
            s = _dot(wsp, vv.astype(BF16)) + bias_all[:, g:g + 1]
            o_ref[rs, cs] = (uu * s).astype(BF16)


def _mlp(proj, w_spatial, b_spatial, ln_g, ln_b, *, blk, want_v_rows):
    t_rows = proj.shape[0]
    lt = MLP_TILE
    ck = w_spatial.shape[1]
    width = ln_g.size
    assert ck == MLP_CHUNK and lt % ck == 0 and ck % blk == 0 and t_rows % lt == 0
    u_col = (4 * DN_HEADS * LANES) // width
    bst = jnp.zeros((ck, LANES), F32).at[:, :MLP_GROUPS].set(b_spatial.T)
    o_spec = pl.BlockSpec((lt, width), lambda t: (t, 0))
    out_specs = [o_spec]
    out_shape = [jax.ShapeDtypeStruct((t_rows, width), BF16)]
    if want_v_rows:
        out_specs.append(o_spec)
        out_shape.append(jax.ShapeDtypeStruct((t_rows, width), F32))
    return pl.pallas_call(
        functools.partial(_mlp_body, blk=blk),
        name="spatial_mlp",
        grid=(t_rows // lt,),
        in_specs=[
            pl.BlockSpec((lt, width), lambda t: (t, u_col)),
            pl.BlockSpec((lt, width), lambda t: (t, u_col + 1)),
            pl.BlockSpec(w_spatial.shape, lambda t: (0, 0, 0)),
            pl.BlockSpec((ck, LANES), lambda t: (0, 0)),
            pl.BlockSpec((1, width), lambda t: (0, 0)),
            pl.BlockSpec((1, width), lambda t: (0, 0)),
        ],
        out_specs=out_specs,
        out_shape=out_shape,
        compiler_params=pltpu.CompilerParams(
            dimension_semantics=("arbitrary",), vmem_limit_bytes=32 * MIB),
    )(proj, proj, w_spatial, bst, ln_g.reshape(1, width), ln_b.reshape(1, width))


def _mix_body(od_ref, om_ref, x_ref, gt_ref, w_ref, lng_ref, lnb_ref, o_ref, w_sc, *, seq_tiles, alpha):
    tm, d = o_ref.shape
    kd = od_ref.shape[1]

    @pl.when(pl.program_id(0) == 0)
    def _():
        w_sc[...] = w_ref[...].astype(BF16)

    sb, rb, _ = x_ref.shape
    rc = min(FFN_RC, tm)
    for c in range(tm // rc):
        rs = slice(c * rc, (c + 1) * rc)
        mix = _dot(od_ref[rs, :], w_sc[:kd, :]) + _dot(om_ref[rs, :], w_sc[kd:, :])
        if sb == 1:
            x = x_ref[:, rs, :]
            gate = _mod_rows(gt_ref, seq_tiles)[:, None, :]
        else:
            ss = slice(c * (rc // rb), (c + 1) * (rc // rb))
            x = x_ref[ss]
            gate = gt_ref[ss, :][:, None, :]
        y = alpha * x + gate * mix.reshape(x.shape)
        o_ref[rs, :] = _layer_norm(y.reshape(rc, d), lng_ref[...], lnb_ref[...])


def _mix(o_dn, o_mlp, x3, mod, sub, w_out, layer, ln_g, ln_b, alpha):
    n_seq, n_rows, d = x3.shape
    grp = _Group(n_seq, n_rows, MIX_TM)
    tm = grp.tm
    out = pl.pallas_call(
        functools.partial(_mix_body, seq_tiles=grp.seq_tiles, alpha=alpha),
        name="out_mix_ln",
        grid=(grp.n_tiles,),
        in_specs=[
            pl.BlockSpec((tm, o_dn.shape[1]), lambda m: (m, 0)),
            pl.BlockSpec((tm, o_mlp.shape[1]), lambda m: (m, 0)),
            grp.x_spec(d, 1),
            grp.mod_spec(mod, d, sub * 3 + 2, 1),
            pl.BlockSpec((None,) + w_out.shape[1:], lambda m: (layer, 0, 0), pipeline_mode=pl.Buffered(1)),
            pl.BlockSpec((1, d), lambda m: (0, 0)),
            pl.BlockSpec((1, d), lambda m: (0, 0)),
        ],
        out_specs=pl.BlockSpec((tm, d), lambda m: (m, 0)),
        out_shape=jax.ShapeDtypeStruct((n_seq * n_rows, d), F32),
        scratch_shapes=[pltpu.VMEM(w_out.shape[1:], BF16)],
        compiler_params=pltpu.CompilerParams(
            dimension_semantics=("arbitrary",), vmem_limit_bytes=56 * MIB),
    )(o_dn, o_mlp, x3, mod, w_out, ln_g.reshape(1, d), ln_b.reshape(1, d))
    return out.reshape(n_seq, n_rows, d)


def _trunk_layer(x3, mod, wts, layer, alpha, conv_buf, s0):
    n_seq, n_rows, d = x3.shape
    qkv_dim = 3 * DN_HEADS * LANES
    ffn_w = (wts["wg"], wts["wu"], wts["wd"], layer)
    x3 = _ffn(x3, mod, 0, *ffn_w, 0, wts["ln_g"][0], wts["ln_b"][0], alpha)
    proj = _proj(x3, mod, 1, wts["w_cat"])
    dn_blk = DN_CHUNK if n_rows % DN_CHUNK == 0 else n_rows
    gdn_kw = dict(n_seq=n_seq, n_rows=n_rows, blk=dn_blk, ba_col=wts["ba_col"])
    if s0 is None:
        o_dn, s_new = _gdn(proj, wts["conv_w"], wts["gate_par"], wts["dn_norm_g"], **gdn_kw)
    else:
        pad = jnp.zeros((n_seq, SUBLANES - (CONV_W - 1), qkv_dim), F32)
        buf = jnp.concatenate([pad, conv_buf], axis=1).reshape(n_seq * SUBLANES, qkv_dim)
        assert n_rows == SUBLANES
        o_dn, s_new = _gdn(proj, wts["conv_w"], wts["gate_par"], wts["dn_norm_g"],
                           conv_buf=buf, s0=s0, **gdn_kw)
    mlp_out = _mlp(proj, wts["w_spatial"], wts["b_spatial"], wts["mlp_ln_g"], wts["mlp_ln_b"],
                   blk=min(n_rows, MLP_CHUNK), want_v_rows=s0 is not None)
    o_mlp = mlp_out[0]
    v_rows = mlp_out[1].reshape(n_seq, n_rows, -1) if s0 is not None else None
    x3 = _mix(o_dn, o_mlp, x3, mod, 1, wts["w_out"], layer, wts["ln_g"][1], wts["ln_b"][1], alpha)
    x3 = _ffn(x3, mod, 2, *ffn_w, 1, wts["ln_g"][2], wts["ln_b"][2], alpha)
    new_buf = proj.reshape(n_seq, n_rows, -1)[:, n_rows - (CONV_W - 1):, :qkv_dim]
    return x3, new_buf, s_new, v_rows


def kernel(x_prompt, x_sample, c_prompt, c_sample, state_delta, state_conv, w_ada, b_ada, ln_g, ln_b,
           ffn_wg, ffn_wu, ffn_wd, w_in, conv_w, a_log, dt_bias, dn_norm_g, mlp_ln_g, mlp_ln_b,
           w_spatial, b_spatial, w_out):
    depth = w_ada.shape[0]
    bp, _, d = x_prompt.shape
    bs = x_sample.shape[0]
    nh = DN_HEADS
    alpha = (2.0 * depth) ** 0.25
    qkvz = 4 * nh * LANES
    gates = 2 * nh

    pad_rows = (-(bs + bp)) % SUBLANES
    c_all = jnp.concatenate([c_sample, c_prompt, jnp.zeros((pad_rows, d), F32)], axis=0)
    w_in_t = jnp.swapaxes(w_in, 1, 2)

    y_p, y_s = x_prompt, x_sample
    delta_p, conv_p, delta_s, conv_s, vrows_s = [], [], [], [], []
    for layer in range(depth):
        mod_s, mod_p = _ada(c_all, w_ada[layer], b_ada[layer], bs)
        gate_par = jnp.zeros((SUBLANES, LANES), F32)
        gate_par = gate_par.at[0, nh:gates].set(a_log[layer]).at[1, nh:gates].set(dt_bias[layer])
        wts = dict(
            wg=ffn_wg, wu=ffn_wu, wd=ffn_wd, ln_g=ln_g[layer], ln_b=ln_b[layer],
            w_cat=_wprep(w_in_t, layer, qkvz, gates), ba_col=(w_in.shape[-1] - gates) // LANES,
            conv_w=conv_w[layer], gate_par=gate_par,
            dn_norm_g=dn_norm_g[layer].reshape(1, LANES), mlp_ln_g=mlp_ln_g[layer], mlp_ln_b=mlp_ln_b[layer],
            w_spatial=w_spatial[layer], b_spatial=b_spatial[layer], w_out=w_out,
        )
        y_p, cb_p, ds_p, _ = _trunk_layer(y_p, mod_p, wts, layer, alpha, None, None)
        y_s, cb_s, ds_s, vr_s = _trunk_layer(y_s, mod_s, wts, layer, alpha, state_conv[layer],
                                             state_delta[layer])
        delta_p.append(ds_p)
        conv_p.append(cb_p)
        delta_s.append(ds_s)
        conv_s.append(cb_s)
        vrows_s.append(vr_s)
    return (y_p, y_s, jnp.stack(delta_p), jnp.stack(conv_p), jnp.stack(delta_s), jnp.stack(conv_s),
            jnp.stack(vrows_s))
```

```python
import functools
import math

import jax
import jax.numpy as jnp
from jax import lax
from jax.experimental import pallas as pl
from jax.experimental.pallas import tpu as pltpu

F32 = jnp.float32
BF16 = jnp.bfloat16

DN_HEADS = 8
DN_CHUNK = 64
CONV_W = 4
MLP_GROUPS = 4
MLP_CHUNK = 128
N_SUB = 3
LN_EPS = 1e-5
RMS_EPS = 1e-6

LANES = 128
SUBLANES = 8
MIB = 2 ** 20

FFN_TM = 1024
FFN_TF = 256
FFN_RC = 256
FFN_VMEM_LIMIT = 58 * MIB
PROJ_TN = 1280
PROJ_N = 6400
PREP_TC = 256
MIX_TM = 512
ADA_TN = 1024
SEQ_TILE = 128
GDN_TILE_CHAINED = 256
MLP_TILE = 512


def _dot(a, b, precision=None):
    return jnp.dot(a, b, preferred_element_type=F32, precision=precision)


def _dot_nt(a, b, precision=None):
    return lax.dot_general(a, b, (((1,), (1,)), ((), ())),
                           preferred_element_type=F32, precision=precision)


def _dot_tn(a, b):
    return lax.dot_general(a, b, (((0,), (0,)), ((), ())), preferred_element_type=F32)


def _layer_norm(y, g, b):
    mu = jnp.mean(y, axis=-1, keepdims=True)
    yc = y - mu
    var = jnp.mean(yc * yc, axis=-1, keepdims=True)
    return yc * lax.rsqrt(var + LN_EPS) * g + b


def _silu(x):
    return x * jax.nn.sigmoid(x)


def _mod_rows(ref, seq_tiles):
    if seq_tiles:
        return ref[pl.ds(pl.program_id(0) // seq_tiles, 1), :]
    return ref[...]


def _ada_body(c_ref, w_ref, b_ref, ms_ref, mp_ref):
    c = c_ref[...]
    a = _silu(c).astype(BF16)
    y = _dot(a, w_ref[...].astype(BF16)) + b_ref[...]
    ns = ms_ref.shape[0]
    ms_ref[...] = y[:ns]
    mp_ref[...] = y[ns:]


def _ada(c_all, w_ada, b_ada, n_sample):
    rows, d = c_all.shape
    n = w_ada.shape[1]
    return pl.pallas_call(
        _ada_body,
        name="ada_mod",
        grid=(n // ADA_TN,),
        in_specs=[
            pl.BlockSpec((rows, d), lambda j: (0, 0)),
            pl.BlockSpec((d, ADA_TN), lambda j: (0, j)),
            pl.BlockSpec((1, ADA_TN), lambda j: (0, j)),
        ],
        out_specs=[
            pl.BlockSpec((n_sample, ADA_TN), lambda j: (0, j)),
            pl.BlockSpec((rows - n_sample, ADA_TN), lambda j: (0, j)),
        ],
        out_shape=[
            jax.ShapeDtypeStruct((n_sample, n), F32),
            jax.ShapeDtypeStruct((rows - n_sample, n), F32),
        ],
        compiler_params=pltpu.CompilerParams(
            dimension_semantics=("arbitrary",), vmem_limit_bytes=40 * MIB),
    )(c_all, w_ada, b_ada.reshape(1, n))


class _Group:
    def __init__(self, n_seq, n_rows, tm):
        if n_rows >= tm:
            assert n_rows % tm == 0
            self.sb, self.rb = 1, tm
        else:
            assert tm % n_rows == 0 and n_seq % (tm // n_rows) == 0
            self.sb, self.rb = tm // n_rows, n_rows
        self.n_seq, self.n_rows = n_seq, n_rows
        self.tiles_r = n_rows // self.rb
        self.tm = self.sb * self.rb
        self.n_tiles = (n_seq // self.sb) * self.tiles_r
        self.seq_tiles = self.tiles_r if self.sb == 1 else 0

    def x_spec(self, d, ngrid, **kw):
        tr = self.tiles_r
        if ngrid == 2:
            return pl.BlockSpec((self.sb, self.rb, d), lambda m, f: (m // tr, m % tr, 0), **kw)
        return pl.BlockSpec((self.sb, self.rb, d), lambda m: (m // tr, m % tr, 0), **kw)

    def mod_spec(self, mod, d, col, ngrid):
        if self.sb == 1:
            rows, row_blk = mod.shape[0], (lambda m: 0)
        else:
            rows, row_blk = self.sb, (lambda m: m)
        if ngrid == 2:
            return pl.BlockSpec((rows, d), lambda m, f: (row_blk(m), col))
        return pl.BlockSpec((rows, d), lambda m: (row_blk(m), col))


def _ffn_body(x_ref, sh_ref, sc_ref, gt_ref, wg_ref, wu_ref, wd_ref, lng_ref, lnb_ref,
              o_ref, h_sc, *, seq_tiles, alpha):
    f = pl.program_id(1)
    nf = pl.num_programs(1)
    tm, d = o_ref.shape
    sb, rb, _ = x_ref.shape
    rc = min(FFN_RC, tm)
    n_chunks = tm // rc

    def x_chunk(c):
        if sb == 1:
            return x_ref[:, c * rc:(c + 1) * rc, :]
        return x_ref[c * (rc // rb):(c + 1) * (rc // rb)]

    def mod_chunk(ref, c):
        if seq_tiles:
            return _mod_rows(ref, seq_tiles)
        return ref[c * (rc // rb):(c + 1) * (rc // rb), :]

    def weights():
        return wg_ref[...].astype(BF16), wu_ref[...].astype(BF16), wd_ref[...].astype(BF16)

    def gate_up(h16, wg16, wu16):
        return (_silu(_dot(h16, wg16)) * _dot(h16, wu16)).astype(BF16)

    @pl.when(f == 0)
    def _():
        wg16, wu16, wd16 = weights()
        for c in range(n_chunks):
            rs = slice(c * rc, (c + 1) * rc)
            x = x_chunk(c)
            h = x * (1.0 + mod_chunk(sc_ref, c)[:, None, :]) + mod_chunk(sh_ref, c)[:, None, :]
            h16 = h.reshape(rc, d).astype(BF16)
            h_sc[rs, :] = h16
            o_ref[rs, :] = _dot(gate_up(h16, wg16, wu16), wd16)

    @pl.when(jnp.logical_and(f > 0, f < nf - 1))
    def _():
        wg16, wu16, wd16 = weights()
        o_ref[...] += _dot(gate_up(h_sc[...], wg16, wu16), wd16)

    @pl.when(f == nf - 1)
    def _():
        wg16, wu16, wd16 = weights()
        for c in range(n_chunks):
            rs = slice(c * rc, (c + 1) * rc)
            acc = o_ref[rs, :] + _dot(gate_up(h_sc[rs, :], wg16, wu16), wd16)
            x = x_chunk(c)
            gate = mod_chunk(gt_ref, c)[:, None, :]
            y = alpha * x + (0.5 * gate) * acc.reshape(x.shape)
            o_ref[rs, :] = _layer_norm(y.reshape(rc, d), lng_ref[...], lnb_ref[...])


def _ffn(x3, mod, sub, wg, wu, wd, layer, slot, ln_g, ln_b, alpha):
    n_seq, n_rows, d = x3.shape
    dff = wg.shape[-1]
    grp = _Group(n_seq, n_rows, FFN_TM)
    tm = grp.tm
    rc = min(FFN_RC, tm)
    assert dff // FFN_TF >= 2 and tm % rc == 0 and (grp.rb % rc == 0 if grp.sb == 1 else rc % grp.rb == 0)
    out = pl.pallas_call(
        functools.partial(_ffn_body, seq_tiles=grp.seq_tiles, alpha=alpha),
        name="swiglu_ln",
        grid=(grp.n_tiles, dff // FFN_TF),
        in_specs=[
            grp.x_spec(d, 2),
            grp.mod_spec(mod, d, sub * 3 + 0, 2),
            grp.mod_spec(mod, d, sub * 3 + 1, 2),
            grp.mod_spec(mod, d, sub * 3 + 2, 2),
            pl.BlockSpec((None, None, d, FFN_TF), lambda m, f: (layer, slot, 0, f)),
            pl.BlockSpec((None, None, d, FFN_TF), lambda m, f: (layer, slot, 0, f)),
            pl.BlockSpec((None, None, FFN_TF, d), lambda m, f: (layer, slot, f, 0)),
            pl.BlockSpec((1, d), lambda m, f: (0, 0)),
            pl.BlockSpec((1, d), lambda m, f: (0, 0)),
        ],
        out_specs=pl.BlockSpec((tm, d), lambda m, f: (m, 0)),
        out_shape=jax.ShapeDtypeStruct((n_seq * n_rows, d), F32),
        scratch_shapes=[pltpu.VMEM((tm, d), BF16)],
        compiler_params=pltpu.CompilerParams(
            dimension_semantics=("arbitrary", "arbitrary"), vmem_limit_bytes=FFN_VMEM_LIMIT),
    )(x3, mod, mod, mod, wg, wu, wd, ln_g.reshape(1, d), ln_b.reshape(1, d))
    return out.reshape(n_seq, n_rows, d)


def _wprep_body(w_ref, o_ref, *, qkvz, gates):
    n_in, cols = w_ref.shape
    n_out = o_ref.shape[0]
    mlp = n_in - qkvz - gates
    o_ref[0:qkvz, :] = w_ref[0:qkvz, :].astype(BF16)
    o_ref[qkvz:qkvz + mlp, :] = w_ref[qkvz + gates:n_in, :].astype(BF16)
    o_ref[qkvz + mlp:n_in, :] = w_ref[qkvz:qkvz + gates, :].astype(BF16)
    o_ref[n_in:n_out, :] = jnp.zeros((n_out - n_in, cols), BF16)


def _wprep(w_in_t, layer, qkvz, gates):
    _, n_in, d = w_in_t.shape
    assert n_in <= PROJ_N and (n_in - gates) % LANES == 0
    return pl.pallas_call(
        functools.partial(_wprep_body, qkvz=qkvz, gates=gates),
        name="proj_weight_prep",
        grid=(d // PREP_TC,),
        in_specs=[pl.BlockSpec((None, n_in, PREP_TC), lambda i: (layer, 0, i))],
        out_specs=pl.BlockSpec((PROJ_N, PREP_TC), lambda i: (0, i)),
        out_shape=jax.ShapeDtypeStruct((PROJ_N, d), BF16),
        compiler_params=pltpu.CompilerParams(
            dimension_semantics=("arbitrary",), vmem_limit_bytes=40 * MIB),
    )(w_in_t)


def _proj_body(x_ref, sh_ref, sc_ref, w_ref, o_ref, h_sc, *, seq_tiles):
    tm, d = h_sc.shape

    @pl.when(pl.program_id(1) == 0)
    def _():
        x = x_ref[...]
        h = x * (1.0 + _mod_rows(sc_ref, seq_tiles)[:, None, :]) + _mod_rows(sh_ref, seq_tiles)[:, None, :]
        h_sc[...] = h.reshape(tm, d).astype(BF16)

    o_ref[...] = _dot_nt(h_sc[...], w_ref[...])


def _proj(x3, mod, sub, w_cat_t):
    n_seq, n_rows, d = x3.shape
    n = w_cat_t.shape[0]
    grp = _Group(n_seq, n_rows, FFN_TM)
    tm = grp.tm
    return pl.pallas_call(
        functools.partial(_proj_body, seq_tiles=grp.seq_tiles),
        name="in_proj",
        grid=(grp.n_tiles, n // PROJ_TN),
        in_specs=[
            grp.x_spec(d, 2),
            grp.mod_spec(mod, d, sub * 3 + 0, 2),
            grp.mod_spec(mod, d, sub * 3 + 1, 2),
            pl.BlockSpec((PROJ_TN, d), lambda m, j: (j, 0)),
        ],
        out_specs=pl.BlockSpec((tm, PROJ_TN), lambda m, j: (m, j)),
        out_shape=jax.ShapeDtypeStruct((n_seq * n_rows, n), F32),
        scratch_shapes=[pltpu.VMEM((tm, d), BF16)],
        compiler_params=pltpu.CompilerParams(
            dimension_semantics=("arbitrary", "arbitrary"), vmem_limit_bytes=48 * MIB),
    )(x3, mod, mod, w_cat_t)


def _block_masks(n, blk):
    shift = int(math.log2(blk))
    i = lax.broadcasted_iota(jnp.int32, (n, n), 0)
    j = lax.broadcasted_iota(jnp.int32, (n, n), 1)
    same = (i >> shift) == (j >> shift)
    return i, j, same


def _gdn_body(*refs, blk, chained, tiles_per_seq, dk_scale):
    if chained:
        (qkv_ref, z_ref, ba_ref, cw_ref, gp_ref, ng_ref, o_ref, st_ref, xc_sc, act_sc) = refs
    else:
        (qkv_ref, z_ref, ba_ref, buf_ref, sin_ref, cw_ref, gp_ref, ng_ref, o_ref, st_ref, act_sc) = refs
    lt, cqkv = qkv_ref.shape
    hd = LANES
    nh = cqkv // (3 * hd)
    t = pl.program_id(0)
    halo = SUBLANES
    ngrp = lt // SUBLANES

    if chained:
        first = (t % tiles_per_seq) == 0

        @pl.when(first)
        def _():
            xc_sc[pl.ds(0, halo), :] = jnp.zeros((halo, cqkv), F32)
            st_ref[...] = jnp.zeros_like(st_ref)

        @pl.when(jnp.logical_not(first))
        def _():
            xc_sc[pl.ds(0, halo), :] = xc_sc[pl.ds(lt, halo), :]

        xc_sc[pl.ds(halo, lt), :] = qkv_ref[...]
    else:
        assert blk == SUBLANES

    cblk = 2 * LANES
    row_in_grp = lax.broadcasted_iota(jnp.int32, (ngrp, SUBLANES, cblk), 1)
    for c0 in range(0, cqkv, cblk):
        cs = slice(c0, c0 + cblk)
        if chained:
            xe = xc_sc[:, cs].reshape(ngrp + 1, SUBLANES, cblk)
            cur = xe[1:]
        else:
            cur = qkv_ref[:, cs].reshape(ngrp, SUBLANES, cblk)
            prev = buf_ref[:, cs].reshape(ngrp, SUBLANES, cblk)
        y = None
        for j in range(CONV_W):
            d = CONV_W - 1 - j
            if d == 0:
                term = cur
            elif chained:
                rot = pltpu.roll(xe, d, axis=1)
                term = jnp.where(row_in_grp >= d, rot[1:], rot[:-1])
            else:
                term = jnp.where(row_in_grp >= d, pltpu.roll(cur, d, axis=1), pltpu.roll(prev, d, axis=1))
            term = cw_ref[j:j + 1, cs][None] * term
            y = term if y is None else y + term
        act_sc[:, cs] = _silu(y).reshape(lt, cblk)

    st = SEQ_TILE
    nsub = lt // st
    i, j, same = _block_masks(st, blk)
    incl = same & (i >= j)
    strict = same & (i > j)
    hi = lax.Precision.HIGHEST
    incl_f = incl.astype(F32)
    last_f = (j == (i | (blk - 1))).astype(F32)
    eye = (i == j).astype(F32)
    pair_masks = []
    for lvl in range(int(math.log2(blk))):
        pair_masks.append(((i >> (lvl + 1)) == (j >> (lvl + 1))) & ((i >> lvl) != (j >> lvl)))

    heads = range(nh)
    hp = []
    for s in range(nsub):
        rows = slice(s * st, (s + 1) * st)
        ba = ba_ref[rows, :]
        beta_all = jax.nn.sigmoid(ba)
        g_all = -jnp.exp(gp_ref[0:1, :]) * jax.nn.softplus(ba + gp_ref[1:2, :])
        gc_all = _dot(incl_f, g_all, hi)
        gl_all = _dot(last_f, gc_all, hi)
        gc_t = gc_all.T
        for h in heads:
            q = act_sc[rows, h * hd:(h + 1) * hd]
            k = act_sc[rows, (nh + h) * hd:(nh + h + 1) * hd]
            v = act_sc[rows, (2 * nh + h) * hd:(2 * nh + h + 1) * hd]
            q = q * lax.rsqrt(jnp.sum(q * q, axis=-1, keepdims=True) + RMS_EPS) * dk_scale
            k = k * lax.rsqrt(jnp.sum(k * k, axis=-1, keepdims=True) + RMS_EPS)
            beta = beta_all[:, h:h + 1]
            gcc = gc_all[:, nh + h:nh + h + 1]
            gcr = gc_t[nh + h:nh + h + 1, :]
            glc = gl_all[:, nh + h:nh + h + 1]
            decay = jnp.exp(jnp.where(incl, gcc - gcr, -jnp.inf))
            kb = k * beta
            k16 = k.astype(BF16)
            egc = jnp.exp(gcc)
            hp.append(dict(
                sub=s, head=h, decay=decay, k16=k16, kb16=kb.astype(BF16), q16=q.astype(BF16), glc=glc,
                rhs16=jnp.concatenate([v * beta, kb * egc], axis=1).astype(BF16),
                qd=q * egc, kd=k * jnp.exp(glc - gcc)))

    ms = [_dot_nt(p["kb16"], p["k16"]) * jnp.where(strict, p["decay"], 0.0) for p in hp]
    dinvs = [eye - jnp.where(pair_masks[0], m, 0.0) for m in ms]
    for pm in pair_masks[1:]:
        d16s = [d.astype(BF16) for d in dinvs]
        t16s = [_dot(d16, jnp.where(pm, m, 0.0).astype(BF16)).astype(BF16) for d16, m in zip(d16s, ms)]
        dinvs = [d - _dot(t16, d16) for d, t16, d16 in zip(dinvs, t16s, d16s)]
    xs = [_dot(d.astype(BF16), p["rhs16"]) for d, p in zip(dinvs, hp)]
    qk16s = [(_dot_nt(p["q16"], p["k16"]) * p["decay"]).astype(BF16) for p in hp]

    blk_per_sub = st // blk
    vn_parts = [[] for _ in hp]
    os_parts = [[] for _ in hp]
    for b in range(lt // blk):
        lb = b % blk_per_sub
        rs = slice(lb * blk, (lb + 1) * blk)
        cidx = [(b // blk_per_sub) * nh + h for h in heads]
        s0s = [st_ref[0, h] if chained else sin_ref[b, h] for h in heads]
        rr = [_dot(jnp.concatenate([xs[c][rs, hd:], hp[c]["qd"][rs]], axis=0).astype(BF16), s0.astype(BF16))
              for c, s0 in zip(cidx, s0s)]
        vns = [xs[c][rs, :hd] - r[:blk] for c, r in zip(cidx, rr)]
        for h, c in zip(heads, cidx):
            p = hp[c]
            s_new = (s0s[h] * jnp.exp(p["glc"][lb * blk:lb * blk + 1, :])
                     + _dot_tn(p["kd"][rs].astype(BF16), vns[h].astype(BF16)))
            if chained:
                st_ref[0, h] = s_new
            else:
                st_ref[b, h] = s_new
            vn_parts[c].append(vns[h])
            os_parts[c].append(rr[h][blk:])
    for c, p in enumerate(hp):
        rows = slice(p["sub"] * st, (p["sub"] + 1) * st)
        cols = slice(p["head"] * hd, (p["head"] + 1) * hd)
        vn_all = jnp.concatenate(vn_parts[c], axis=0)
        o = jnp.concatenate(os_parts[c], axis=0) + _dot(qk16s[c], vn_all.astype(BF16))
        o = o * lax.rsqrt(jnp.mean(o * o, axis=-1, keepdims=True) + RMS_EPS) * ng_ref[...]
        o = o * _silu(z_ref[rows, cols])
        o_ref[rows, cols] = o.astype(BF16)


def _gdn(proj, conv_w, gate_par, norm_g, *, n_seq, n_rows, blk, ba_col, conv_buf=None, s0=None):
    t_rows = proj.shape[0]
    nh = DN_HEADS
    hd = LANES
    cqkv = 3 * nh * hd
    chained = s0 is None
    lt = GDN_TILE_CHAINED if chained else SEQ_TILE
    assert lt % SEQ_TILE == 0 and SEQ_TILE % blk == 0 and blk & (blk - 1) == 0 and blk >= 2
    n_tiles = t_rows // lt
    common_in = [
        pl.BlockSpec((lt, cqkv), lambda t: (t, 0)),
        pl.BlockSpec((lt, nh * hd), lambda t: (t, 3)),
        pl.BlockSpec((lt, LANES), lambda t: (t, ba_col)),
    ]
    par_in = [
        pl.BlockSpec(conv_w.shape, lambda t: (0, 0)),
        pl.BlockSpec(gate_par.shape, lambda t: (0, 0)),
        pl.BlockSpec((1, hd), lambda t: (0, 0)),
    ]
    o_spec = pl.BlockSpec((lt, nh * hd), lambda t: (t, 0))
    o_shape = jax.ShapeDtypeStruct((t_rows, nh * hd), BF16)
    scratch = []
    if chained:
        assert n_rows % lt == 0 and blk == DN_CHUNK
        tps = n_rows // lt
        in_specs = common_in + par_in
        args = (proj, proj, proj, conv_w, gate_par, norm_g)
        st_spec = pl.BlockSpec((1, nh, hd, hd), lambda t: (t // tps, 0, 0, 0))
        scratch.append(pltpu.VMEM((lt + SUBLANES, cqkv), F32))
    else:
        assert blk == n_rows and conv_buf is not None
        tps = 0
        spt = lt // n_rows
        in_specs = common_in + [
            pl.BlockSpec((lt, cqkv), lambda t: (t, 0)),
            pl.BlockSpec((spt, nh, hd, hd), lambda t: (t, 0, 0, 0)),
        ] + par_in
        args = (proj, proj, proj, conv_buf, s0, conv_w, gate_par, norm_g)
        st_spec = pl.BlockSpec((spt, nh, hd, hd), lambda t: (t, 0, 0, 0))
    scratch.append(pltpu.VMEM((lt, cqkv), F32))
    return pl.pallas_call(
        functools.partial(_gdn_body, blk=blk, chained=chained, tiles_per_seq=tps,
                          dk_scale=float(hd) ** -0.5),
        name="gdn_chained" if chained else "gdn_stateful",
        grid=(n_tiles,),
        in_specs=in_specs,
        out_specs=[o_spec, st_spec],
        out_shape=[o_shape, jax.ShapeDtypeStruct((n_seq, nh, hd, hd), F32)],
        scratch_shapes=scratch,
        compiler_params=pltpu.CompilerParams(
            dimension_semantics=("arbitrary",), vmem_limit_bytes=56 * MIB),
    )(*args)


def _mlp_body(u_ref, v_ref, ws_ref, bst_ref, lng_ref, lnb_ref, o_ref, *maybe_vr_ref, blk):
    ck = ws_ref.shape[1]
    gd = u_ref.shape[1] // MLP_GROUPS
    i, j, same = _block_masks(ck, blk)
    incl = same & (i >= j)
    hi = lax.Precision.HIGHEST
    if blk != ck:
        rep = (j == (i & (blk - 1))).astype(F32)
        bias_all = _dot(rep, bst_ref[...], hi)
    else:
        bias_all = bst_ref[...]
    for g in range(MLP_GROUPS):
        cs = slice(g * gd, (g + 1) * gd)
        wsp = ws_ref[g]
        if blk != ck:
            wsp = _dot_nt(_dot(rep, wsp, hi), rep, hi)
        wsp = jnp.where(incl, wsp, 0.0).astype(BF16)
        for c in range(u_ref.shape[0] // ck):
            rs = slice(c * ck, (c + 1) * ck)
            uu = jax.nn.gelu(u_ref[rs, cs])
            vv = _layer_norm(jax.nn.gelu(v_ref[rs, cs]), lng_ref[:, cs], lnb_ref[:, cs])
            if maybe_vr_ref:
                maybe_vr_ref[0][rs, cs] = vv
            s = _dot(wsp, vv.astype(BF16)) + bias_all[:, g:g + 1]
            o_ref[rs, cs] = (uu * s).astype(BF16)


def _mlp(proj, w_spatial, b_spatial, ln_g, ln_b, *, blk, want_v_rows):
    t_rows = proj.shape[0]
    lt = MLP_TILE
    ck = w_spatial.shape[1]
    width = ln_g.size
    assert ck == MLP_CHUNK and lt % ck == 0 and ck % blk == 0 and t_rows % lt == 0
    u_col = (4 * DN_HEADS * LANES) // width
    bst = jnp.zeros((ck, LANES), F32).at[:, :MLP_GROUPS].set(b_spatial.T)
    o_spec = pl.BlockSpec((lt, width), lambda t: (t, 0))
    out_specs = [o_spec]
    out_shape = [jax.ShapeDtypeStruct((t_rows, width), BF16)]
    if want_v_rows:
        out_specs.append(o_spec)
        out_shape.append(jax.ShapeDtypeStruct((t_rows, width), F32))
    return pl.pallas_call(
        functools.partial(_mlp_body, blk=blk),
        name="spatial_mlp",
        grid=(t_rows // lt,),
        in_specs=[
            pl.BlockSpec((lt, width), lambda t: (t, u_col)),
            pl.BlockSpec((lt, width), lambda t: (t, u_col + 1)),
            pl.BlockSpec(w_spatial.shape, lambda t: (0, 0, 0)),
            pl.BlockSpec((ck, LANES), lambda t: (0, 0)),
            pl.BlockSpec((1, width), lambda t: (0, 0)),
            pl.BlockSpec((1, width), lambda t: (0, 0)),
        ],
        out_specs=out_specs,
        out_shape=out_shape,
        compiler_params=pltpu.CompilerParams(
            dimension_semantics=("arbitrary",), vmem_limit_bytes=32 * MIB),
    )(proj, proj, w_spatial, bst, ln_g.reshape(1, width), ln_b.reshape(1, width))


def _mix_body(od_ref, om_ref, x_ref, gt_ref, w_ref, lng_ref, lnb_ref, o_ref, w_sc, *, seq_tiles, alpha):
    tm, d = o_ref.shape
    kd = od_ref.shape[1]

    @pl.when(pl.program_id(0) == 0)
    def _():
        w_sc[...] = w_ref[...].astype(BF16)

    sb, rb, _ = x_ref.shape
    rc = min(FFN_RC, tm)
    for c in range(tm // rc):
        rs = slice(c * rc, (c + 1) * rc)
        mix = _dot(od_ref[rs, :], w_sc[:kd, :]) + _dot(om_ref[rs, :], w_sc[kd:, :])
        if sb == 1:
            x = x_ref[:, rs, :]
            gate = _mod_rows(gt_ref, seq_tiles)[:, None, :]
        else:
            ss = slice(c * (rc // rb), (c + 1) * (rc // rb))
            x = x_ref[ss]
            gate = gt_ref[ss, :][:, None, :]
        y = alpha * x + gate * mix.reshape(x.shape)
        o_ref[rs, :] = _layer_norm(y.reshape(rc, d), lng_ref[...], lnb_ref[...])


def _mix(o_dn, o_mlp, x3, mod, sub, w_out, layer, ln_g, ln_b, alpha):
    n_seq, n_rows, d = x3.shape
    grp = _Group(n_seq, n_rows, MIX_TM)
    tm = grp.tm
    out = pl.pallas_call(
        functools.partial(_mix_body, seq_tiles=grp.seq_tiles, alpha=alpha),
        name="out_mix_ln",
        grid=(grp.n_tiles,),
        in_specs=[
            pl.BlockSpec((tm, o_dn.shape[1]), lambda m: (m, 0)),
            pl.BlockSpec((tm, o_mlp.shape[1]), lambda m: (m, 0)),
            grp.x_spec(d, 1),
            grp.mod_spec(mod, d, sub * 3 + 2, 1),
            pl.BlockSpec((None,) + w_out.shape[1:], lambda m: (layer, 0, 0), pipeline_mode=pl.Buffered(1)),
            pl.BlockSpec((1, d), lambda m: (0, 0)),
            pl.BlockSpec((1, d), lambda m: (0, 0)),
        ],
        out_specs=pl.BlockSpec((tm, d), lambda m: (m, 0)),
        out_shape=jax.ShapeDtypeStruct((n_seq * n_rows, d), F32),
        scratch_shapes=[pltpu.VMEM(w_out.shape[1:], BF16)],
        compiler_params=pltpu.CompilerParams(
            dimension_semantics=("arbitrary",), vmem_limit_bytes=56 * MIB),
    )(o_dn, o_mlp, x3, mod, w_out, ln_g.reshape(1, d), ln_b.reshape(1, d))
    return out.reshape(n_seq, n_rows, d)


def _trunk_layer(x3, mod, wts, layer, alpha, conv_buf, s0):
    n_seq, n_rows, d = x3.shape
    qkv_dim = 3 * DN_HEADS * LANES
    ffn_w = (wts["wg"], wts["wu"], wts["wd"], layer)
    x3 = _ffn(x3, mod, 0, *ffn_w, 0, wts["ln_g"][0], wts["ln_b"][0], alpha)
    proj = _proj(x3, mod, 1, wts["w_cat"])
    dn_blk = DN_CHUNK if n_rows % DN_CHUNK == 0 else n_rows
    gdn_kw = dict(n_seq=n_seq, n_rows=n_rows, blk=dn_blk, ba_col=wts["ba_col"])
    if s0 is None:
        o_dn, s_new = _gdn(proj, wts["conv_w"], wts["gate_par"], wts["dn_norm_g"], **gdn_kw)
    else:
        pad = jnp.zeros((n_seq, SUBLANES - (CONV_W - 1), qkv_dim), F32)
        buf = jnp.concatenate([pad, conv_buf], axis=1).reshape(n_seq * SUBLANES, qkv_dim)
        assert n_rows == SUBLANES
        o_dn, s_new = _gdn(proj, wts["conv_w"], wts["gate_par"], wts["dn_norm_g"],
                           conv_buf=buf, s0=s0, **gdn_kw)
    mlp_out = _mlp(proj, wts["w_spatial"], wts["b_spatial"], wts["mlp_ln_g"], wts["mlp_ln_b"],
                   blk=min(n_rows, MLP_CHUNK), want_v_rows=s0 is not None)
    o_mlp = mlp_out[0]
    v_rows = mlp_out[1].reshape(n_seq, n_rows, -1) if s0 is not None else None
    x3 = _mix(o_dn, o_mlp, x3, mod, 1, wts["w_out"], layer, wts["ln_g"][1], wts["ln_b"][1], alpha)
    x3 = _ffn(x3, mod, 2, *ffn_w, 1, wts["ln_g"][2], wts["ln_b"][2], alpha)
    new_buf = proj.reshape(n_seq, n_rows, -1)[:, n_rows - (CONV_W - 1):, :qkv_dim]
    return x3, new_buf, s_new, v_rows


def kernel(x_prompt, x_sample, c_prompt, c_sample, state_delta, state_conv, w_ada, b_ada, ln_g, ln_b,
           ffn_wg, ffn_wu, ffn_wd, w_in, conv_w, a_log, dt_bias, dn_norm_g, mlp_ln_g, mlp_ln_b,
           w_spatial, b_spatial, w_out):
    depth = w_ada.shape[0]
    bp, _, d = x_prompt.shape
    bs = x_sample.shape[0]
    nh = DN_HEADS
    alpha = (2.0 * depth) ** 0.25
    qkvz = 4 * nh * LANES
    gates = 2 * nh

    pad_rows = (-(bs + bp)) % SUBLANES
    c_all = jnp.concatenate([c_sample, c_prompt, jnp.zeros((pad_rows, d), F32)], axis=0)
    w_in_t = jnp.swapaxes(w_in, 1, 2)

    y_p, y_s = x_prompt, x_sample
    delta_p, conv_p, delta_s, conv_s, vrows_s = [], [], [], [], []
    for layer in range(depth):
        mod_s, mod_p = _ada(c_all, w_ada[layer], b_ada[layer], bs)
        gate_par = jnp.zeros((SUBLANES, LANES), F32)
        gate_par = gate_par.at[0, nh:gates].set(a_log[layer]).at[1, nh:gates].set(dt_bias[layer])
        wts = dict(
            wg=ffn_wg, wu=ffn_wu, wd=ffn_wd, ln_g=ln_g[layer], ln_b=ln_b[layer],
            w_cat=_wprep(w_in_t, layer, qkvz, gates), ba_col=(w_in.shape[-1] - gates) // LANES,
            conv_w=conv_w[layer], gate_par=gate_par,
            dn_norm_g=dn_norm_g[layer].reshape(1, LANES), mlp_ln_g=mlp_ln_g[layer], mlp_ln_b=mlp_ln_b[layer],
            w_spatial=w_spatial[layer], b_spatial=b_spatial[layer], w_out=w_out,
        )
        y_p, cb_p, ds_p, _ = _trunk_layer(y_p, mod_p, wts, layer, alpha, None, None)
        y_s, cb_s, ds_s, vr_s = _trunk_layer(y_s, mod_s, wts, layer, alpha, state_conv[layer],
                                             state_delta[layer])
        delta_p.append(ds_p)
        conv_p.append(cb_p)
        delta_s.append(ds_s)
        conv_s.append(cb_s)
        vrows_s.append(vr_s)
    return (y_p, y_s, jnp.stack(delta_p), jnp.stack(conv_p), jnp.stack(delta_s), jnp.stack(conv_s),
            jnp.stack(vrows_s))
```

```python
import functools
import math

import jax
import jax.numpy as jnp
from jax import lax
from jax.experimental import pallas as pl
from jax.experimental.pallas import tpu as pltpu

F32 = jnp.float32
BF16 = jnp.bfloat16

DN_HEADS = 8
DN_CHUNK = 64
CONV_W = 4
MLP_GROUPS = 4
MLP_CHUNK = 128
N_SUB = 3
LN_EPS = 1e-5
RMS_EPS = 1e-6

LANES = 128
SUBLANES = 8
MIB = 2 ** 20

FFN_TM = 1024
FFN_TF = 256
FFN_RC = 256
FFN_VMEM_LIMIT = 58 * MIB
PROJ_TN = 1280
PROJ_N = 6400
PREP_TC = 256
MIX_TM = 1024
ADA_TN = 1024
SEQ_TILE = 128
GDN_TILE_CHAINED = 256
MLP_TILE = 512


def _dot(a, b, precision=None):
    return jnp.dot(a, b, preferred_element_type=F32, precision=precision)


def _dot_nt(a, b, precision=None):
    return lax.dot_general(a, b, (((1,), (1,)), ((), ())),
                           preferred_element_type=F32, precision=precision)


def _dot_tn(a, b):
    return lax.dot_general(a, b, (((0,), (0,)), ((), ())), preferred_element_type=F32)


def _layer_norm(y, g, b):
    mu = jnp.mean(y, axis=-1, keepdims=True)
    yc = y - mu
    var = jnp.mean(yc * yc, axis=-1, keepdims=True)
    return yc * lax.rsqrt(var + LN_EPS) * g + b


def _silu(x):
    return x * jax.nn.sigmoid(x)


def _mod_rows(ref, seq_tiles):
    if seq_tiles:
        return ref[pl.ds(pl.program_id(0) // seq_tiles, 1), :]
    return ref[...]


def _ada_body(c_ref, w_ref, b_ref, ms_ref, mp_ref):
    c = c_ref[...]
    a = _silu(c).astype(BF16)
    y = _dot(a, w_ref[...].astype(BF16)) + b_ref[...]
    ns = ms_ref.shape[0]
    ms_ref[...] = y[:ns]
    mp_ref[...] = y[ns:]


def _ada(c_all, w_ada, b_ada, n_sample):
    rows, d = c_all.shape
    n = w_ada.shape[1]
    return pl.pallas_call(
        _ada_body,
        name="ada_mod",
        grid=(n // ADA_TN,),
        in_specs=[
            pl.BlockSpec((rows, d), lambda j: (0, 0)),
            pl.BlockSpec((d, ADA_TN), lambda j: (0, j)),
            pl.BlockSpec((1, ADA_TN), lambda j: (0, j)),
        ],
        out_specs=[
            pl.BlockSpec((n_sample, ADA_TN), lambda j: (0, j)),
            pl.BlockSpec((rows - n_sample, ADA_TN), lambda j: (0, j)),
        ],
        out_shape=[
            jax.ShapeDtypeStruct((n_sample, n), F32),
            jax.ShapeDtypeStruct((rows - n_sample, n), F32),
        ],
        compiler_params=pltpu.CompilerParams(
            dimension_semantics=("arbitrary",), vmem_limit_bytes=40 * MIB),
    )(c_all, w_ada, b_ada.reshape(1, n))


class _Group:
    def __init__(self, n_seq, n_rows, tm):
        if n_rows >= tm:
            assert n_rows % tm == 0
            self.sb, self.rb = 1, tm
        else:
            assert tm % n_rows == 0 and n_seq % (tm // n_rows) == 0
            self.sb, self.rb = tm // n_rows, n_rows
        self.n_seq, self.n_rows = n_seq, n_rows
        self.tiles_r = n_rows // self.rb
        self.tm = self.sb * self.rb
        self.n_tiles = (n_seq // self.sb) * self.tiles_r
        self.seq_tiles = self.tiles_r if self.sb == 1 else 0

    def x_spec(self, d, ngrid, **kw):
        tr = self.tiles_r
        if ngrid == 2:
            return pl.BlockSpec((self.sb, self.rb, d), lambda m, f: (m // tr, m % tr, 0), **kw)
        return pl.BlockSpec((self.sb, self.rb, d), lambda m: (m // tr, m % tr, 0), **kw)

    def mod_spec(self, mod, d, col, ngrid):
        if self.sb == 1:
            rows, row_blk = mod.shape[0], (lambda m: 0)
        else:
            rows, row_blk = self.sb, (lambda m: m)
        if ngrid == 2:
            return pl.BlockSpec((rows, d), lambda m, f: (row_blk(m), col))
        return pl.BlockSpec((rows, d), lambda m: (row_blk(m), col))


def _ffn_body(x_ref, sh_ref, sc_ref, gt_ref, wg_ref, wu_ref, wd_ref, lng_ref, lnb_ref,
              o_ref, h_sc, *, seq_tiles, alpha):
    f = pl.program_id(1)
    nf = pl.num_programs(1)
    tm, d = o_ref.shape
    sb, rb, _ = x_ref.shape
    rc = min(FFN_RC, tm)
    n_chunks = tm // rc

    def x_chunk(c):
        if sb == 1:
            return x_ref[:, c * rc:(c + 1) * rc, :]
        return x_ref[c * (rc // rb):(c + 1) * (rc // rb)]

    def mod_chunk(ref, c):
        if seq_tiles:
            return _mod_rows(ref, seq_tiles)
        return ref[c * (rc // rb):(c + 1) * (rc // rb), :]

    def weights():
        return wg_ref[...].astype(BF16), wu_ref[...].astype(BF16), wd_ref[...].astype(BF16)

    def gate_up(h16, wg16, wu16):
        return (_silu(_dot(h16, wg16)) * _dot(h16, wu16)).astype(BF16)

    @pl.when(f == 0)
    def _():
        wg16, wu16, wd16 = weights()
        for c in range(n_chunks):
            rs = slice(c * rc, (c + 1) * rc)
            x = x_chunk(c)
            h = x * (1.0 + mod_chunk(sc_ref, c)[:, None, :]) + mod_chunk(sh_ref, c)[:, None, :]
            h16 = h.reshape(rc, d).astype(BF16)
            h_sc[rs, :] = h16
            o_ref[rs, :] = _dot(gate_up(h16, wg16, wu16), wd16)

    @pl.when(jnp.logical_and(f > 0, f < nf - 1))
    def _():
        wg16, wu16, wd16 = weights()
        o_ref[...] += _dot(gate_up(h_sc[...], wg16, wu16), wd16)

    @pl.when(f == nf - 1)
    def _():
        wg16, wu16, wd16 = weights()
        for c in range(n_chunks):
            rs = slice(c * rc, (c + 1) * rc)
            acc = o_ref[rs, :] + _dot(gate_up(h_sc[rs, :], wg16, wu16), wd16)
            x = x_chunk(c)
            gate = mod_chunk(gt_ref, c)[:, None, :]
            y = alpha * x + (0.5 * gate) * acc.reshape(x.shape)
            o_ref[rs, :] = _layer_norm(y.reshape(rc, d), lng_ref[...], lnb_ref[...])


def _ffn(x3, mod, sub, wg, wu, wd, layer, slot, ln_g, ln_b, alpha):
    n_seq, n_rows, d = x3.shape
    dff = wg.shape[-1]
    grp = _Group(n_seq, n_rows, FFN_TM)
    tm = grp.tm
    rc = min(FFN_RC, tm)
    assert dff // FFN_TF >= 2 and tm % rc == 0 and (grp.rb % rc == 0 if grp.sb == 1 else rc % grp.rb == 0)
    out = pl.pallas_call(
        functools.partial(_ffn_body, seq_tiles=grp.seq_tiles, alpha=alpha),
        name="swiglu_ln",
        grid=(grp.n_tiles, dff // FFN_TF),
        in_specs=[
            grp.x_spec(d, 2),
            grp.mod_spec(mod, d, sub * 3 + 0, 2),
            grp.mod_spec(mod, d, sub * 3 + 1, 2),
            grp.mod_spec(mod, d, sub * 3 + 2, 2),
            pl.BlockSpec((None, None, d, FFN_TF), lambda m, f: (layer, slot, 0, f)),
            pl.BlockSpec((None, None, d, FFN_TF), lambda m, f: (layer, slot, 0, f)),
            pl.BlockSpec((None, None, FFN_TF, d), lambda m, f: (layer, slot, f, 0)),
            pl.BlockSpec((1, d), lambda m, f: (0, 0)),
            pl.BlockSpec((1, d), lambda m, f: (0, 0)),
        ],
        out_specs=pl.BlockSpec((tm, d), lambda m, f: (m, 0)),
        out_shape=jax.ShapeDtypeStruct((n_seq * n_rows, d), F32),
        scratch_shapes=[pltpu.VMEM((tm, d), BF16)],
        compiler_params=pltpu.CompilerParams(
            dimension_semantics=("arbitrary", "arbitrary"), vmem_limit_bytes=FFN_VMEM_LIMIT),
    )(x3, mod, mod, mod, wg, wu, wd, ln_g.reshape(1, d), ln_b.reshape(1, d))
    return out.reshape(n_seq, n_rows, d)


def _wprep_body(w_ref, wo_ref, o_ref, oo_ref, *, qkvz, gates):
    n_in, cols = w_ref.shape
    n_out = o_ref.shape[0]
    mlp = n_in - qkvz - gates
    o_ref[0:qkvz, :] = w_ref[0:qkvz, :].astype(BF16)
    o_ref[qkvz:qkvz + mlp, :] = w_ref[qkvz + gates:n_in, :].astype(BF16)
    o_ref[qkvz + mlp:n_in, :] = w_ref[qkvz:qkvz + gates, :].astype(BF16)
    o_ref[n_in:n_out, :] = jnp.zeros((n_out - n_in, cols), BF16)
    oo_ref[...] = wo_ref[...].astype(BF16)


def _wprep(w_in_t, w_out, layer, qkvz, gates):
    _, n_in, d = w_in_t.shape
    _, d_mix, d_o = w_out.shape
    assert n_in <= PROJ_N and (n_in - gates) % LANES == 0 and d_o == d
    return pl.pallas_call(
        functools.partial(_wprep_body, qkvz=qkvz, gates=gates),
        name="proj_weight_prep",
        grid=(d // PREP_TC,),
        in_specs=[
            pl.BlockSpec((None, n_in, PREP_TC), lambda i: (layer, 0, i)),
            pl.BlockSpec((None, d_mix, PREP_TC), lambda i: (layer, 0, i)),
        ],
        out_specs=[
            pl.BlockSpec((PROJ_N, PREP_TC), lambda i: (0, i)),
            pl.BlockSpec((d_mix, PREP_TC), lambda i: (0, i)),
        ],
        out_shape=[jax.ShapeDtypeStruct((PROJ_N, d), BF16), jax.ShapeDtypeStruct((d_mix, d), BF16)],
        compiler_params=pltpu.CompilerParams(
            dimension_semantics=("arbitrary",), vmem_limit_bytes=40 * MIB),
    )(w_in_t, w_out)


def _proj_body(x_ref, sh_ref, sc_ref, w_ref, o_ref, h_sc, *, seq_tiles):
    tm, d = h_sc.shape

    @pl.when(pl.program_id(1) == 0)
    def _():
        x = x_ref[...]
        h = x * (1.0 + _mod_rows(sc_ref, seq_tiles)[:, None, :]) + _mod_rows(sh_ref, seq_tiles)[:, None, :]
        h_sc[...] = h.reshape(tm, d).astype(BF16)

    o_ref[...] = _dot_nt(h_sc[...], w_ref[...])


def _proj(x3, mod, sub, w_cat_t):
    n_seq, n_rows, d = x3.shape
    n = w_cat_t.shape[0]
    grp = _Group(n_seq, n_rows, FFN_TM)
    tm = grp.tm
    return pl.pallas_call(
        functools.partial(_proj_body, seq_tiles=grp.seq_tiles),
        name="in_proj",
        grid=(grp.n_tiles, n // PROJ_TN),
        in_specs=[
            grp.x_spec(d, 2),
            grp.mod_spec(mod, d, sub * 3 + 0, 2),
            grp.mod_spec(mod, d, sub * 3 + 1, 2),
            pl.BlockSpec((PROJ_TN, d), lambda m, j: (j, 0)),
        ],
        out_specs=pl.BlockSpec((tm, PROJ_TN), lambda m, j: (m, j)),
        out_shape=jax.ShapeDtypeStruct((n_seq * n_rows, n), F32),
        scratch_shapes=[pltpu.VMEM((tm, d), BF16)],
        compiler_params=pltpu.CompilerParams(
            dimension_semantics=("arbitrary", "arbitrary"), vmem_limit_bytes=48 * MIB),
    )(x3, mod, mod, w_cat_t)


def _block_masks(n, blk):
    shift = int(math.log2(blk))
    i = lax.broadcasted_iota(jnp.int32, (n, n), 0)
    j = lax.broadcasted_iota(jnp.int32, (n, n), 1)
    same = (i >> shift) == (j >> shift)
    return i, j, same


def _gdn_body(*refs, blk, chained, tiles_per_seq, dk_scale):
    if chained:
        (qkv_ref, z_ref, ba_ref, cw_ref, gp_ref, ng_ref, o_ref, st_ref, xc_sc, act_sc) = refs
    else:
        (qkv_ref, z_ref, ba_ref, buf_ref, sin_ref, cw_ref, gp_ref, ng_ref, o_ref, st_ref, act_sc) = refs
    lt, cqkv = qkv_ref.shape
    hd = LANES
    nh = cqkv // (3 * hd)
    t = pl.program_id(0)
    halo = SUBLANES
    ngrp = lt // SUBLANES

    if chained:
        first = (t % tiles_per_seq) == 0

        @pl.when(first)
        def _():
            xc_sc[pl.ds(0, halo), :] = jnp.zeros((halo, cqkv), F32)
            st_ref[...] = jnp.zeros_like(st_ref)

        @pl.when(jnp.logical_not(first))
        def _():
            xc_sc[pl.ds(0, halo), :] = xc_sc[pl.ds(lt, halo), :]

        xc_sc[pl.ds(halo, lt), :] = qkv_ref[...]
    else:
        assert blk == SUBLANES

    cblk = 2 * LANES
    row_in_grp = lax.broadcasted_iota(jnp.int32, (ngrp, SUBLANES, cblk), 1)
    for c0 in range(0, cqkv, cblk):
        cs = slice(c0, c0 + cblk)
        if chained:
            xe = xc_sc[:, cs].reshape(ngrp + 1, SUBLANES, cblk)
            cur = xe[1:]
        else:
            cur = qkv_ref[:, cs].reshape(ngrp, SUBLANES, cblk)
            prev = buf_ref[:, cs].reshape(ngrp, SUBLANES, cblk)
        y = None
        for j in range(CONV_W):
            d = CONV_W - 1 - j
            if d == 0:
                term = cur
            elif chained:
                rot = pltpu.roll(xe, d, axis=1)
                term = jnp.where(row_in_grp >= d, rot[1:], rot[:-1])
            else:
                term = jnp.where(row_in_grp >= d, pltpu.roll(cur, d, axis=1), pltpu.roll(prev, d, axis=1))
            term = cw_ref[j:j + 1, cs][None] * term
            y = term if y is None else y + term
        act_sc[:, cs] = _silu(y).reshape(lt, cblk)

    st = SEQ_TILE
    nsub = lt // st
    i, j, same = _block_masks(st, blk)
    incl = same & (i >= j)
    strict = same & (i > j)
    hi = lax.Precision.HIGHEST
    incl_f = incl.astype(F32)
    last_f = (j == (i | (blk - 1))).astype(F32)
    eye = (i == j).astype(F32)
    pair_masks = []
    for lvl in range(int(math.log2(blk))):
        pair_masks.append(((i >> (lvl + 1)) == (j >> (lvl + 1))) & ((i >> lvl) != (j >> lvl)))

    heads = range(nh)
    hp = []
    for s in range(nsub):
        rows = slice(s * st, (s + 1) * st)
        ba = ba_ref[rows, :]
        beta_all = jax.nn.sigmoid(ba)
        g_all = -jnp.exp(gp_ref[0:1, :]) * jax.nn.softplus(ba + gp_ref[1:2, :])
        gc_all = _dot(incl_f, g_all, hi)
        gl_all = _dot(last_f, gc_all, hi)
        gc_t = gc_all.T
        for h in heads:
            q = act_sc[rows, h * hd:(h + 1) * hd]
            k = act_sc[rows, (nh + h) * hd:(nh + h + 1) * hd]
            v = act_sc[rows, (2 * nh + h) * hd:(2 * nh + h + 1) * hd]
            q = q * lax.rsqrt(jnp.sum(q * q, axis=-1, keepdims=True) + RMS_EPS) * dk_scale
            k = k * lax.rsqrt(jnp.sum(k * k, axis=-1, keepdims=True) + RMS_EPS)
            beta = beta_all[:, h:h + 1]
            gcc = gc_all[:, nh + h:nh + h + 1]
            gcr = gc_t[nh + h:nh + h + 1, :]
            glc = gl_all[:, nh + h:nh + h + 1]
            decay = jnp.exp(jnp.where(incl, gcc - gcr, -jnp.inf))
            kb = k * beta
            k16 = k.astype(BF16)
            egc = jnp.exp(gcc)
            hp.append(dict(
                sub=s, head=h, decay=decay, k16=k16, kb16=kb.astype(BF16), q16=q.astype(BF16), glc=glc,
                rhs16=jnp.concatenate([v * beta, kb * egc], axis=1).astype(BF16),
                qd=q * egc, kd=k * jnp.exp(glc - gcc)))

    ms = [_dot_nt(p["kb16"], p["k16"]) * jnp.where(strict, p["decay"], 0.0) for p in hp]
    dinvs = [eye - jnp.where(pair_masks[0], m, 0.0) for m in ms]
    for pm in pair_masks[1:]:
        d16s = [d.astype(BF16) for d in dinvs]
        t16s = [_dot(d16, jnp.where(pm, m, 0.0).astype(BF16)).astype(BF16) for d16, m in zip(d16s, ms)]
        dinvs = [d - _dot(t16, d16) for d, t16, d16 in zip(dinvs, t16s, d16s)]
    xs = [_dot(d.astype(BF16), p["rhs16"]) for d, p in zip(dinvs, hp)]
    qk16s = [(_dot_nt(p["q16"], p["k16"]) * p["decay"]).astype(BF16) for p in hp]

    blk_per_sub = st // blk
    vn_parts = [[] for _ in hp]
    os_parts = [[] for _ in hp]
    for b in range(lt // blk):
        lb = b % blk_per_sub
        rs = slice(lb * blk, (lb + 1) * blk)
        cidx = [(b // blk_per_sub) * nh + h for h in heads]
        s0s = [st_ref[0, h] if chained else sin_ref[b, h] for h in heads]
        rr = [_dot(jnp.concatenate([xs[c][rs, hd:], hp[c]["qd"][rs]], axis=0).astype(BF16), s0.astype(BF16))
              for c, s0 in zip(cidx, s0s)]
        vns = [xs[c][rs, :hd] - r[:blk] for c, r in zip(cidx, rr)]
        for h, c in zip(heads, cidx):
            p = hp[c]
            s_new = (s0s[h] * jnp.exp(p["glc"][lb * blk:lb * blk + 1, :])
                     + _dot_tn(p["kd"][rs].astype(BF16), vns[h].astype(BF16)))
            if chained:
                st_ref[0, h] = s_new
            else:
                st_ref[b, h] = s_new
            vn_parts[c].append(vns[h])
            os_parts[c].append(rr[h][blk:])
    for c, p in enumerate(hp):
        rows = slice(p["sub"] * st, (p["sub"] + 1) * st)
        cols = slice(p["head"] * hd, (p["head"] + 1) * hd)
        vn_all = jnp.concatenate(vn_parts[c], axis=0)
        o = jnp.concatenate(os_parts[c], axis=0) + _dot(qk16s[c], vn_all.astype(BF16))
        o = o * lax.rsqrt(jnp.mean(o * o, axis=-1, keepdims=True) + RMS_EPS) * ng_ref[...]
        o = o * _silu(z_ref[rows, cols])
        o_ref[rows, cols] = o.astype(BF16)


def _gdn(proj, conv_w, gate_par, norm_g, *, n_seq, n_rows, blk, ba_col, conv_buf=None, s0=None):
    t_rows = proj.shape[0]
    nh = DN_HEADS
    hd = LANES
    cqkv = 3 * nh * hd
    chained = s0 is None
    lt = GDN_TILE_CHAINED if chained else SEQ_TILE
    assert lt % SEQ_TILE == 0 and SEQ_TILE % blk == 0 and blk & (blk - 1) == 0 and blk >= 2
    n_tiles = t_rows // lt
    common_in = [
        pl.BlockSpec((lt, cqkv), lambda t: (t, 0)),
        pl.BlockSpec((lt, nh * hd), lambda t: (t, 3)),
        pl.BlockSpec((lt, LANES), lambda t: (t, ba_col)),
    ]
    par_in = [
        pl.BlockSpec(conv_w.shape, lambda t: (0, 0)),
        pl.BlockSpec(gate_par.shape, lambda t: (0, 0)),
        pl.BlockSpec((1, hd), lambda t: (0, 0)),
    ]
    o_spec = pl.BlockSpec((lt, nh * hd), lambda t: (t, 0))
    o_shape = jax.ShapeDtypeStruct((t_rows, nh * hd), BF16)
    scratch = []
    if chained:
        assert n_rows % lt == 0 and blk == DN_CHUNK
        tps = n_rows // lt
        in_specs = common_in + par_in
        args = (proj, proj, proj, conv_w, gate_par, norm_g)
        st_spec = pl.BlockSpec((1, nh, hd, hd), lambda t: (t // tps, 0, 0, 0))
        scratch.append(pltpu.VMEM((lt + SUBLANES, cqkv), F32))
    else:
        assert blk == n_rows and conv_buf is not None
        tps = 0
        spt = lt // n_rows
        in_specs = common_in + [
            pl.BlockSpec((lt, cqkv), lambda t: (t, 0)),
            pl.BlockSpec((spt, nh, hd, hd), lambda t: (t, 0, 0, 0)),
        ] + par_in
        args = (proj, proj, proj, conv_buf, s0, conv_w, gate_par, norm_g)
        st_spec = pl.BlockSpec((spt, nh, hd, hd), lambda t: (t, 0, 0, 0))
    scratch.append(pltpu.VMEM((lt, cqkv), F32))
    return pl.pallas_call(
        functools.partial(_gdn_body, blk=blk, chained=chained, tiles_per_seq=tps,
                          dk_scale=float(hd) ** -0.5),
        name="gdn_chained" if chained else "gdn_stateful",
        grid=(n_tiles,),
        in_specs=in_specs,
        out_specs=[o_spec, st_spec],
        out_shape=[o_shape, jax.ShapeDtypeStruct((n_seq, nh, hd, hd), F32)],
        scratch_shapes=scratch,
        compiler_params=pltpu.CompilerParams(
            dimension_semantics=("arbitrary",), vmem_limit_bytes=56 * MIB),
    )(*args)


def _mlp_body(u_ref, v_ref, ws_ref, bst_ref, lng_ref, lnb_ref, o_ref, *maybe_vr_ref, blk):
    ck = ws_ref.shape[1]
    gd = u_ref.shape[1] // MLP_GROUPS
    i, j, same = _block_masks(ck, blk)
    incl = same & (i >= j)
    hi = lax.Precision.HIGHEST
    if blk != ck:
        rep = (j == (i & (blk - 1))).astype(F32)
        bias_all = _dot(rep, bst_ref[...], hi)
    else:
        bias_all = bst_ref[...]
    for g in range(MLP_GROUPS):
        cs = slice(g * gd, (g + 1) * gd)
        wsp = ws_ref[g]
        if blk != ck:
            wsp = _dot_nt(_dot(rep, wsp, hi), rep, hi)
        wsp = jnp.where(incl, wsp, 0.0).astype(BF16)
        for c in range(u_ref.shape[0] // ck):
            rs = slice(c * ck, (c + 1) * ck)
            uu = jax.nn.gelu(u_ref[rs, cs])
            vv = _layer_norm(jax.nn.gelu(v_ref[rs, cs]), lng_ref[:, cs], lnb_ref[:, cs])
            if maybe_vr_ref:
                maybe_vr_ref[0][rs, cs] = vv
            s = _dot(wsp, vv.astype(BF16)) + bias_all[:, g:g + 1]
            o_ref[rs, cs] = (uu * s).astype(BF16)


def _mlp(proj, w_spatial, b_spatial, ln_g, ln_b, *, blk, want_v_rows):
    t_rows = proj.shape[0]
    lt = MLP_TILE
    ck = w_spatial.shape[1]
    width = ln_g.size
    assert ck == MLP_CHUNK and lt % ck == 0 and ck % blk == 0 and t_rows % lt == 0
    u_col = (4 * DN_HEADS * LANES) // width
    bst = jnp.zeros((ck, LANES), F32).at[:, :MLP_GROUPS].set(b_spatial.T)
    o_spec = pl.BlockSpec((lt, width), lambda t: (t, 0))
    out_specs = [o_spec]
    out_shape = [jax.ShapeDtypeStruct((t_rows, width), BF16)]
    if want_v_rows:
        out_specs.append(o_spec)
        out_shape.append(jax.ShapeDtypeStruct((t_rows, width), F32))
    return pl.pallas_call(
        functools.partial(_mlp_body, blk=blk),
        name="spatial_mlp",
        grid=(t_rows // lt,),
        in_specs=[
            pl.BlockSpec((lt, width), lambda t: (t, u_col)),
            pl.BlockSpec((lt, width), lambda t: (t, u_col + 1)),
            pl.BlockSpec(w_spatial.shape, lambda t: (0, 0, 0)),
            pl.BlockSpec((ck, LANES), lambda t: (0, 0)),
            pl.BlockSpec((1, width), lambda t: (0, 0)),
            pl.BlockSpec((1, width), lambda t: (0, 0)),
        ],
        out_specs=out_specs,
        out_shape=out_shape,
        compiler_params=pltpu.CompilerParams(
            dimension_semantics=("arbitrary",), vmem_limit_bytes=32 * MIB),
    )(proj, proj, w_spatial, bst, ln_g.reshape(1, width), ln_b.reshape(1, width))


def _mix_body(od_ref, om_ref, x_ref, gt_ref, w_ref, lng_ref, lnb_ref, o_ref, *, seq_tiles, alpha):
    tm, d = o_ref.shape
    kd = od_ref.shape[1]

    sb, rb, _ = x_ref.shape
    rc = min(FFN_RC, tm)
    for c in range(tm // rc):
        rs = slice(c * rc, (c + 1) * rc)
        mix = _dot(od_ref[rs, :], w_ref[:kd, :]) + _dot(om_ref[rs, :], w_ref[kd:, :])
        if sb == 1:
            x = x_ref[:, rs, :]
            gate = _mod_rows(gt_ref, seq_tiles)[:, None, :]
        else:
            ss = slice(c * (rc // rb), (c + 1) * (rc // rb))
            x = x_ref[ss]
            gate = gt_ref[ss, :][:, None, :]
        y = alpha * x + gate * mix.reshape(x.shape)
        o_ref[rs, :] = _layer_norm(y.reshape(rc, d), lng_ref[...], lnb_ref[...])


def _mix(o_dn, o_mlp, x3, mod, sub, w_out16, ln_g, ln_b, alpha):
    n_seq, n_rows, d = x3.shape
    grp = _Group(n_seq, n_rows, MIX_TM)
    tm = grp.tm
    out = pl.pallas_call(
        functools.partial(_mix_body, seq_tiles=grp.seq_tiles, alpha=alpha),
        name="out_mix_ln",
        grid=(grp.n_tiles,),
        in_specs=[
            pl.BlockSpec((tm, o_dn.shape[1]), lambda m: (m, 0)),
            pl.BlockSpec((tm, o_mlp.shape[1]), lambda m: (m, 0)),
            grp.x_spec(d, 1),
            grp.mod_spec(mod, d, sub * 3 + 2, 1),
            pl.BlockSpec(w_out16.shape, lambda m: (0, 0), pipeline_mode=pl.Buffered(1)),
            pl.BlockSpec((1, d), lambda m: (0, 0)),
            pl.BlockSpec((1, d), lambda m: (0, 0)),
        ],
        out_specs=pl.BlockSpec((tm, d), lambda m: (m, 0)),
        out_shape=jax.ShapeDtypeStruct((n_seq * n_rows, d), F32),
        compiler_params=pltpu.CompilerParams(
            dimension_semantics=("arbitrary",), vmem_limit_bytes=56 * MIB),
    )(o_dn, o_mlp, x3, mod, w_out16, ln_g.reshape(1, d), ln_b.reshape(1, d))
    return out.reshape(n_seq, n_rows, d)


def _trunk_layer(x3, mod, wts, layer, alpha, conv_buf, s0):
    n_seq, n_rows, d = x3.shape
    qkv_dim = 3 * DN_HEADS * LANES
    ffn_w = (wts["wg"], wts["wu"], wts["wd"], layer)
    x3 = _ffn(x3, mod, 0, *ffn_w, 0, wts["ln_g"][0], wts["ln_b"][0], alpha)
    proj = _proj(x3, mod, 1, wts["w_cat"])
    dn_blk = DN_CHUNK if n_rows % DN_CHUNK == 0 else n_rows
    gdn_kw = dict(n_seq=n_seq, n_rows=n_rows, blk=dn_blk, ba_col=wts["ba_col"])
    if s0 is None:
        o_dn, s_new = _gdn(proj, wts["conv_w"], wts["gate_par"], wts["dn_norm_g"], **gdn_kw)
    else:
        pad = jnp.zeros((n_seq, SUBLANES - (CONV_W - 1), qkv_dim), F32)
        buf = jnp.concatenate([pad, conv_buf], axis=1).reshape(n_seq * SUBLANES, qkv_dim)
        assert n_rows == SUBLANES
        o_dn, s_new = _gdn(proj, wts["conv_w"], wts["gate_par"], wts["dn_norm_g"],
                           conv_buf=buf, s0=s0, **gdn_kw)
    mlp_out = _mlp(proj, wts["w_spatial"], wts["b_spatial"], wts["mlp_ln_g"], wts["mlp_ln_b"],
                   blk=min(n_rows, MLP_CHUNK), want_v_rows=s0 is not None)
    o_mlp = mlp_out[0]
    v_rows = mlp_out[1].reshape(n_seq, n_rows, -1) if s0 is not None else None
    x3 = _mix(o_dn, o_mlp, x3, mod, 1, wts["w_out"], wts["ln_g"][1], wts["ln_b"][1], alpha)
    x3 = _ffn(x3, mod, 2, *ffn_w, 1, wts["ln_g"][2], wts["ln_b"][2], alpha)
    new_buf = proj.reshape(n_seq, n_rows, -1)[:, n_rows - (CONV_W - 1):, :qkv_dim]
    return x3, new_buf, s_new, v_rows


def kernel(x_prompt, x_sample, c_prompt, c_sample, state_delta, state_conv, w_ada, b_ada, ln_g, ln_b,
           ffn_wg, ffn_wu, ffn_wd, w_in, conv_w, a_log, dt_bias, dn_norm_g, mlp_ln_g, mlp_ln_b,
           w_spatial, b_spatial, w_out):
    depth = w_ada.shape[0]
    bp, _, d = x_prompt.shape
    bs = x_sample.shape[0]
    nh = DN_HEADS
    alpha = (2.0 * depth) ** 0.25
    qkvz = 4 * nh * LANES
    gates = 2 * nh

    pad_rows = (-(bs + bp)) % SUBLANES
    c_all = jnp.concatenate([c_sample, c_prompt, jnp.zeros((pad_rows, d), F32)], axis=0)
    w_in_t = jnp.swapaxes(w_in, 1, 2)

    y_p, y_s = x_prompt, x_sample
    delta_p, conv_p, delta_s, conv_s, vrows_s = [], [], [], [], []
    for layer in range(depth):
        mod_s, mod_p = _ada(c_all, w_ada[layer], b_ada[layer], bs)
        gate_par = jnp.zeros((SUBLANES, LANES), F32)
        gate_par = gate_par.at[0, nh:gates].set(a_log[layer]).at[1, nh:gates].set(dt_bias[layer])
        w_cat, w_out16 = _wprep(w_in_t, w_out, layer, qkvz, gates)
        wts = dict(
            wg=ffn_wg, wu=ffn_wu, wd=ffn_wd, ln_g=ln_g[layer], ln_b=ln_b[layer],
            w_cat=w_cat, ba_col=(w_in.shape[-1] - gates) // LANES,
            conv_w=conv_w[layer], gate_par=gate_par,
            dn_norm_g=dn_norm_g[layer].reshape(1, LANES), mlp_ln_g=mlp_ln_g[layer], mlp_ln_b=mlp_ln_b[layer],
            w_spatial=w_spatial[layer], b_spatial=b_spatial[layer], w_out=w_out16,
        )
        y_p, cb_p, ds_p, _ = _trunk_layer(y_p, mod_p, wts, layer, alpha, None, None)
        y_s, cb_s, ds_s, vr_s = _trunk_layer(y_s, mod_s, wts, layer, alpha, state_conv[layer],
                                             state_delta[layer])
        delta_p.append(ds_p)
        conv_p.append(cb_p)
        delta_s.append(ds_s)
        conv_s.append(cb_s)
        vrows_s.append(vr_s)
    return (y_p, y_s, jnp.stack(delta_p), jnp.stack(conv_p), jnp.stack(delta_s), jnp.stack(conv_s),
            jnp.stack(vrows_s))
```

```python
import functools
import math

import jax
import jax.numpy as jnp
from jax import lax
from jax.experimental import pallas as pl
from jax.experimental.pallas import tpu as pltpu

F32 = jnp.float32
BF16 = jnp.bfloat16

DN_HEADS = 8
DN_CHUNK = 64
CONV_W = 4
MLP_GROUPS = 4
MLP_CHUNK = 128
N_SUB = 3
LN_EPS = 1e-5
RMS_EPS = 1e-6

LANES = 128
SUBLANES = 8
MIB = 2 ** 20

FFN_TM = 1024
FFN_TF = 256
FFN_TF16 = 512
FFN_RC = 256
FFN_VMEM_LIMIT = 58 * MIB
PROJ_TN = 1280
PROJ_N = 6400
PREP_TC = 256
MIX_TM = 512
ADA_TN = 1024
SEQ_TILE = 128
GDN_TILE_CHAINED = 256
MLP_TILE = 512


def _dot(a, b, precision=None):
    return jnp.dot(a, b, preferred_element_type=F32, precision=precision)


def _dot_nt(a, b, precision=None):
    return lax.dot_general(a, b, (((1,), (1,)), ((), ())),
                           preferred_element_type=F32, precision=precision)


def _dot_tn(a, b):
    return lax.dot_general(a, b, (((0,), (0,)), ((), ())), preferred_element_type=F32)


def _layer_norm(y, g, b):
    mu = jnp.mean(y, axis=-1, keepdims=True)
    yc = y - mu
    var = jnp.mean(yc * yc, axis=-1, keepdims=True)
    return yc * lax.rsqrt(var + LN_EPS) * g + b


def _silu(x):
    return x * jax.nn.sigmoid(x)


def _mod_rows(ref, seq_tiles):
    if seq_tiles:
        return ref[pl.ds(pl.program_id(0) // seq_tiles, 1), :]
    return ref[...]


def _ada_body(c_ref, w_ref, b_ref, ms_ref, mp_ref):
    c = c_ref[...]
    a = _silu(c).astype(BF16)
    y = _dot(a, w_ref[...].astype(BF16)) + b_ref[...]
    ns = ms_ref.shape[0]
    ms_ref[...] = y[:ns]
    mp_ref[...] = y[ns:]


def _ada(c_all, w_ada, b_ada, n_sample):
    rows, d = c_all.shape
    n = w_ada.shape[1]
    return pl.pallas_call(
        _ada_body,
        name="ada_mod",
        grid=(n // ADA_TN,),
        in_specs=[
            pl.BlockSpec((rows, d), lambda j: (0, 0)),
            pl.BlockSpec((d, ADA_TN), lambda j: (0, j)),
            pl.BlockSpec((1, ADA_TN), lambda j: (0, j)),
        ],
        out_specs=[
            pl.BlockSpec((n_sample, ADA_TN), lambda j: (0, j)),
            pl.BlockSpec((rows - n_sample, ADA_TN), lambda j: (0, j)),
        ],
        out_shape=[
            jax.ShapeDtypeStruct((n_sample, n), F32),
            jax.ShapeDtypeStruct((rows - n_sample, n), F32),
        ],
        compiler_params=pltpu.CompilerParams(
            dimension_semantics=("arbitrary",), vmem_limit_bytes=40 * MIB),
    )(c_all, w_ada, b_ada.reshape(1, n))


class _Group:
    def __init__(self, n_seq, n_rows, tm):
        if n_rows >= tm:
            assert n_rows % tm == 0
            self.sb, self.rb = 1, tm
        else:
            assert tm % n_rows == 0 and n_seq % (tm // n_rows) == 0
            self.sb, self.rb = tm // n_rows, n_rows
        self.n_seq, self.n_rows = n_seq, n_rows
        self.tiles_r = n_rows // self.rb
        self.tm = self.sb * self.rb
        self.n_tiles = (n_seq // self.sb) * self.tiles_r
        self.seq_tiles = self.tiles_r if self.sb == 1 else 0

    def x_spec(self, d, ngrid, **kw):
        tr = self.tiles_r
        if ngrid == 2:
            return pl.BlockSpec((self.sb, self.rb, d), lambda m, f: (m // tr, m % tr, 0), **kw)
        return pl.BlockSpec((self.sb, self.rb, d), lambda m: (m // tr, m % tr, 0), **kw)

    def mod_spec(self, mod, d, col, ngrid):
        if self.sb == 1:
            rows, row_blk = mod.shape[0], (lambda m: 0)
        else:
            rows, row_blk = self.sb, (lambda m: m)
        if ngrid == 2:
            return pl.BlockSpec((rows, d), lambda m, f: (row_blk(m), col))
        return pl.BlockSpec((rows, d), lambda m: (row_blk(m), col))


def _ffn_body(x_ref, sh_ref, sc_ref, gt_ref, wg_ref, wu_ref, wd_ref, lng_ref, lnb_ref,
              o_ref, *rest, seq_tiles, alpha):
    w16_refs, h_sc = rest[:-1], rest[-1]
    f = pl.program_id(1)
    nf = pl.num_programs(1)
    tm, d = o_ref.shape
    sb, rb, _ = x_ref.shape
    rc = min(FFN_RC, tm)
    n_chunks = tm // rc
    if w16_refs:
        for src, dst in zip((wg_ref, wu_ref, wd_ref), w16_refs):
            dst[...] = src[...].astype(BF16)
        wg_ref, wu_ref, wd_ref = w16_refs

    def x_chunk(c):
        if sb == 1:
            return x_ref[:, c * rc:(c + 1) * rc, :]
        return x_ref[c * (rc // rb):(c + 1) * (rc // rb)]

    def mod_chunk(ref, c):
        if seq_tiles:
            return _mod_rows(ref, seq_tiles)
        return ref[c * (rc // rb):(c + 1) * (rc // rb), :]

    def weights():
        return wg_ref[...].astype(BF16), wu_ref[...].astype(BF16), wd_ref[...].astype(BF16)

    def gate_up(h16, wg16, wu16):
        return (_silu(_dot(h16, wg16)) * _dot(h16, wu16)).astype(BF16)

    @pl.when(f == 0)
    def _():
        wg16, wu16, wd16 = weights()
        for c in range(n_chunks):
            rs = slice(c * rc, (c + 1) * rc)
            x = x_chunk(c)
            h = x * (1.0 + mod_chunk(sc_ref, c)[:, None, :]) + mod_chunk(sh_ref, c)[:, None, :]
            h16 = h.reshape(rc, d).astype(BF16)
            h_sc[rs, :] = h16
            o_ref[rs, :] = _dot(gate_up(h16, wg16, wu16), wd16)

    @pl.when(jnp.logical_and(f > 0, f < nf - 1))
    def _():
        wg16, wu16, wd16 = weights()
        o_ref[...] += _dot(gate_up(h_sc[...], wg16, wu16), wd16)

    @pl.when(f == nf - 1)
    def _():
        wg16, wu16, wd16 = weights()
        for c in range(n_chunks):
            rs = slice(c * rc, (c + 1) * rc)
            acc = o_ref[rs, :] + _dot(gate_up(h_sc[rs, :], wg16, wu16), wd16)
            x = x_chunk(c)
            gate = mod_chunk(gt_ref, c)[:, None, :]
            y = alpha * x + (0.5 * gate) * acc.reshape(x.shape)
            o_ref[rs, :] = _layer_norm(y.reshape(rc, d), lng_ref[...], lnb_ref[...])


def _ffn(x3, mod, sub, weights, ln_g, ln_b, alpha, emit_w16=False):
    n_seq, n_rows, d = x3.shape
    grp = _Group(n_seq, n_rows, FFN_TM)
    tm = grp.tm
    rc = min(FFN_RC, tm)
    if len(weights) == 5:
        wg, wu, wd, layer, slot = weights
        tf = FFN_TF
        w_specs = [
            pl.BlockSpec((None, None, d, tf), lambda m, f: (layer, slot, 0, f)),
            pl.BlockSpec((None, None, d, tf), lambda m, f: (layer, slot, 0, f)),
            pl.BlockSpec((None, None, tf, d), lambda m, f: (layer, slot, f, 0)),
        ]
    else:
        assert not emit_w16
        wg, wu, wd = weights
        tf = FFN_TF16
        w_specs = [
            pl.BlockSpec((d, tf), lambda m, f: (0, f)),
            pl.BlockSpec((d, tf), lambda m, f: (0, f)),
            pl.BlockSpec((tf, d), lambda m, f: (f, 0)),
        ]
    dff = wg.shape[-1]
    assert dff // tf >= 2 and tm % rc == 0 and (grp.rb % rc == 0 if grp.sb == 1 else rc % grp.rb == 0)
    out_specs = [pl.BlockSpec((tm, d), lambda m, f: (m, 0))]
    out_shape = [jax.ShapeDtypeStruct((n_seq * n_rows, d), F32)]
    if emit_w16:
        assert grp.n_tiles == 1
        out_specs += [
            pl.BlockSpec((d, tf), lambda m, f: (0, f)),
            pl.BlockSpec((d, tf), lambda m, f: (0, f)),
            pl.BlockSpec((tf, d), lambda m, f: (f, 0)),
        ]
        out_shape += [
            jax.ShapeDtypeStruct((d, dff), BF16),
            jax.ShapeDtypeStruct((d, dff), BF16),
            jax.ShapeDtypeStruct((dff, d), BF16),
        ]
    outs = pl.pallas_call(
        functools.partial(_ffn_body, seq_tiles=grp.seq_tiles, alpha=alpha),
        name="swiglu_ln",
        grid=(grp.n_tiles, dff // tf),
        in_specs=[
            grp.x_spec(d, 2),
            grp.mod_spec(mod, d, sub * 3 + 0, 2),
            grp.mod_spec(mod, d, sub * 3 + 1, 2),
            grp.mod_spec(mod, d, sub * 3 + 2, 2),
            *w_specs,
            pl.BlockSpec((1, d), lambda m, f: (0, 0)),
            pl.BlockSpec((1, d), lambda m, f: (0, 0)),
        ],
        out_specs=out_specs,
        out_shape=out_shape,
        scratch_shapes=[pltpu.VMEM((tm, d), BF16)],
        compiler_params=pltpu.CompilerParams(
            dimension_semantics=("arbitrary", "arbitrary"), vmem_limit_bytes=FFN_VMEM_LIMIT),
    )(x3, mod, mod, mod, wg, wu, wd, ln_g.reshape(1, d), ln_b.reshape(1, d))
    y = outs[0].reshape(n_seq, n_rows, d)
    return (y, tuple(outs[1:])) if emit_w16 else y


def _wprep_body(w_ref, o_ref, *, qkvz, gates):
    n_in, cols = w_ref.shape
    n_out = o_ref.shape[0]
    mlp = n_in - qkvz - gates
    o_ref[0:qkvz, :] = w_ref[0:qkvz, :].astype(BF16)
    o_ref[qkvz:qkvz + mlp, :] = w_ref[qkvz + gates:n_in, :].astype(BF16)
    o_ref[qkvz + mlp:n_in, :] = w_ref[qkvz:qkvz + gates, :].astype(BF16)
    o_ref[n_in:n_out, :] = jnp.zeros((n_out - n_in, cols), BF16)


def _wprep(w_in_t, layer, qkvz, gates):
    _, n_in, d = w_in_t.shape
    assert n_in <= PROJ_N and (n_in - gates) % LANES == 0
    return pl.pallas_call(
        functools.partial(_wprep_body, qkvz=qkvz, gates=gates),
        name="proj_weight_prep",
        grid=(d // PREP_TC,),
        in_specs=[pl.BlockSpec((None, n_in, PREP_TC), lambda i: (layer, 0, i))],
        out_specs=pl.BlockSpec((PROJ_N, PREP_TC), lambda i: (0, i)),
        out_shape=jax.ShapeDtypeStruct((PROJ_N, d), BF16),
        compiler_params=pltpu.CompilerParams(
            dimension_semantics=("arbitrary",), vmem_limit_bytes=40 * MIB),
    )(w_in_t)


def _proj_body(x_ref, sh_ref, sc_ref, w_ref, o_ref, h_sc, *, seq_tiles):
    tm, d = h_sc.shape

    @pl.when(pl.program_id(1) == 0)
    def _():
        x = x_ref[...]
        h = x * (1.0 + _mod_rows(sc_ref, seq_tiles)[:, None, :]) + _mod_rows(sh_ref, seq_tiles)[:, None, :]
        h_sc[...] = h.reshape(tm, d).astype(BF16)

    o_ref[...] = _dot_nt(h_sc[...], w_ref[...])


def _proj(x3, mod, sub, w_cat_t):
    n_seq, n_rows, d = x3.shape
    n = w_cat_t.shape[0]
    grp = _Group(n_seq, n_rows, FFN_TM)
    tm = grp.tm
    return pl.pallas_call(
        functools.partial(_proj_body, seq_tiles=grp.seq_tiles),
        name="in_proj",
        grid=(grp.n_tiles, n // PROJ_TN),
        in_specs=[
            grp.x_spec(d, 2),
            grp.mod_spec(mod, d, sub * 3 + 0, 2),
            grp.mod_spec(mod, d, sub * 3 + 1, 2),
            pl.BlockSpec((PROJ_TN, d), lambda m, j: (j, 0)),
        ],
        out_specs=pl.BlockSpec((tm, PROJ_TN), lambda m, j: (m, j)),
        out_shape=jax.ShapeDtypeStruct((n_seq * n_rows, n), F32),
        scratch_shapes=[pltpu.VMEM((tm, d), BF16)],
        compiler_params=pltpu.CompilerParams(
            dimension_semantics=("arbitrary", "arbitrary"), vmem_limit_bytes=48 * MIB),
    )(x3, mod, mod, w_cat_t)


def _block_masks(n, blk):
    shift = int(math.log2(blk))
    i = lax.broadcasted_iota(jnp.int32, (n, n), 0)
    j = lax.broadcasted_iota(jnp.int32, (n, n), 1)
    same = (i >> shift) == (j >> shift)
    return i, j, same


def _gdn_body(*refs, blk, chained, tiles_per_seq, dk_scale):
    if chained:
        (qkv_ref, z_ref, ba_ref, cw_ref, gp_ref, ng_ref, o_ref, st_ref, xc_sc, act_sc) = refs
    else:
        (qkv_ref, z_ref, ba_ref, buf_ref, sin_ref, cw_ref, gp_ref, ng_ref, o_ref, st_ref, act_sc) = refs
    lt, cqkv = qkv_ref.shape
    hd = LANES
    nh = cqkv // (3 * hd)
    t = pl.program_id(0)
    halo = SUBLANES
    ngrp = lt // SUBLANES

    if chained:
        first = (t % tiles_per_seq) == 0

        @pl.when(first)
        def _():
            xc_sc[pl.ds(0, halo), :] = jnp.zeros((halo, cqkv), F32)
            st_ref[...] = jnp.zeros_like(st_ref)

        @pl.when(jnp.logical_not(first))
        def _():
            xc_sc[pl.ds(0, halo), :] = xc_sc[pl.ds(lt, halo), :]

        xc_sc[pl.ds(halo, lt), :] = qkv_ref[...]
    else:
        assert blk == SUBLANES

    cblk = 2 * LANES
    row_in_grp = lax.broadcasted_iota(jnp.int32, (ngrp, SUBLANES, cblk), 1)
    for c0 in range(0, cqkv, cblk):
        cs = slice(c0, c0 + cblk)
        if chained:
            xe = xc_sc[:, cs].reshape(ngrp + 1, SUBLANES, cblk)
            cur = xe[1:]
        else:
            cur = qkv_ref[:, cs].reshape(ngrp, SUBLANES, cblk)
            prev = buf_ref[:, cs].reshape(ngrp, SUBLANES, cblk)
        y = None
        for j in range(CONV_W):
            d = CONV_W - 1 - j
            if d == 0:
                term = cur
            elif chained:
                rot = pltpu.roll(xe, d, axis=1)
                term = jnp.where(row_in_grp >= d, rot[1:], rot[:-1])
            else:
                term = jnp.where(row_in_grp >= d, pltpu.roll(cur, d, axis=1), pltpu.roll(prev, d, axis=1))
            term = cw_ref[j:j + 1, cs][None] * term
            y = term if y is None else y + term
        act_sc[:, cs] = _silu(y).reshape(lt, cblk)

    st = SEQ_TILE
    nsub = lt // st
    i, j, same = _block_masks(st, blk)
    incl = same & (i >= j)
    strict = same & (i > j)
    hi = lax.Precision.HIGHEST
    incl_f = incl.astype(F32)
    last_f = (j == (i | (blk - 1))).astype(F32)
    eye = (i == j).astype(F32)
    pair_masks = []
    for lvl in range(int(math.log2(blk))):
        pair_masks.append(((i >> (lvl + 1)) == (j >> (lvl + 1))) & ((i >> lvl) != (j >> lvl)))

    heads = range(nh)
    hp = []
    for s in range(nsub):
        rows = slice(s * st, (s + 1) * st)
        ba = ba_ref[rows, :]
        beta_all = jax.nn.sigmoid(ba)
        g_all = -jnp.exp(gp_ref[0:1, :]) * jax.nn.softplus(ba + gp_ref[1:2, :])
        gc_all = _dot(incl_f, g_all, hi)
        gl_all = _dot(last_f, gc_all, hi)
        gc_t = gc_all.T
        for h in heads:
            q = act_sc[rows, h * hd:(h + 1) * hd]
            k = act_sc[rows, (nh + h) * hd:(nh + h + 1) * hd]
            v = act_sc[rows, (2 * nh + h) * hd:(2 * nh + h + 1) * hd]
            q = q * lax.rsqrt(jnp.sum(q * q, axis=-1, keepdims=True) + RMS_EPS) * dk_scale
            k = k * lax.rsqrt(jnp.sum(k * k, axis=-1, keepdims=True) + RMS_EPS)
            beta = beta_all[:, h:h + 1]
            gcc = gc_all[:, nh + h:nh + h + 1]
            gcr = gc_t[nh + h:nh + h + 1, :]
            glc = gl_all[:, nh + h:nh + h + 1]
            decay = jnp.exp(jnp.where(incl, gcc - gcr, -jnp.inf))
            kb = k * beta
            k16 = k.astype(BF16)
            egc = jnp.exp(gcc)
            hp.append(dict(
                sub=s, head=h, decay=decay, k16=k16, kb16=kb.astype(BF16), q16=q.astype(BF16), glc=glc,
                rhs16=jnp.concatenate([v * beta, kb * egc], axis=1).astype(BF16),
                qd=q * egc, kd=k * jnp.exp(glc - gcc)))

    ms = [_dot_nt(p["kb16"], p["k16"]) * jnp.where(strict, p["decay"], 0.0) for p in hp]
    dinvs = [eye - jnp.where(pair_masks[0], m, 0.0) for m in ms]
    for pm in pair_masks[1:]:
        d16s = [d.astype(BF16) for d in dinvs]
        t16s = [_dot(d16, jnp.where(pm, m, 0.0).astype(BF16)).astype(BF16) for d16, m in zip(d16s, ms)]
        dinvs = [d - _dot(t16, d16) for d, t16, d16 in zip(dinvs, t16s, d16s)]
    xs = [_dot(d.astype(BF16), p["rhs16"]) for d, p in zip(dinvs, hp)]
    qk16s = [(_dot_nt(p["q16"], p["k16"]) * p["decay"]).astype(BF16) for p in hp]

    blk_per_sub = st // blk
    vn_parts = [[] for _ in hp]
    os_parts = [[] for _ in hp]
    for b in range(lt // blk):
        lb = b % blk_per_sub
        rs = slice(lb * blk, (lb + 1) * blk)
        cidx = [(b // blk_per_sub) * nh + h for h in heads]
        s0s = [st_ref[0, h] if chained else sin_ref[b, h] for h in heads]
        rr = [_dot(jnp.concatenate([xs[c][rs, hd:], hp[c]["qd"][rs]], axis=0).astype(BF16), s0.astype(BF16))
              for c, s0 in zip(cidx, s0s)]
        vns = [xs[c][rs, :hd] - r[:blk] for c, r in zip(cidx, rr)]
        for h, c in zip(heads, cidx):
            p = hp[c]
            s_new = (s0s[h] * jnp.exp(p["glc"][lb * blk:lb * blk + 1, :])
                     + _dot_tn(p["kd"][rs].astype(BF16), vns[h].astype(BF16)))
            if chained:
                st_ref[0, h] = s_new
            else:
                st_ref[b, h] = s_new
            vn_parts[c].append(vns[h])
            os_parts[c].append(rr[h][blk:])
    for c, p in enumerate(hp):
        rows = slice(p["sub"] * st, (p["sub"] + 1) * st)
        cols = slice(p["head"] * hd, (p["head"] + 1) * hd)
        vn_all = jnp.concatenate(vn_parts[c], axis=0)
        o = jnp.concatenate(os_parts[c], axis=0) + _dot(qk16s[c], vn_all.astype(BF16))
        o = o * lax.rsqrt(jnp.mean(o * o, axis=-1, keepdims=True) + RMS_EPS) * ng_ref[...]
        o = o * _silu(z_ref[rows, cols])
        o_ref[rows, cols] = o.astype(BF16)


def _gdn(proj, conv_w, gate_par, norm_g, *, n_seq, n_rows, blk, ba_col, conv_buf=None, s0=None):
    t_rows = proj.shape[0]
    nh = DN_HEADS
    hd = LANES
    cqkv = 3 * nh * hd
    chained = s0 is None
    lt = GDN_TILE_CHAINED if chained else SEQ_TILE
    assert lt % SEQ_TILE == 0 and SEQ_TILE % blk == 0 and blk & (blk - 1) == 0 and blk >= 2
    n_tiles = t_rows // lt
    common_in = [
        pl.BlockSpec((lt, cqkv), lambda t: (t, 0)),
        pl.BlockSpec((lt, nh * hd), lambda t: (t, 3)),
        pl.BlockSpec((lt, LANES), lambda t: (t, ba_col)),
    ]
    par_in = [
        pl.BlockSpec(conv_w.shape, lambda t: (0, 0)),
        pl.BlockSpec(gate_par.shape, lambda t: (0, 0)),
        pl.BlockSpec((1, hd), lambda t: (0, 0)),
    ]
    o_spec = pl.BlockSpec((lt, nh * hd), lambda t: (t, 0))
    o_shape = jax.ShapeDtypeStruct((t_rows, nh * hd), BF16)
    scratch = []
    if chained:
        assert n_rows % lt == 0 and blk == DN_CHUNK
        tps = n_rows // lt
        in_specs = common_in + par_in
        args = (proj, proj, proj, conv_w, gate_par, norm_g)
        st_spec = pl.BlockSpec((1, nh, hd, hd), lambda t: (t // tps, 0, 0, 0))
        scratch.append(pltpu.VMEM((lt + SUBLANES, cqkv), F32))
    else:
        assert blk == n_rows and conv_buf is not None
        tps = 0
        spt = lt // n_rows
        in_specs = common_in + [
            pl.BlockSpec((lt, cqkv), lambda t: (t, 0)),
            pl.BlockSpec((spt, nh, hd, hd), lambda t: (t, 0, 0, 0)),
        ] + par_in
        args = (proj, proj, proj, conv_buf, s0, conv_w, gate_par, norm_g)
        st_spec = pl.BlockSpec((spt, nh, hd, hd), lambda t: (t, 0, 0, 0))
    scratch.append(pltpu.VMEM((lt, cqkv), F32))
    return pl.pallas_call(
        functools.partial(_gdn_body, blk=blk, chained=chained, tiles_per_seq=tps,
                          dk_scale=float(hd) ** -0.5),
        name="gdn_chained" if chained else "gdn_stateful",
        grid=(n_tiles,),
        in_specs=in_specs,
        out_specs=[o_spec, st_spec],
        out_shape=[o_shape, jax.ShapeDtypeStruct((n_seq, nh, hd, hd), F32)],
        scratch_shapes=scratch,
        compiler_params=pltpu.CompilerParams(
            dimension_semantics=("arbitrary",), vmem_limit_bytes=56 * MIB),
    )(*args)


def _mlp_body(u_ref, v_ref, ws_ref, bst_ref, lng_ref, lnb_ref, o_ref, *maybe_vr_ref, blk):
    ck = ws_ref.shape[1]
    gd = u_ref.shape[1] // MLP_GROUPS
    i, j, same = _block_masks(ck, blk)
    incl = same & (i >= j)
    hi = lax.Precision.HIGHEST
    if blk != ck:
        rep = (j == (i & (blk - 1))).astype(F32)
        bias_all = _dot(rep, bst_ref[...], hi)
    else:
        bias_all = bst_ref[...]
    for g in range(MLP_GROUPS):
        cs = slice(g * gd, (g + 1) * gd)
        wsp = ws_ref[g]
        if blk != ck:
            wsp = _dot_nt(_dot(rep, wsp, hi), rep, hi)
        wsp = jnp.where(incl, wsp, 0.0).astype(BF16)
        for c in range(u_ref.shape[0] // ck):
            rs = slice(c * ck, (c + 1) * ck)
            uu = jax.nn.gelu(u_ref[rs, cs])
            vv = _layer_norm(jax.nn.gelu(v_ref[rs, cs]), lng_ref[:, cs], lnb_ref[:, cs])
            if maybe_vr_ref:
                maybe_vr_ref[0][rs, cs] = vv
            s = _dot(wsp, vv.astype(BF16)) + bias_all[:, g:g + 1]
            o_ref[rs, cs] = (uu * s).astype(BF16)


def _mlp(proj, w_spatial, b_spatial, ln_g, ln_b, *, blk, want_v_rows):
    t_rows = proj.shape[0]
    lt = MLP_TILE
    ck = w_spatial.shape[1]
    width = ln_g.size
    assert ck == MLP_CHUNK and lt % ck == 0 and ck % blk == 0 and t_rows % lt == 0
    u_col = (4 * DN_HEADS * LANES) // width
    bst = jnp.zeros((ck, LANES), F32).at[:, :MLP_GROUPS].set(b_spatial.T)
    o_spec = pl.BlockSpec((lt, width), lambda t: (t, 0))
    out_specs = [o_spec]
    out_shape = [jax.ShapeDtypeStruct((t_rows, width), BF16)]
    if want_v_rows:
        out_specs.append(o_spec)
        out_shape.append(jax.ShapeDtypeStruct((t_rows, width), F32))
    return pl.pallas_call(
        functools.partial(_mlp_body, blk=blk),
        name="spatial_mlp",
        grid=(t_rows // lt,),
        in_specs=[
            pl.BlockSpec((lt, width), lambda t: (t, u_col)),
            pl.BlockSpec((lt, width), lambda t: (t, u_col + 1)),
            pl.BlockSpec(w_spatial.shape, lambda t: (0, 0, 0)),
            pl.BlockSpec((ck, LANES), lambda t: (0, 0)),
            pl.BlockSpec((1, width), lambda t: (0, 0)),
            pl.BlockSpec((1, width), lambda t: (0, 0)),
        ],
        out_specs=out_specs,
        out_shape=out_shape,
        compiler_params=pltpu.CompilerParams(
            dimension_semantics=("arbitrary",), vmem_limit_bytes=32 * MIB),
    )(proj, proj, w_spatial, bst, ln_g.reshape(1, width), ln_b.reshape(1, width))


def _mix_body(od_ref, om_ref, x_ref, gt_ref, w_ref, lng_ref, lnb_ref, o_ref, w_sc, *, seq_tiles, alpha):
    tm, d = o_ref.shape
    kd = od_ref.shape[1]

    @pl.when(pl.program_id(0) == 0)
    def _():
        w_sc[...] = w_ref[...].astype(BF16)

    sb, rb, _ = x_ref.shape
    rc = min(FFN_RC, tm)
    for c in range(tm // rc):
        rs = slice(c * rc, (c + 1) * rc)
        mix = _dot(od_ref[rs, :], w_sc[:kd, :]) + _dot(om_ref[rs, :], w_sc[kd:, :])
        if sb == 1:
            x = x_ref[:, rs, :]
            gate = _mod_rows(gt_ref, seq_tiles)[:, None, :]
        else:
            ss = slice(c * (rc // rb), (c + 1) * (rc // rb))
            x = x_ref[ss]
            gate = gt_ref[ss, :][:, None, :]
        y = alpha * x + gate * mix.reshape(x.shape)
        o_ref[rs, :] = _layer_norm(y.reshape(rc, d), lng_ref[...], lnb_ref[...])


def _mix(o_dn, o_mlp, x3, mod, sub, w_out, layer, ln_g, ln_b, alpha):
    n_seq, n_rows, d = x3.shape
    grp = _Group(n_seq, n_rows, MIX_TM)
    tm = grp.tm
    out = pl.pallas_call(
        functools.partial(_mix_body, seq_tiles=grp.seq_tiles, alpha=alpha),
        name="out_mix_ln",
        grid=(grp.n_tiles,),
        in_specs=[
            pl.BlockSpec((tm, o_dn.shape[1]), lambda m: (m, 0)),
            pl.BlockSpec((tm, o_mlp.shape[1]), lambda m: (m, 0)),
            grp.x_spec(d, 1),
            grp.mod_spec(mod, d, sub * 3 + 2, 1),
            pl.BlockSpec((None,) + w_out.shape[1:], lambda m: (layer, 0, 0), pipeline_mode=pl.Buffered(1)),
            pl.BlockSpec((1, d), lambda m: (0, 0)),
            pl.BlockSpec((1, d), lambda m: (0, 0)),
        ],
        out_specs=pl.BlockSpec((tm, d), lambda m: (m, 0)),
        out_shape=jax.ShapeDtypeStruct((n_seq * n_rows, d), F32),
        scratch_shapes=[pltpu.VMEM(w_out.shape[1:], BF16)],
        compiler_params=pltpu.CompilerParams(
            dimension_semantics=("arbitrary",), vmem_limit_bytes=56 * MIB),
    )(o_dn, o_mlp, x3, mod, w_out, ln_g.reshape(1, d), ln_b.reshape(1, d))
    return out.reshape(n_seq, n_rows, d)


def _trunk_layer(x3, mod, wts, layer, alpha, conv_buf, s0, ffn_weights, emit_w16):
    n_seq, n_rows, d = x3.shape
    qkv_dim = 3 * DN_HEADS * LANES
    w16 = []

    def ffn(x3, sub, slot):
        r = _ffn(x3, mod, sub, ffn_weights[slot], wts["ln_g"][sub], wts["ln_b"][sub], alpha, emit_w16)
        if emit_w16:
            w16.append(r[1])
            return r[0]
        return r

    x3 = ffn(x3, 0, 0)
    proj = _proj(x3, mod, 1, wts["w_cat"])
    dn_blk = DN_CHUNK if n_rows % DN_CHUNK == 0 else n_rows
    gdn_kw = dict(n_seq=n_seq, n_rows=n_rows, blk=dn_blk, ba_col=wts["ba_col"])
    if s0 is None:
        o_dn, s_new = _gdn(proj, wts["conv_w"], wts["gate_par"], wts["dn_norm_g"], **gdn_kw)
    else:
        pad = jnp.zeros((n_seq, SUBLANES - (CONV_W - 1), qkv_dim), F32)
        buf = jnp.concatenate([pad, conv_buf], axis=1).reshape(n_seq * SUBLANES, qkv_dim)
        assert n_rows == SUBLANES
        o_dn, s_new = _gdn(proj, wts["conv_w"], wts["gate_par"], wts["dn_norm_g"],
                           conv_buf=buf, s0=s0, **gdn_kw)
    mlp_out = _mlp(proj, wts["w_spatial"], wts["b_spatial"], wts["mlp_ln_g"], wts["mlp_ln_b"],
                   blk=min(n_rows, MLP_CHUNK), want_v_rows=s0 is not None)
    o_mlp = mlp_out[0]
    v_rows = mlp_out[1].reshape(n_seq, n_rows, -1) if s0 is not None else None
    x3 = _mix(o_dn, o_mlp, x3, mod, 1, wts["w_out"], layer, wts["ln_g"][1], wts["ln_b"][1], alpha)
    x3 = ffn(x3, 2, 1)
    new_buf = proj.reshape(n_seq, n_rows, -1)[:, n_rows - (CONV_W - 1):, :qkv_dim]
    return x3, new_buf, s_new, v_rows, w16


def kernel(x_prompt, x_sample, c_prompt, c_sample, state_delta, state_conv, w_ada, b_ada, ln_g, ln_b,
           ffn_wg, ffn_wu, ffn_wd, w_in, conv_w, a_log, dt_bias, dn_norm_g, mlp_ln_g, mlp_ln_b,
           w_spatial, b_spatial, w_out):
    depth = w_ada.shape[0]
    bp, _, d = x_prompt.shape
    bs = x_sample.shape[0]
    nh = DN_HEADS
    alpha = (2.0 * depth) ** 0.25
    qkvz = 4 * nh * LANES
    gates = 2 * nh

    pad_rows = (-(bs + bp)) % SUBLANES
    c_all = jnp.concatenate([c_sample, c_prompt, jnp.zeros((pad_rows, d), F32)], axis=0)
    w_in_t = jnp.swapaxes(w_in, 1, 2)

    y_p, y_s = x_prompt, x_sample
    delta_p, conv_p, delta_s, conv_s, vrows_s = [], [], [], [], []
    for layer in range(depth):
        mod_s, mod_p = _ada(c_all, w_ada[layer], b_ada[layer], bs)
        gate_par = jnp.zeros((SUBLANES, LANES), F32)
        gate_par = gate_par.at[0, nh:gates].set(a_log[layer]).at[1, nh:gates].set(dt_bias[layer])
        wts = dict(
            ln_g=ln_g[layer], ln_b=ln_b[layer],
            w_cat=_wprep(w_in_t, layer, qkvz, gates), ba_col=(w_in.shape[-1] - gates) // LANES,
            conv_w=conv_w[layer], gate_par=gate_par,
            dn_norm_g=dn_norm_g[layer].reshape(1, LANES), mlp_ln_g=mlp_ln_g[layer], mlp_ln_b=mlp_ln_b[layer],
            w_spatial=w_spatial[layer], b_spatial=b_spatial[layer], w_out=w_out,
        )
        w32 = [(ffn_wg, ffn_wu, ffn_wd, layer, slot) for slot in range(2)]
        y_s, cb_s, ds_s, vr_s, w16 = _trunk_layer(y_s, mod_s, wts, layer, alpha, state_conv[layer],
                                                  state_delta[layer], w32, True)
        y_p, cb_p, ds_p, _, _ = _trunk_layer(y_p, mod_p, wts, layer, alpha, None, None, w16, False)
        delta_p.append(ds_p)
        conv_p.append(cb_p)
        delta_s.append(ds_s)
        conv_s.append(cb_s)
        vrows_s.append(vr_s)
    return (y_p, y_s, jnp.stack(delta_p), jnp.stack(conv_p), jnp.stack(delta_s), jnp.stack(conv_s),
            jnp.stack(vrows_s))
```

```python
import functools
import math

import jax
import jax.numpy as jnp
from jax import lax
from jax.experimental import pallas as pl
from jax.experimental.pallas import tpu as pltpu

F32 = jnp.float32
BF16 = jnp.bfloat16

DN_HEADS = 8
DN_CHUNK = 64
CONV_W = 4
MLP_GROUPS = 4
MLP_CHUNK = 128
N_SUB = 3
LN_EPS = 1e-5
RMS_EPS = 1e-6

LANES = 128
SUBLANES = 8
MIB = 2 ** 20

FFN_TM = 1024
FFN_TF = 256
FFN_TF16 = 512
FFN_RC = 256
FFN_VMEM_LIMIT = 58 * MIB
PROJ_TN = 1280
PROJ_N = 6400
PREP_TC = 256
MIX_TM = 512
ADA_TN = 1024
SEQ_TILE = 128
GDN_TILE_CHAINED = 256
MLP_TILE = 512


def _dot(a, b, precision=None):
    return jnp.dot(a, b, preferred_element_type=F32, precision=precision)


def _dot_nt(a, b, precision=None):
    return lax.dot_general(a, b, (((1,), (1,)), ((), ())),
                           preferred_element_type=F32, precision=precision)


def _dot_tn(a, b):
    return lax.dot_general(a, b, (((0,), (0,)), ((), ())), preferred_element_type=F32)


def _layer_norm(y, g, b):
    mu = jnp.mean(y, axis=-1, keepdims=True)
    yc = y - mu
    var = jnp.mean(yc * yc, axis=-1, keepdims=True)
    return yc * lax.rsqrt(var + LN_EPS) * g + b


def _silu(x):
    return x * jax.nn.sigmoid(x)


def _mod_rows(ref, seq_tiles):
    if seq_tiles:
        return ref[pl.ds(pl.program_id(0) // seq_tiles, 1), :]
    return ref[...]


def _ada_body(c_ref, w_ref, b_ref, ms_ref, mp_ref):
    c = c_ref[...]
    a = _silu(c).astype(BF16)
    y = _dot(a, w_ref[...].astype(BF16)) + b_ref[...]
    ns = ms_ref.shape[0]
    ms_ref[...] = y[:ns]
    mp_ref[...] = y[ns:]


def _ada(c_all, w_ada, b_ada, n_sample):
    rows, d = c_all.shape
    n = w_ada.shape[1]
    return pl.pallas_call(
        _ada_body,
        name="ada_mod",
        grid=(n // ADA_TN,),
        in_specs=[
            pl.BlockSpec((rows, d), lambda j: (0, 0)),
            pl.BlockSpec((d, ADA_TN), lambda j: (0, j)),
            pl.BlockSpec((1, ADA_TN), lambda j: (0, j)),
        ],
        out_specs=[
            pl.BlockSpec((n_sample, ADA_TN), lambda j: (0, j)),
            pl.BlockSpec((rows - n_sample, ADA_TN), lambda j: (0, j)),
        ],
        out_shape=[
            jax.ShapeDtypeStruct((n_sample, n), F32),
            jax.ShapeDtypeStruct((rows - n_sample, n), F32),
        ],
        compiler_params=pltpu.CompilerParams(
            dimension_semantics=("arbitrary",), vmem_limit_bytes=40 * MIB),
    )(c_all, w_ada, b_ada.reshape(1, n))


class _Group:
    def __init__(self, n_seq, n_rows, tm):
        if n_rows >= tm:
            assert n_rows % tm == 0
            self.sb, self.rb = 1, tm
        else:
            assert tm % n_rows == 0 and n_seq % (tm // n_rows) == 0
            self.sb, self.rb = tm // n_rows, n_rows
        self.n_seq, self.n_rows = n_seq, n_rows
        self.tiles_r = n_rows // self.rb
        self.tm = self.sb * self.rb
        self.n_tiles = (n_seq // self.sb) * self.tiles_r
        self.seq_tiles = self.tiles_r if self.sb == 1 else 0

    def x_spec(self, d, ngrid, **kw):
        tr = self.tiles_r
        if ngrid == 2:
            return pl.BlockSpec((self.sb, self.rb, d), lambda m, f: (m // tr, m % tr, 0), **kw)
        return pl.BlockSpec((self.sb, self.rb, d), lambda m: (m // tr, m % tr, 0), **kw)

    def mod_spec(self, mod, d, col, ngrid):
        if self.sb == 1:
            rows, row_blk = mod.shape[0], (lambda m: 0)
        else:
            rows, row_blk = self.sb, (lambda m: m)
        if ngrid == 2:
            return pl.BlockSpec((rows, d), lambda m, f: (row_blk(m), col))
        return pl.BlockSpec((rows, d), lambda m: (row_blk(m), col))


def _ffn_body(x_ref, sh_ref, sc_ref, gt_ref, wg_ref, wu_ref, wd_ref, lng_ref, lnb_ref,
              o_ref, *rest, seq_tiles, alpha):
    w16_refs, h_sc = rest[:-1], rest[-1]
    f = pl.program_id(1)
    nf = pl.num_programs(1)
    tm, d = o_ref.shape
    sb, rb, _ = x_ref.shape
    rc = min(FFN_RC, tm)
    n_chunks = tm // rc
    if w16_refs:
        for src, dst in zip((wg_ref, wu_ref, wd_ref), w16_refs):
            dst[...] = src[...].astype(BF16)
        wg_ref, wu_ref, wd_ref = w16_refs

    def x_chunk(c):
        if sb == 1:
            return x_ref[:, c * rc:(c + 1) * rc, :]
        return x_ref[c * (rc // rb):(c + 1) * (rc // rb)]

    def mod_chunk(ref, c):
        if seq_tiles:
            return _mod_rows(ref, seq_tiles)
        return ref[c * (rc // rb):(c + 1) * (rc // rb), :]

    def col_tiles(ref):
        if len(ref.shape) == 3:
            return [ref[i] for i in range(ref.shape[0])]
        return [ref[...].astype(BF16)]

    def weights():
        return col_tiles(wg_ref), col_tiles(wu_ref), wd_ref[...].astype(BF16)

    def gate_up(h16, wgs, wus):
        g = jnp.concatenate([_dot(h16, w) for w in wgs], axis=1)
        u = jnp.concatenate([_dot(h16, w) for w in wus], axis=1)
        return (_silu(g) * u).astype(BF16)

    @pl.when(f == 0)
    def _():
        wg16, wu16, wd16 = weights()
        for c in range(n_chunks):
            rs = slice(c * rc, (c + 1) * rc)
            x = x_chunk(c)
            h = x * (1.0 + mod_chunk(sc_ref, c)[:, None, :]) + mod_chunk(sh_ref, c)[:, None, :]
            h16 = h.reshape(rc, d).astype(BF16)
            h_sc[rs, :] = h16
            o_ref[rs, :] = _dot(gate_up(h16, wg16, wu16), wd16)

    @pl.when(jnp.logical_and(f > 0, f < nf - 1))
    def _():
        wg16, wu16, wd16 = weights()
        o_ref[...] += _dot(gate_up(h_sc[...], wg16, wu16), wd16)

    @pl.when(f == nf - 1)
    def _():
        wg16, wu16, wd16 = weights()
        for c in range(n_chunks):
            rs = slice(c * rc, (c + 1) * rc)
            acc = o_ref[rs, :] + _dot(gate_up(h_sc[rs, :], wg16, wu16), wd16)
            x = x_chunk(c)
            gate = mod_chunk(gt_ref, c)[:, None, :]
            y = alpha * x + (0.5 * gate) * acc.reshape(x.shape)
            o_ref[rs, :] = _layer_norm(y.reshape(rc, d), lng_ref[...], lnb_ref[...])


def _ffn(x3, mod, sub, weights, ln_g, ln_b, alpha, emit_w16=False):
    n_seq, n_rows, d = x3.shape
    grp = _Group(n_seq, n_rows, FFN_TM)
    tm = grp.tm
    rc = min(FFN_RC, tm)
    if len(weights) == 5:
        wg, wu, wd, layer, slot = weights
        tf = FFN_TF
        w_specs = [
            pl.BlockSpec((None, None, d, tf), lambda m, f: (layer, slot, 0, f)),
            pl.BlockSpec((None, None, d, tf), lambda m, f: (layer, slot, 0, f)),
            pl.BlockSpec((None, None, tf, d), lambda m, f: (layer, slot, f, 0)),
        ]
    else:
        assert not emit_w16
        wg, wu, wd = weights
        tf = FFN_TF16
        k = tf // wg.shape[-1]
        w_specs = [
            pl.BlockSpec((k, d, wg.shape[-1]), lambda m, f: (f, 0, 0)),
            pl.BlockSpec((k, d, wu.shape[-1]), lambda m, f: (f, 0, 0)),
            pl.BlockSpec((tf, d), lambda m, f: (f, 0)),
        ]
    dff = wd.shape[-2]
    assert dff // tf >= 2 and tm % rc == 0 and (grp.rb % rc == 0 if grp.sb == 1 else rc % grp.rb == 0)
    out_specs = [pl.BlockSpec((tm, d), lambda m, f: (m, 0))]
    out_shape = [jax.ShapeDtypeStruct((n_seq * n_rows, d), F32)]
    if emit_w16:
        assert grp.n_tiles == 1
        out_specs += [
            pl.BlockSpec((None, d, tf), lambda m, f: (f, 0, 0)),
            pl.BlockSpec((None, d, tf), lambda m, f: (f, 0, 0)),
            pl.BlockSpec((tf, d), lambda m, f: (f, 0)),
        ]
        out_shape += [
            jax.ShapeDtypeStruct((dff // tf, d, tf), BF16),
            jax.ShapeDtypeStruct((dff // tf, d, tf), BF16),
            jax.ShapeDtypeStruct((dff, d), BF16),
        ]
    outs = pl.pallas_call(
        functools.partial(_ffn_body, seq_tiles=grp.seq_tiles, alpha=alpha),
        name="swiglu_ln",
        grid=(grp.n_tiles, dff // tf),
        in_specs=[
            grp.x_spec(d, 2),
            grp.mod_spec(mod, d, sub * 3 + 0, 2),
            grp.mod_spec(mod, d, sub * 3 + 1, 2),
            grp.mod_spec(mod, d, sub * 3 + 2, 2),
            *w_specs,
            pl.BlockSpec((1, d), lambda m, f: (0, 0)),
            pl.BlockSpec((1, d), lambda m, f: (0, 0)),
        ],
        out_specs=out_specs,
        out_shape=out_shape,
        scratch_shapes=[pltpu.VMEM((tm, d), BF16)],
        compiler_params=pltpu.CompilerParams(
            dimension_semantics=("arbitrary", "arbitrary"), vmem_limit_bytes=FFN_VMEM_LIMIT),
    )(x3, mod, mod, mod, wg, wu, wd, ln_g.reshape(1, d), ln_b.reshape(1, d))
    y = outs[0].reshape(n_seq, n_rows, d)
    return (y, tuple(outs[1:])) if emit_w16 else y


def _wprep_body(w_ref, o_ref, *, qkvz, gates):
    n_in, cols = w_ref.shape
    n_out = o_ref.shape[0]
    mlp = n_in - qkvz - gates
    o_ref[0:qkvz, :] = w_ref[0:qkvz, :].astype(BF16)
    o_ref[qkvz:qkvz + mlp, :] = w_ref[qkvz + gates:n_in, :].astype(BF16)
    o_ref[qkvz + mlp:n_in, :] = w_ref[qkvz:qkvz + gates, :].astype(BF16)
    o_ref[n_in:n_out, :] = jnp.zeros((n_out - n_in, cols), BF16)


def _wprep(w_in_t, layer, qkvz, gates):
    _, n_in, d = w_in_t.shape
    assert n_in <= PROJ_N and (n_in - gates) % LANES == 0
    return pl.pallas_call(
        functools.partial(_wprep_body, qkvz=qkvz, gates=gates),
        name="proj_weight_prep",
        grid=(d // PREP_TC,),
        in_specs=[pl.BlockSpec((None, n_in, PREP_TC), lambda i: (layer, 0, i))],
        out_specs=pl.BlockSpec((PROJ_N, PREP_TC), lambda i: (0, i)),
        out_shape=jax.ShapeDtypeStruct((PROJ_N, d), BF16),
        compiler_params=pltpu.CompilerParams(
            dimension_semantics=("arbitrary",), vmem_limit_bytes=40 * MIB),
    )(w_in_t)


def _proj_body(x_ref, sh_ref, sc_ref, w_ref, o_ref, h_sc, *, seq_tiles):
    tm, d = h_sc.shape

    @pl.when(pl.program_id(1) == 0)
    def _():
        x = x_ref[...]
        h = x * (1.0 + _mod_rows(sc_ref, seq_tiles)[:, None, :]) + _mod_rows(sh_ref, seq_tiles)[:, None, :]
        h_sc[...] = h.reshape(tm, d).astype(BF16)

    o_ref[...] = _dot_nt(h_sc[...], w_ref[...])


def _proj(x3, mod, sub, w_cat_t):
    n_seq, n_rows, d = x3.shape
    n = w_cat_t.shape[0]
    grp = _Group(n_seq, n_rows, FFN_TM)
    tm = grp.tm
    return pl.pallas_call(
        functools.partial(_proj_body, seq_tiles=grp.seq_tiles),
        name="in_proj",
        grid=(grp.n_tiles, n // PROJ_TN),
        in_specs=[
            grp.x_spec(d, 2),
            grp.mod_spec(mod, d, sub * 3 + 0, 2),
            grp.mod_spec(mod, d, sub * 3 + 1, 2),
            pl.BlockSpec((PROJ_TN, d), lambda m, j: (j, 0)),
        ],
        out_specs=pl.BlockSpec((tm, PROJ_TN), lambda m, j: (m, j)),
        out_shape=jax.ShapeDtypeStruct((n_seq * n_rows, n), F32),
        scratch_shapes=[pltpu.VMEM((tm, d), BF16)],
        compiler_params=pltpu.CompilerParams(
            dimension_semantics=("arbitrary", "arbitrary"), vmem_limit_bytes=48 * MIB),
    )(x3, mod, mod, w_cat_t)


def _block_masks(n, blk):
    shift = int(math.log2(blk))
    i = lax.broadcasted_iota(jnp.int32, (n, n), 0)
    j = lax.broadcasted_iota(jnp.int32, (n, n), 1)
    same = (i >> shift) == (j >> shift)
    return i, j, same


def _gdn_body(*refs, blk, chained, tiles_per_seq, dk_scale):
    if chained:
        (qkv_ref, z_ref, ba_ref, cw_ref, gp_ref, ng_ref, o_ref, st_ref, xc_sc, act_sc) = refs
    else:
        (qkv_ref, z_ref, ba_ref, buf_ref, sin_ref, cw_ref, gp_ref, ng_ref, o_ref, st_ref, act_sc) = refs
    lt, cqkv = qkv_ref.shape
    hd = LANES
    nh = cqkv // (3 * hd)
    t = pl.program_id(0)
    halo = SUBLANES
    ngrp = lt // SUBLANES

    if chained:
        first = (t % tiles_per_seq) == 0

        @pl.when(first)
        def _():
            xc_sc[pl.ds(0, halo), :] = jnp.zeros((halo, cqkv), F32)
            st_ref[...] = jnp.zeros_like(st_ref)

        @pl.when(jnp.logical_not(first))
        def _():
            xc_sc[pl.ds(0, halo), :] = xc_sc[pl.ds(lt, halo), :]

        xc_sc[pl.ds(halo, lt), :] = qkv_ref[...]
    else:
        assert blk == SUBLANES

    cblk = 2 * LANES
    row_in_grp = lax.broadcasted_iota(jnp.int32, (ngrp, SUBLANES, cblk), 1)
    for c0 in range(0, cqkv, cblk):
        cs = slice(c0, c0 + cblk)
        if chained:
            xe = xc_sc[:, cs].reshape(ngrp + 1, SUBLANES, cblk)
            cur = xe[1:]
        else:
            cur = qkv_ref[:, cs].reshape(ngrp, SUBLANES, cblk)
            prev = buf_ref[:, cs].reshape(ngrp, SUBLANES, cblk)
        y = None
        for j in range(CONV_W):
            d = CONV_W - 1 - j
            if d == 0:
                term = cur
            elif chained:
                rot = pltpu.roll(xe, d, axis=1)
                term = jnp.where(row_in_grp >= d, rot[1:], rot[:-1])
            else:
                term = jnp.where(row_in_grp >= d, pltpu.roll(cur, d, axis=1), pltpu.roll(prev, d, axis=1))
            term = cw_ref[j:j + 1, cs][None] * term
            y = term if y is None else y + term
        act_sc[:, cs] = _silu(y).reshape(lt, cblk)

    st = SEQ_TILE
    nsub = lt // st
    i, j, same = _block_masks(st, blk)
    incl = same & (i >= j)
    strict = same & (i > j)
    hi = lax.Precision.HIGHEST
    incl_f = incl.astype(F32)
    last_f = (j == (i | (blk - 1))).astype(F32)
    eye = (i == j).astype(F32)
    pair_masks = []
    for lvl in range(int(math.log2(blk))):
        pair_masks.append(((i >> (lvl + 1)) == (j >> (lvl + 1))) & ((i >> lvl) != (j >> lvl)))

    heads = range(nh)
    hp = []
    for s in range(nsub):
        rows = slice(s * st, (s + 1) * st)
        ba = ba_ref[rows, :]
        beta_all = jax.nn.sigmoid(ba)
        g_all = -jnp.exp(gp_ref[0:1, :]) * jax.nn.softplus(ba + gp_ref[1:2, :])
        gc_all = _dot(incl_f, g_all, hi)
        gl_all = _dot(last_f, gc_all, hi)
        gc_t = gc_all.T
        for h in heads:
            q = act_sc[rows, h * hd:(h + 1) * hd]
            k = act_sc[rows, (nh + h) * hd:(nh + h + 1) * hd]
            v = act_sc[rows, (2 * nh + h) * hd:(2 * nh + h + 1) * hd]
            q = q * lax.rsqrt(jnp.sum(q * q, axis=-1, keepdims=True) + RMS_EPS) * dk_scale
            k = k * lax.rsqrt(jnp.sum(k * k, axis=-1, keepdims=True) + RMS_EPS)
            beta = beta_all[:, h:h + 1]
            gcc = gc_all[:, nh + h:nh + h + 1]
            gcr = gc_t[nh + h:nh + h + 1, :]
            glc = gl_all[:, nh + h:nh + h + 1]
            decay = jnp.exp(jnp.where(incl, gcc - gcr, -jnp.inf))
            kb = k * beta
            k16 = k.astype(BF16)
            egc = jnp.exp(gcc)
            hp.append(dict(
                sub=s, head=h, decay=decay, k16=k16, kb16=kb.astype(BF16), q16=q.astype(BF16), glc=glc,
                rhs16=jnp.concatenate([v * beta, kb * egc], axis=1).astype(BF16),
                qd=q * egc, kd=k * jnp.exp(glc - gcc)))

    ms = [_dot_nt(p["kb16"], p["k16"]) * jnp.where(strict, p["decay"], 0.0) for p in hp]
    dinvs = [eye - jnp.where(pair_masks[0], m, 0.0) for m in ms]
    for pm in pair_masks[1:]:
        d16s = [d.astype(BF16) for d in dinvs]
        t16s = [_dot(d16, jnp.where(pm, m, 0.0).astype(BF16)).astype(BF16) for d16, m in zip(d16s, ms)]
        dinvs = [d - _dot(t16, d16) for d, t16, d16 in zip(dinvs, t16s, d16s)]
    xs = [_dot(d.astype(BF16), p["rhs16"]) for d, p in zip(dinvs, hp)]
    qk16s = [(_dot_nt(p["q16"], p["k16"]) * p["decay"]).astype(BF16) for p in hp]

    blk_per_sub = st // blk
    vn_parts = [[] for _ in hp]
    os_parts = [[] for _ in hp]
    for b in range(lt // blk):
        lb = b % blk_per_sub
        rs = slice(lb * blk, (lb + 1) * blk)
        cidx = [(b // blk_per_sub) * nh + h for h in heads]
        s0s = [st_ref[0, h] if chained else sin_ref[b, h] for h in heads]
        rr = [_dot(jnp.concatenate([xs[c][rs, hd:], hp[c]["qd"][rs]], axis=0).astype(BF16), s0.astype(BF16))
              for c, s0 in zip(cidx, s0s)]
        vns = [xs[c][rs, :hd] - r[:blk] for c, r in zip(cidx, rr)]
        for h, c in zip(heads, cidx):
            p = hp[c]
            s_new = (s0s[h] * jnp.exp(p["glc"][lb * blk:lb * blk + 1, :])
                     + _dot_tn(p["kd"][rs].astype(BF16), vns[h].astype(BF16)))
            if chained:
                st_ref[0, h] = s_new
            else:
                st_ref[b, h] = s_new
            vn_parts[c].append(vns[h])
            os_parts[c].append(rr[h][blk:])
    for c, p in enumerate(hp):
        rows = slice(p["sub"] * st, (p["sub"] + 1) * st)
        cols = slice(p["head"] * hd, (p["head"] + 1) * hd)
        vn_all = jnp.concatenate(vn_parts[c], axis=0)
        o = jnp.concatenate(os_parts[c], axis=0) + _dot(qk16s[c], vn_all.astype(BF16))
        o = o * lax.rsqrt(jnp.mean(o * o, axis=-1, keepdims=True) + RMS_EPS) * ng_ref[...]
        o = o * _silu(z_ref[rows, cols])
        o_ref[rows, cols] = o.astype(BF16)


def _gdn(proj, conv_w, gate_par, norm_g, *, n_seq, n_rows, blk, ba_col, conv_buf=None, s0=None):
    t_rows = proj.shape[0]
    nh = DN_HEADS
    hd = LANES
    cqkv = 3 * nh * hd
    chained = s0 is None
    lt = GDN_TILE_CHAINED if chained else SEQ_TILE
    assert lt % SEQ_TILE == 0 and SEQ_TILE % blk == 0 and blk & (blk - 1) == 0 and blk >= 2
    n_tiles = t_rows // lt
    common_in = [
        pl.BlockSpec((lt, cqkv), lambda t: (t, 0)),
        pl.BlockSpec((lt, nh * hd), lambda t: (t, 3)),
        pl.BlockSpec((lt, LANES), lambda t: (t, ba_col)),
    ]
    par_in = [
        pl.BlockSpec(conv_w.shape, lambda t: (0, 0)),
        pl.BlockSpec(gate_par.shape, lambda t: (0, 0)),
        pl.BlockSpec((1, hd), lambda t: (0, 0)),
    ]
    o_spec = pl.BlockSpec((lt, nh * hd), lambda t: (t, 0))
    o_shape = jax.ShapeDtypeStruct((t_rows, nh * hd), BF16)
    scratch = []
    if chained:
        assert n_rows % lt == 0 and blk == DN_CHUNK
        tps = n_rows // lt
        in_specs = common_in + par_in
        args = (proj, proj, proj, conv_w, gate_par, norm_g)
        st_spec = pl.BlockSpec((1, nh, hd, hd), lambda t: (t // tps, 0, 0, 0))
        scratch.append(pltpu.VMEM((lt + SUBLANES, cqkv), F32))
    else:
        assert blk == n_rows and conv_buf is not None
        tps = 0
        spt = lt // n_rows
        in_specs = common_in + [
            pl.BlockSpec((lt, cqkv), lambda t: (t, 0)),
            pl.BlockSpec((spt, nh, hd, hd), lambda t: (t, 0, 0, 0)),
        ] + par_in
        args = (proj, proj, proj, conv_buf, s0, conv_w, gate_par, norm_g)
        st_spec = pl.BlockSpec((spt, nh, hd, hd), lambda t: (t, 0, 0, 0))
    scratch.append(pltpu.VMEM((lt, cqkv), F32))
    return pl.pallas_call(
        functools.partial(_gdn_body, blk=blk, chained=chained, tiles_per_seq=tps,
                          dk_scale=float(hd) ** -0.5),
        name="gdn_chained" if chained else "gdn_stateful",
        grid=(n_tiles,),
        in_specs=in_specs,
        out_specs=[o_spec, st_spec],
        out_shape=[o_shape, jax.ShapeDtypeStruct((n_seq, nh, hd, hd), F32)],
        scratch_shapes=scratch,
        compiler_params=pltpu.CompilerParams(
            dimension_semantics=("arbitrary",), vmem_limit_bytes=56 * MIB),
    )(*args)


def _mlp_body(u_ref, v_ref, ws_ref, bst_ref, lng_ref, lnb_ref, o_ref, *maybe_vr_ref, blk):
    ck = ws_ref.shape[1]
    gd = u_ref.shape[1] // MLP_GROUPS
    i, j, same = _block_masks(ck, blk)
    incl = same & (i >= j)
    hi = lax.Precision.HIGHEST
    if blk != ck:
        rep = (j == (i & (blk - 1))).astype(F32)
        bias_all = _dot(rep, bst_ref[...], hi)
    else:
        bias_all = bst_ref[...]
    for g in range(MLP_GROUPS):
        cs = slice(g * gd, (g + 1) * gd)
        wsp = ws_ref[g]
        if blk != ck:
            wsp = _dot_nt(_dot(rep, wsp, hi), rep, hi)
        wsp = jnp.where(incl, wsp, 0.0).astype(BF16)
        for c in range(u_ref.shape[0] // ck):
            rs = slice(c * ck, (c + 1) * ck)
            uu = jax.nn.gelu(u_ref[rs, cs])
            vv = _layer_norm(jax.nn.gelu(v_ref[rs, cs]), lng_ref[:, cs], lnb_ref[:, cs])
            if maybe_vr_ref:
                maybe_vr_ref[0][rs, cs] = vv
            s = _dot(wsp, vv.astype(BF16)) + bias_all[:, g:g + 1]
            o_ref[rs, cs] = (uu * s).astype(BF16)


def _mlp(proj, w_spatial, b_spatial, ln_g, ln_b, *, blk, want_v_rows):
    t_rows = proj.shape[0]
    lt = MLP_TILE
    ck = w_spatial.shape[1]
    width = ln_g.size
    assert ck == MLP_CHUNK and lt % ck == 0 and ck % blk == 0 and t_rows % lt == 0
    u_col = (4 * DN_HEADS * LANES) // width
    bst = jnp.zeros((ck, LANES), F32).at[:, :MLP_GROUPS].set(b_spatial.T)
    o_spec = pl.BlockSpec((lt, width), lambda t: (t, 0))
    out_specs = [o_spec]
    out_shape = [jax.ShapeDtypeStruct((t_rows, width), BF16)]
    if want_v_rows:
        out_specs.append(o_spec)
        out_shape.append(jax.ShapeDtypeStruct((t_rows, width), F32))
    return pl.pallas_call(
        functools.partial(_mlp_body, blk=blk),
        name="spatial_mlp",
        grid=(t_rows // lt,),
        in_specs=[
            pl.BlockSpec((lt, width), lambda t: (t, u_col)),
            pl.BlockSpec((lt, width), lambda t: (t, u_col + 1)),
            pl.BlockSpec(w_spatial.shape, lambda t: (0, 0, 0)),
            pl.BlockSpec((ck, LANES), lambda t: (0, 0)),
            pl.BlockSpec((1, width), lambda t: (0, 0)),
            pl.BlockSpec((1, width), lambda t: (0, 0)),
        ],
        out_specs=out_specs,
        out_shape=out_shape,
        compiler_params=pltpu.CompilerParams(
            dimension_semantics=("arbitrary",), vmem_limit_bytes=32 * MIB),
    )(proj, proj, w_spatial, bst, ln_g.reshape(1, width), ln_b.reshape(1, width))


def _mix_body(od_ref, om_ref, x_ref, gt_ref, w_ref, lng_ref, lnb_ref, o_ref, w_sc, *, seq_tiles, alpha):
    tm, d = o_ref.shape
    kd = od_ref.shape[1]

    @pl.when(pl.program_id(0) == 0)
    def _():
        w_sc[...] = w_ref[...].astype(BF16)

    sb, rb, _ = x_ref.shape
    rc = min(FFN_RC, tm)
    for c in range(tm // rc):
        rs = slice(c * rc, (c + 1) * rc)
        mix = _dot(od_ref[rs, :], w_sc[:kd, :]) + _dot(om_ref[rs, :], w_sc[kd:, :])
        if sb == 1:
            x = x_ref[:, rs, :]
            gate = _mod_rows(gt_ref, seq_tiles)[:, None, :]
        else:
            ss = slice(c * (rc // rb), (c + 1) * (rc // rb))
            x = x_ref[ss]
            gate = gt_ref[ss, :][:, None, :]
        y = alpha * x + gate * mix.reshape(x.shape)
        o_ref[rs, :] = _layer_norm(y.reshape(rc, d), lng_ref[...], lnb_ref[...])


def _mix(o_dn, o_mlp, x3, mod, sub, w_out, layer, ln_g, ln_b, alpha):
    n_seq, n_rows, d = x3.shape
    grp = _Group(n_seq, n_rows, MIX_TM)
    tm = grp.tm
    out = pl.pallas_call(
        functools.partial(_mix_body, seq_tiles=grp.seq_tiles, alpha=alpha),
        name="out_mix_ln",
        grid=(grp.n_tiles,),
        in_specs=[
            pl.BlockSpec((tm, o_dn.shape[1]), lambda m: (m, 0)),
            pl.BlockSpec((tm, o_mlp.shape[1]), lambda m: (m, 0)),
            grp.x_spec(d, 1),
            grp.mod_spec(mod, d, sub * 3 + 2, 1),
            pl.BlockSpec((None,) + w_out.shape[1:], lambda m: (layer, 0, 0), pipeline_mode=pl.Buffered(1)),
            pl.BlockSpec((1, d), lambda m: (0, 0)),
            pl.BlockSpec((1, d), lambda m: (0, 0)),
        ],
        out_specs=pl.BlockSpec((tm, d), lambda m: (m, 0)),
        out_shape=jax.ShapeDtypeStruct((n_seq * n_rows, d), F32),
        scratch_shapes=[pltpu.VMEM(w_out.shape[1:], BF16)],
        compiler_params=pltpu.CompilerParams(
            dimension_semantics=("arbitrary",), vmem_limit_bytes=56 * MIB),
    )(o_dn, o_mlp, x3, mod, w_out, ln_g.reshape(1, d), ln_b.reshape(1, d))
    return out.reshape(n_seq, n_rows, d)


def _trunk_layer(x3, mod, wts, layer, alpha, conv_buf, s0, ffn_weights, emit_w16):
    n_seq, n_rows, d = x3.shape
    qkv_dim = 3 * DN_HEADS * LANES
    w16 = []

    def ffn(x3, sub, slot):
        r = _ffn(x3, mod, sub, ffn_weights[slot], wts["ln_g"][sub], wts["ln_b"][sub], alpha, emit_w16)
        if emit_w16:
            w16.append(r[1])
            return r[0]
        return r

    x3 = ffn(x3, 0, 0)
    proj = _proj(x3, mod, 1, wts["w_cat"])
    dn_blk = DN_CHUNK if n_rows % DN_CHUNK == 0 else n_rows
    gdn_kw = dict(n_seq=n_seq, n_rows=n_rows, blk=dn_blk, ba_col=wts["ba_col"])
    if s0 is None:
        o_dn, s_new = _gdn(proj, wts["conv_w"], wts["gate_par"], wts["dn_norm_g"], **gdn_kw)
    else:
        pad = jnp.zeros((n_seq, SUBLANES - (CONV_W - 1), qkv_dim), F32)
        buf = jnp.concatenate([pad, conv_buf], axis=1).reshape(n_seq * SUBLANES, qkv_dim)
        assert n_rows == SUBLANES
        o_dn, s_new = _gdn(proj, wts["conv_w"], wts["gate_par"], wts["dn_norm_g"],
                           conv_buf=buf, s0=s0, **gdn_kw)
    mlp_out = _mlp(proj, wts["w_spatial"], wts["b_spatial"], wts["mlp_ln_g"], wts["mlp_ln_b"],
                   blk=min(n_rows, MLP_CHUNK), want_v_rows=s0 is not None)
    o_mlp = mlp_out[0]
    v_rows = mlp_out[1].reshape(n_seq, n_rows, -1) if s0 is not None else None
    x3 = _mix(o_dn, o_mlp, x3, mod, 1, wts["w_out"], layer, wts["ln_g"][1], wts["ln_b"][1], alpha)
    x3 = ffn(x3, 2, 1)
    new_buf = proj.reshape(n_seq, n_rows, -1)[:, n_rows - (CONV_W - 1):, :qkv_dim]
    return x3, new_buf, s_new, v_rows, w16


def kernel(x_prompt, x_sample, c_prompt, c_sample, state_delta, state_conv, w_ada, b_ada, ln_g, ln_b,
           ffn_wg, ffn_wu, ffn_wd, w_in, conv_w, a_log, dt_bias, dn_norm_g, mlp_ln_g, mlp_ln_b,
           w_spatial, b_spatial, w_out):
    depth = w_ada.shape[0]
    bp, _, d = x_prompt.shape
    bs = x_sample.shape[0]
    nh = DN_HEADS
    alpha = (2.0 * depth) ** 0.25
    qkvz = 4 * nh * LANES
    gates = 2 * nh

    pad_rows = (-(bs + bp)) % SUBLANES
    c_all = jnp.concatenate([c_sample, c_prompt, jnp.zeros((pad_rows, d), F32)], axis=0)
    w_in_t = jnp.swapaxes(w_in, 1, 2)

    y_p, y_s = x_prompt, x_sample
    delta_p, conv_p, delta_s, conv_s, vrows_s = [], [], [], [], []
    for layer in range(depth):
        mod_s, mod_p = _ada(c_all, w_ada[layer], b_ada[layer], bs)
        gate_par = jnp.zeros((SUBLANES, LANES), F32)
        gate_par = gate_par.at[0, nh:gates].set(a_log[layer]).at[1, nh:gates].set(dt_bias[layer])
        wts = dict(
            ln_g=ln_g[layer], ln_b=ln_b[layer],
            w_cat=_wprep(w_in_t, layer, qkvz, gates), ba_col=(w_in.shape[-1] - gates) // LANES,
            conv_w=conv_w[layer], gate_par=gate_par,
            dn_norm_g=dn_norm_g[layer].reshape(1, LANES), mlp_ln_g=mlp_ln_g[layer], mlp_ln_b=mlp_ln_b[layer],
            w_spatial=w_spatial[layer], b_spatial=b_spatial[layer], w_out=w_out,
        )
        w32 = [(ffn_wg, ffn_wu, ffn_wd, layer, slot) for slot in range(2)]
        y_s, cb_s, ds_s, vr_s, w16 = _trunk_layer(y_s, mod_s, wts, layer, alpha, state_conv[layer],
                                                  state_delta[layer], w32, True)
        y_p, cb_p, ds_p, _, _ = _trunk_layer(y_p, mod_p, wts, layer, alpha, None, None, w16, False)
        delta_p.append(ds_p)
        conv_p.append(cb_p)
        delta_s.append(ds_s)
        conv_s.append(cb_s)
        vrows_s.append(vr_s)
    return (y_p, y_s, jnp.stack(delta_p), jnp.stack(conv_p), jnp.stack(delta_s), jnp.stack(conv_s),
            jnp.stack(vrows_s))
```

```python
import functools
import math

import jax
import jax.numpy as jnp
from jax import lax
from jax.experimental import pallas as pl
from jax.experimental.pallas import tpu as pltpu

F32 = jnp.float32
BF16 = jnp.bfloat16

DN_HEADS = 8
DN_CHUNK = 64
CONV_W = 4
MLP_GROUPS = 4
MLP_CHUNK = 128
N_SUB = 3
LN_EPS = 1e-5
RMS_EPS = 1e-6

LANES = 128
SUBLANES = 8
MIB = 2 ** 20

FFN_TM = 1024
FFN_TF = 256
FFN_TF16 = 512
FFN_RC = 256
FFN_VMEM_LIMIT = 58 * MIB
PROJ_TN = 1280
PROJ_N = 6400
PREP_TC = 256
MIX_TM = 512
ADA_TN = 1024
SEQ_TILE = 128
GDN_TILE_CHAINED = 256
MLP_TILE = 512


def _dot(a, b, precision=None):
    return jnp.dot(a, b, preferred_element_type=F32, precision=precision)


def _dot_nt(a, b, precision=None):
    return lax.dot_general(a, b, (((1,), (1,)), ((), ())),
                           preferred_element_type=F32, precision=precision)


def _dot_tn(a, b):
    return lax.dot_general(a, b, (((0,), (0,)), ((), ())), preferred_element_type=F32)


def _layer_norm(y, g, b):
    mu = jnp.mean(y, axis=-1, keepdims=True)
    yc = y - mu
    var = jnp.mean(yc * yc, axis=-1, keepdims=True)
    return yc * lax.rsqrt(var + LN_EPS) * g + b


def _silu(x):
    return x * jax.nn.sigmoid(x)


def _mod_rows(ref, seq_tiles):
    if seq_tiles:
        return ref[pl.ds(pl.program_id(0) // seq_tiles, 1), :]
    return ref[...]


def _ada_body(c_ref, w_ref, b_ref, ms_ref, mp_ref):
    c = c_ref[...]
    a = _silu(c).astype(BF16)
    y = _dot(a, w_ref[...].astype(BF16)) + b_ref[...]
    ns = ms_ref.shape[0]
    ms_ref[...] = y[:ns]
    mp_ref[...] = y[ns:]


def _ada(c_all, w_ada, b_ada, n_sample):
    rows, d = c_all.shape
    n = w_ada.shape[1]
    return pl.pallas_call(
        _ada_body,
        name="ada_mod",
        grid=(n // ADA_TN,),
        in_specs=[
            pl.BlockSpec((rows, d), lambda j: (0, 0)),
            pl.BlockSpec((d, ADA_TN), lambda j: (0, j)),
            pl.BlockSpec((1, ADA_TN), lambda j: (0, j)),
        ],
        out_specs=[
            pl.BlockSpec((n_sample, ADA_TN), lambda j: (0, j)),
            pl.BlockSpec((rows - n_sample, ADA_TN), lambda j: (0, j)),
        ],
        out_shape=[
            jax.ShapeDtypeStruct((n_sample, n), F32),
            jax.ShapeDtypeStruct((rows - n_sample, n), F32),
        ],
        compiler_params=pltpu.CompilerParams(
            dimension_semantics=("arbitrary",), vmem_limit_bytes=40 * MIB),
    )(c_all, w_ada, b_ada.reshape(1, n))


class _Group:
    def __init__(self, n_seq, n_rows, tm):
        if n_rows >= tm:
            assert n_rows % tm == 0
            self.sb, self.rb = 1, tm
        else:
            assert tm % n_rows == 0 and n_seq % (tm // n_rows) == 0
            self.sb, self.rb = tm // n_rows, n_rows
        self.n_seq, self.n_rows = n_seq, n_rows
        self.tiles_r = n_rows // self.rb
        self.tm = self.sb * self.rb
        self.n_tiles = (n_seq // self.sb) * self.tiles_r
        self.seq_tiles = self.tiles_r if self.sb == 1 else 0

    def x_spec(self, d, ngrid, **kw):
        tr = self.tiles_r
        if ngrid == 2:
            return pl.BlockSpec((self.sb, self.rb, d), lambda m, f: (m // tr, m % tr, 0), **kw)
        return pl.BlockSpec((self.sb, self.rb, d), lambda m: (m // tr, m % tr, 0), **kw)

    def mod_spec(self, mod, d, col, ngrid):
        if self.sb == 1:
            rows, row_blk = mod.shape[0], (lambda m: 0)
        else:
            rows, row_blk = self.sb, (lambda m: m)
        if ngrid == 2:
            return pl.BlockSpec((rows, d), lambda m, f: (row_blk(m), col))
        return pl.BlockSpec((rows, d), lambda m: (row_blk(m), col))


def _ffn_body(x_ref, sh_ref, sc_ref, gt_ref, wg_ref, wu_ref, wd_ref, lng_ref, lnb_ref,
              o_ref, *rest, seq_tiles, alpha):
    w16_refs, h_sc = rest[:-1], rest[-1]
    f = pl.program_id(1)
    nf = pl.num_programs(1)
    tm, d = o_ref.shape
    sb, rb, _ = x_ref.shape
    rc = min(FFN_RC, tm)
    n_chunks = tm // rc
    if w16_refs:
        for src, dst in zip((wg_ref, wu_ref, wd_ref), w16_refs):
            dst[...] = src[...].astype(BF16)
        wg_ref, wu_ref, wd_ref = w16_refs

    def x_chunk(c):
        if sb == 1:
            return x_ref[:, c * rc:(c + 1) * rc, :]
        return x_ref[c * (rc // rb):(c + 1) * (rc // rb)]

    def mod_chunk(ref, c):
        if seq_tiles:
            return _mod_rows(ref, seq_tiles)
        return ref[c * (rc // rb):(c + 1) * (rc // rb), :]

    def weights():
        return wg_ref[...].astype(BF16), wu_ref[...].astype(BF16), wd_ref[...].astype(BF16)

    def gate_up(h16, wg16, wu16):
        return (_silu(_dot(h16, wg16)) * _dot(h16, wu16)).astype(BF16)

    @pl.when(f == 0)
    def _():
        wg16, wu16, wd16 = weights()
        for c in range(n_chunks):
            rs = slice(c * rc, (c + 1) * rc)
            x = x_chunk(c)
            h = x * (1.0 + mod_chunk(sc_ref, c)[:, None, :]) + mod_chunk(sh_ref, c)[:, None, :]
            h16 = h.reshape(rc, d).astype(BF16)
            h_sc[rs, :] = h16
            o_ref[rs, :] = _dot(gate_up(h16, wg16, wu16), wd16)

    @pl.when(jnp.logical_and(f > 0, f < nf - 1))
    def _():
        wg16, wu16, wd16 = weights()
        o_ref[...] += _dot(gate_up(h_sc[...], wg16, wu16), wd16)

    @pl.when(f == nf - 1)
    def _():
        wg16, wu16, wd16 = weights()
        for c in range(n_chunks):
            rs = slice(c * rc, (c + 1) * rc)
            acc = o_ref[rs, :] + _dot(gate_up(h_sc[rs, :], wg16, wu16), wd16)
            x = x_chunk(c)
            gate = mod_chunk(gt_ref, c)[:, None, :]
            y = alpha * x + (0.5 * gate) * acc.reshape(x.shape)
            o_ref[rs, :] = _layer_norm(y.reshape(rc, d), lng_ref[...], lnb_ref[...])


def _ffn(x3, mod, sub, weights, ln_g, ln_b, alpha, emit_w16=False):
    n_seq, n_rows, d = x3.shape
    grp = _Group(n_seq, n_rows, FFN_TM)
    tm = grp.tm
    rc = min(FFN_RC, tm)
    if len(weights) == 5:
        wg, wu, wd, layer, slot = weights
        tf = FFN_TF
        w_specs = [
            pl.BlockSpec((None, None, d, tf), lambda m, f: (layer, slot, 0, f)),
            pl.BlockSpec((None, None, d, tf), lambda m, f: (layer, slot, 0, f)),
            pl.BlockSpec((None, None, tf, d), lambda m, f: (layer, slot, f, 0)),
        ]
    else:
        assert not emit_w16
        wg, wu, wd = weights
        tf = FFN_TF16
        w_specs = [
            pl.BlockSpec((d, tf), lambda m, f: (0, f)),
            pl.BlockSpec((d, tf), lambda m, f: (0, f)),
            pl.BlockSpec((tf, d), lambda m, f: (f, 0)),
        ]
    dff = wg.shape[-1]
    assert dff // tf >= 2 and tm % rc == 0 and (grp.rb % rc == 0 if grp.sb == 1 else rc % grp.rb == 0)
    out_specs = [pl.BlockSpec((tm, d), lambda m, f: (m, 0))]
    out_shape = [jax.ShapeDtypeStruct((n_seq * n_rows, d), F32)]
    if emit_w16:
        assert grp.n_tiles == 1
        out_specs += [
            pl.BlockSpec((d, tf), lambda m, f: (0, f)),
            pl.BlockSpec((d, tf), lambda m, f: (0, f)),
            pl.BlockSpec((tf, d), lambda m, f: (f, 0)),
        ]
        out_shape += [
            jax.ShapeDtypeStruct((d, dff), BF16),
            jax.ShapeDtypeStruct((d, dff), BF16),
            jax.ShapeDtypeStruct((dff, d), BF16),
        ]
    outs = pl.pallas_call(
        functools.partial(_ffn_body, seq_tiles=grp.seq_tiles, alpha=alpha),
        name="swiglu_ln",
        grid=(grp.n_tiles, dff // tf),
        in_specs=[
            grp.x_spec(d, 2),
            grp.mod_spec(mod, d, sub * 3 + 0, 2),
            grp.mod_spec(mod, d, sub * 3 + 1, 2),
            grp.mod_spec(mod, d, sub * 3 + 2, 2),
            *w_specs,
            pl.BlockSpec((1, d), lambda m, f: (0, 0)),
            pl.BlockSpec((1, d), lambda m, f: (0, 0)),
        ],
        out_specs=out_specs,
        out_shape=out_shape,
        scratch_shapes=[pltpu.VMEM((tm, d), BF16)],
        compiler_params=pltpu.CompilerParams(
            dimension_semantics=("arbitrary", "arbitrary"), vmem_limit_bytes=FFN_VMEM_LIMIT),
    )(x3, mod, mod, mod, wg, wu, wd, ln_g.reshape(1, d), ln_b.reshape(1, d))
    y = outs[0].reshape(n_seq, n_rows, d)
    return (y, tuple(outs[1:])) if emit_w16 else y


def _wprep_body(w_ref, o_ref, *, qkvz, gates):
    n_in, cols = w_ref.shape
    n_out = o_ref.shape[0]
    mlp = n_in - qkvz - gates
    o_ref[0:qkvz, :] = w_ref[0:qkvz, :].astype(BF16)
    o_ref[qkvz:qkvz + mlp, :] = w_ref[qkvz + gates:n_in, :].astype(BF16)
    o_ref[qkvz + mlp:n_in, :] = w_ref[qkvz:qkvz + gates, :].astype(BF16)
    o_ref[n_in:n_out, :] = jnp.zeros((n_out - n_in, cols), BF16)


def _wprep(w_in_t, layer, qkvz, gates):
    _, n_in, d = w_in_t.shape
    assert n_in <= PROJ_N and (n_in - gates) % LANES == 0
    return pl.pallas_call(
        functools.partial(_wprep_body, qkvz=qkvz, gates=gates),
        name="proj_weight_prep",
        grid=(d // PREP_TC,),
        in_specs=[pl.BlockSpec((None, n_in, PREP_TC), lambda i: (layer, 0, i))],
        out_specs=pl.BlockSpec((PROJ_N, PREP_TC), lambda i: (0, i)),
        out_shape=jax.ShapeDtypeStruct((PROJ_N, d), BF16),
        compiler_params=pltpu.CompilerParams(
            dimension_semantics=("arbitrary",), vmem_limit_bytes=40 * MIB),
    )(w_in_t)


def _proj_body(x_ref, sh_ref, sc_ref, w_ref, o_ref, h_sc, *, seq_tiles):
    tm, d = h_sc.shape

    @pl.when(pl.program_id(1) == 0)
    def _():
        x = x_ref[...]
        h = x * (1.0 + _mod_rows(sc_ref, seq_tiles)[:, None, :]) + _mod_rows(sh_ref, seq_tiles)[:, None, :]
        h_sc[...] = h.reshape(tm, d).astype(BF16)

    o_ref[...] = _dot_nt(h_sc[...], w_ref[...])


def _proj(x3, mod, sub, w_cat_t):
    n_seq, n_rows, d = x3.shape
    n = w_cat_t.shape[0]
    grp = _Group(n_seq, n_rows, FFN_TM)
    tm = grp.tm
    return pl.pallas_call(
        functools.partial(_proj_body, seq_tiles=grp.seq_tiles),
        name="in_proj",
        grid=(grp.n_tiles, n // PROJ_TN),
        in_specs=[
            grp.x_spec(d, 2),
            grp.mod_spec(mod, d, sub * 3 + 0, 2),
            grp.mod_spec(mod, d, sub * 3 + 1, 2),
            pl.BlockSpec((PROJ_TN, d), lambda m, j: (j, 0)),
        ],
        out_specs=pl.BlockSpec((tm, PROJ_TN), lambda m, j: (m, j)),
        out_shape=jax.ShapeDtypeStruct((n_seq * n_rows, n), F32),
        scratch_shapes=[pltpu.VMEM((tm, d), BF16)],
        compiler_params=pltpu.CompilerParams(
            dimension_semantics=("arbitrary", "arbitrary"), vmem_limit_bytes=48 * MIB),
    )(x3, mod, mod, w_cat_t)


def _block_masks(n, blk):
    shift = int(math.log2(blk))
    i = lax.broadcasted_iota(jnp.int32, (n, n), 0)
    j = lax.broadcasted_iota(jnp.int32, (n, n), 1)
    same = (i >> shift) == (j >> shift)
    return i, j, same


def _gdn_body(*refs, blk, chained, tiles_per_seq, dk_scale):
    if chained:
        (qkv_ref, z_ref, ba_ref, cw_ref, gp_ref, ng_ref, o_ref, st_ref, xc_sc, act_sc) = refs
    else:
        (qkv_ref, z_ref, ba_ref, buf_ref, sin_ref, cw_ref, gp_ref, ng_ref, o_ref, st_ref, nbuf_ref,
         prev_sc, act_sc) = refs
    lt, cqkv = qkv_ref.shape
    hd = LANES
    nh = cqkv // (3 * hd)
    t = pl.program_id(0)
    halo = SUBLANES
    ngrp = lt // SUBLANES

    if chained:
        first = (t % tiles_per_seq) == 0

        @pl.when(first)
        def _():
            xc_sc[pl.ds(0, halo), :] = jnp.zeros((halo, cqkv), F32)
            st_ref[...] = jnp.zeros_like(st_ref)

        @pl.when(jnp.logical_not(first))
        def _():
            xc_sc[pl.ds(0, halo), :] = xc_sc[pl.ds(lt, halo), :]

        xc_sc[pl.ds(halo, lt), :] = qkv_ref[...]
    else:
        assert blk == SUBLANES
        keep = buf_ref.shape[0]

        @pl.when(t == 0)
        def _():
            prev_sc[...] = jnp.zeros_like(prev_sc)

        for s in range(ngrp):
            for r in range(keep):
                row = s * SUBLANES + SUBLANES - keep + r
                prev_sc[row:row + 1, :] = buf_ref[r, s:s + 1, :]
                nbuf_ref[r, s:s + 1, :] = qkv_ref[row:row + 1, :]

    cblk = 2 * LANES
    row_in_grp = lax.broadcasted_iota(jnp.int32, (ngrp, SUBLANES, cblk), 1)
    for c0 in range(0, cqkv, cblk):
        cs = slice(c0, c0 + cblk)
        if chained:
            xe = xc_sc[:, cs].reshape(ngrp + 1, SUBLANES, cblk)
            cur = xe[1:]
        else:
            cur = qkv_ref[:, cs].reshape(ngrp, SUBLANES, cblk)
            prev = prev_sc[:, cs].reshape(ngrp, SUBLANES, cblk)
        y = None
        for j in range(CONV_W):
            d = CONV_W - 1 - j
            if d == 0:
                term = cur
            elif chained:
                rot = pltpu.roll(xe, d, axis=1)
                term = jnp.where(row_in_grp >= d, rot[1:], rot[:-1])
            else:
                term = jnp.where(row_in_grp >= d, pltpu.roll(cur, d, axis=1), pltpu.roll(prev, d, axis=1))
            term = cw_ref[j:j + 1, cs][None] * term
            y = term if y is None else y + term
        act_sc[:, cs] = _silu(y).reshape(lt, cblk)

    st = SEQ_TILE
    nsub = lt // st
    i, j, same = _block_masks(st, blk)
    incl = same & (i >= j)
    strict = same & (i > j)
    hi = lax.Precision.HIGHEST
    incl_f = incl.astype(F32)
    last_f = (j == (i | (blk - 1))).astype(F32)
    eye = (i == j).astype(F32)
    pair_masks = []
    for lvl in range(int(math.log2(blk))):
        pair_masks.append(((i >> (lvl + 1)) == (j >> (lvl + 1))) & ((i >> lvl) != (j >> lvl)))

    heads = range(nh)
    hp = []
    for s in range(nsub):
        rows = slice(s * st, (s + 1) * st)
        ba = ba_ref[rows, :]
        beta_all = jax.nn.sigmoid(ba)
        g_all = -jnp.exp(gp_ref[0:1, :]) * jax.nn.softplus(ba + gp_ref[1:2, :])
        gc_all = _dot(incl_f, g_all, hi)
        gl_all = _dot(last_f, gc_all, hi)
        gc_t = gc_all.T
        for h in heads:
            q = act_sc[rows, h * hd:(h + 1) * hd]
            k = act_sc[rows, (nh + h) * hd:(nh + h + 1) * hd]
            v = act_sc[rows, (2 * nh + h) * hd:(2 * nh + h + 1) * hd]
            q = q * lax.rsqrt(jnp.sum(q * q, axis=-1, keepdims=True) + RMS_EPS) * dk_scale
            k = k * lax.rsqrt(jnp.sum(k * k, axis=-1, keepdims=True) + RMS_EPS)
            beta = beta_all[:, h:h + 1]
            gcc = gc_all[:, nh + h:nh + h + 1]
            gcr = gc_t[nh + h:nh + h + 1, :]
            glc = gl_all[:, nh + h:nh + h + 1]
            decay = jnp.exp(jnp.where(incl, gcc - gcr, -jnp.inf))
            kb = k * beta
            k16 = k.astype(BF16)
            egc = jnp.exp(gcc)
            hp.append(dict(
                sub=s, head=h, decay=decay, k16=k16, kb16=kb.astype(BF16), q16=q.astype(BF16), glc=glc,
                rhs16=jnp.concatenate([v * beta, kb * egc], axis=1).astype(BF16),
                qd=q * egc, kd=k * jnp.exp(glc - gcc)))

    ms = [_dot_nt(p["kb16"], p["k16"]) * jnp.where(strict, p["decay"], 0.0) for p in hp]
    dinvs = [eye - jnp.where(pair_masks[0], m, 0.0) for m in ms]
    for pm in pair_masks[1:]:
        d16s = [d.astype(BF16) for d in dinvs]
        t16s = [_dot(d16, jnp.where(pm, m, 0.0).astype(BF16)).astype(BF16) for d16, m in zip(d16s, ms)]
        dinvs = [d - _dot(t16, d16) for d, t16, d16 in zip(dinvs, t16s, d16s)]
    xs = [_dot(d.astype(BF16), p["rhs16"]) for d, p in zip(dinvs, hp)]
    qk16s = [(_dot_nt(p["q16"], p["k16"]) * p["decay"]).astype(BF16) for p in hp]

    blk_per_sub = st // blk
    vn_parts = [[] for _ in hp]
    os_parts = [[] for _ in hp]
    for b in range(lt // blk):
        lb = b % blk_per_sub
        rs = slice(lb * blk, (lb + 1) * blk)
        cidx = [(b // blk_per_sub) * nh + h for h in heads]
        s0s = [st_ref[0, h] if chained else sin_ref[b, h] for h in heads]
        rr = [_dot(jnp.concatenate([xs[c][rs, hd:], hp[c]["qd"][rs]], axis=0).astype(BF16), s0.astype(BF16))
              for c, s0 in zip(cidx, s0s)]
        vns = [xs[c][rs, :hd] - r[:blk] for c, r in zip(cidx, rr)]
        for h, c in zip(heads, cidx):
            p = hp[c]
            s_new = (s0s[h] * jnp.exp(p["glc"][lb * blk:lb * blk + 1, :])
                     + _dot_tn(p["kd"][rs].astype(BF16), vns[h].astype(BF16)))
            if chained:
                st_ref[0, h] = s_new
            else:
                st_ref[b, h] = s_new
            vn_parts[c].append(vns[h])
            os_parts[c].append(rr[h][blk:])
    for c, p in enumerate(hp):
        rows = slice(p["sub"] * st, (p["sub"] + 1) * st)
        cols = slice(p["head"] * hd, (p["head"] + 1) * hd)
        vn_all = jnp.concatenate(vn_parts[c], axis=0)
        o = jnp.concatenate(os_parts[c], axis=0) + _dot(qk16s[c], vn_all.astype(BF16))
        o = o * lax.rsqrt(jnp.mean(o * o, axis=-1, keepdims=True) + RMS_EPS) * ng_ref[...]
        o = o * _silu(z_ref[rows, cols])
        o_ref[rows, cols] = o.astype(BF16)


def _gdn(proj, conv_w, gate_par, norm_g, *, n_seq, n_rows, blk, ba_col, conv_buf=None, s0=None):
    t_rows = proj.shape[0]
    nh = DN_HEADS
    hd = LANES
    cqkv = 3 * nh * hd
    chained = s0 is None
    lt = GDN_TILE_CHAINED if chained else SEQ_TILE
    assert lt % SEQ_TILE == 0 and SEQ_TILE % blk == 0 and blk & (blk - 1) == 0 and blk >= 2
    n_tiles = t_rows // lt
    common_in = [
        pl.BlockSpec((lt, cqkv), lambda t: (t, 0)),
        pl.BlockSpec((lt, nh * hd), lambda t: (t, 3)),
        pl.BlockSpec((lt, LANES), lambda t: (t, ba_col)),
    ]
    par_in = [
        pl.BlockSpec(conv_w.shape, lambda t: (0, 0)),
        pl.BlockSpec(gate_par.shape, lambda t: (0, 0)),
        pl.BlockSpec((1, hd), lambda t: (0, 0)),
    ]
    o_spec = pl.BlockSpec((lt, nh * hd), lambda t: (t, 0))
    o_shape = jax.ShapeDtypeStruct((t_rows, nh * hd), BF16)
    scratch, extra_specs, extra_shapes = [], [], []
    if chained:
        assert n_rows % lt == 0 and blk == DN_CHUNK
        tps = n_rows // lt
        in_specs = common_in + par_in
        args = (proj, proj, proj, conv_w, gate_par, norm_g)
        st_spec = pl.BlockSpec((1, nh, hd, hd), lambda t: (t // tps, 0, 0, 0))
        scratch.append(pltpu.VMEM((lt + SUBLANES, cqkv), F32))
    else:
        assert blk == n_rows and conv_buf is not None and conv_buf.shape[1:] == (n_seq, cqkv)
        tps = 0
        spt = lt // n_rows
        buf_spec = pl.BlockSpec((conv_buf.shape[0], spt, cqkv), lambda t: (0, t, 0))
        in_specs = common_in + [
            buf_spec,
            pl.BlockSpec((spt, nh, hd, hd), lambda t: (t, 0, 0, 0)),
        ] + par_in
        args = (proj, proj, proj, conv_buf, s0, conv_w, gate_par, norm_g)
        st_spec = pl.BlockSpec((spt, nh, hd, hd), lambda t: (t, 0, 0, 0))
        extra_specs.append(buf_spec)
        extra_shapes.append(jax.ShapeDtypeStruct(conv_buf.shape, F32))
        scratch.append(pltpu.VMEM((lt, cqkv), F32))
    scratch.append(pltpu.VMEM((lt, cqkv), F32))
    return pl.pallas_call(
        functools.partial(_gdn_body, blk=blk, chained=chained, tiles_per_seq=tps,
                          dk_scale=float(hd) ** -0.5),
        name="gdn_chained" if chained else "gdn_stateful",
        grid=(n_tiles,),
        in_specs=in_specs,
        out_specs=[o_spec, st_spec] + extra_specs,
        out_shape=[o_shape, jax.ShapeDtypeStruct((n_seq, nh, hd, hd), F32)] + extra_shapes,
        scratch_shapes=scratch,
        compiler_params=pltpu.CompilerParams(
            dimension_semantics=("arbitrary",), vmem_limit_bytes=56 * MIB),
    )(*args)


def _mlp_body(u_ref, v_ref, ws_ref, bst_ref, lng_ref, lnb_ref, o_ref, *maybe_vr_ref, blk):
    ck = ws_ref.shape[1]
    gd = u_ref.shape[1] // MLP_GROUPS
    i, j, same = _block_masks(ck, blk)
    incl = same & (i >= j)
    hi = lax.Precision.HIGHEST
    if blk != ck:
        rep = (j == (i & (blk - 1))).astype(F32)
        bias_all = _dot(rep, bst_ref[...], hi)
    else:
        bias_all = bst_ref[...]
    for g in range(MLP_GROUPS):
        cs = slice(g * gd, (g + 1) * gd)
        wsp = ws_ref[g]
        if blk != ck:
            wsp = _dot_nt(_dot(rep, wsp, hi), rep, hi)
        wsp = jnp.where(incl, wsp, 0.0).astype(BF16)
        for c in range(u_ref.shape[0] // ck):
            rs = slice(c * ck, (c + 1) * ck)
            uu = jax.nn.gelu(u_ref[rs, cs])
            vv = _layer_norm(jax.nn.gelu(v_ref[rs, cs]), lng_ref[:, cs], lnb_ref[:, cs])
            if maybe_vr_ref:
                maybe_vr_ref[0][rs, cs] = vv
            s = _dot(wsp, vv.astype(BF16)) + bias_all[:, g:g + 1]
            o_ref[rs, cs] = (uu * s).astype(BF16)


def _mlp(proj, w_spatial, b_spatial, ln_g, ln_b, *, blk, want_v_rows):
    t_rows = proj.shape[0]
    lt = MLP_TILE
    ck = w_spatial.shape[1]
    width = ln_g.size
    assert ck == MLP_CHUNK and lt % ck == 0 and ck % blk == 0 and t_rows % lt == 0
    u_col = (4 * DN_HEADS * LANES) // width
    bst = jnp.zeros((ck, LANES), F32).at[:, :MLP_GROUPS].set(b_spatial.T)
    o_spec = pl.BlockSpec((lt, width), lambda t: (t, 0))
    out_specs = [o_spec]
    out_shape = [jax.ShapeDtypeStruct((t_rows, width), BF16)]
    if want_v_rows:
        out_specs.append(o_spec)
        out_shape.append(jax.ShapeDtypeStruct((t_rows, width), F32))
    return pl.pallas_call(
        functools.partial(_mlp_body, blk=blk),
        name="spatial_mlp",
        grid=(t_rows // lt,),
        in_specs=[
            pl.BlockSpec((lt, width), lambda t: (t, u_col)),
            pl.BlockSpec((lt, width), lambda t: (t, u_col + 1)),
            pl.BlockSpec(w_spatial.shape, lambda t: (0, 0, 0)),
            pl.BlockSpec((ck, LANES), lambda t: (0, 0)),
            pl.BlockSpec((1, width), lambda t: (0, 0)),
            pl.BlockSpec((1, width), lambda t: (0, 0)),
        ],
        out_specs=out_specs,
        out_shape=out_shape,
        compiler_params=pltpu.CompilerParams(
            dimension_semantics=("arbitrary",), vmem_limit_bytes=32 * MIB),
    )(proj, proj, w_spatial, bst, ln_g.reshape(1, width), ln_b.reshape(1, width))


def _mix_body(od_ref, om_ref, x_ref, gt_ref, w_ref, lng_ref, lnb_ref, o_ref, w_sc, *, seq_tiles, alpha):
    tm, d = o_ref.shape
    kd = od_ref.shape[1]

    @pl.when(pl.program_id(0) == 0)
    def _():
        w_sc[...] = w_ref[...].astype(BF16)

    sb, rb, _ = x_ref.shape
    rc = min(FFN_RC, tm)
    for c in range(tm // rc):
        rs = slice(c * rc, (c + 1) * rc)
        mix = _dot(od_ref[rs, :], w_sc[:kd, :]) + _dot(om_ref[rs, :], w_sc[kd:, :])
        if sb == 1:
            x = x_ref[:, rs, :]
            gate = _mod_rows(gt_ref, seq_tiles)[:, None, :]
        else:
            ss = slice(c * (rc // rb), (c + 1) * (rc // rb))
            x = x_ref[ss]
            gate = gt_ref[ss, :][:, None, :]
        y = alpha * x + gate * mix.reshape(x.shape)
        o_ref[rs, :] = _layer_norm(y.reshape(rc, d), lng_ref[...], lnb_ref[...])


def _mix(o_dn, o_mlp, x3, mod, sub, w_out, layer, ln_g, ln_b, alpha):
    n_seq, n_rows, d = x3.shape
    grp = _Group(n_seq, n_rows, MIX_TM)
    tm = grp.tm
    out = pl.pallas_call(
        functools.partial(_mix_body, seq_tiles=grp.seq_tiles, alpha=alpha),
        name="out_mix_ln",
        grid=(grp.n_tiles,),
        in_specs=[
            pl.BlockSpec((tm, o_dn.shape[1]), lambda m: (m, 0)),
            pl.BlockSpec((tm, o_mlp.shape[1]), lambda m: (m, 0)),
            grp.x_spec(d, 1),
            grp.mod_spec(mod, d, sub * 3 + 2, 1),
            pl.BlockSpec((None,) + w_out.shape[1:], lambda m: (layer, 0, 0), pipeline_mode=pl.Buffered(1)),
            pl.BlockSpec((1, d), lambda m: (0, 0)),
            pl.BlockSpec((1, d), lambda m: (0, 0)),
        ],
        out_specs=pl.BlockSpec((tm, d), lambda m: (m, 0)),
        out_shape=jax.ShapeDtypeStruct((n_seq * n_rows, d), F32),
        scratch_shapes=[pltpu.VMEM(w_out.shape[1:], BF16)],
        compiler_params=pltpu.CompilerParams(
            dimension_semantics=("arbitrary",), vmem_limit_bytes=56 * MIB),
    )(o_dn, o_mlp, x3, mod, w_out, ln_g.reshape(1, d), ln_b.reshape(1, d))
    return out.reshape(n_seq, n_rows, d)


def _trunk_layer(x3, mod, wts, layer, alpha, conv_buf, s0, ffn_weights, emit_w16):
    n_seq, n_rows, d = x3.shape
    qkv_dim = 3 * DN_HEADS * LANES
    w16 = []

    def ffn(x3, sub, slot):
        r = _ffn(x3, mod, sub, ffn_weights[slot], wts["ln_g"][sub], wts["ln_b"][sub], alpha, emit_w16)
        if emit_w16:
            w16.append(r[1])
            return r[0]
        return r

    x3 = ffn(x3, 0, 0)
    proj = _proj(x3, mod, 1, wts["w_cat"])
    dn_blk = DN_CHUNK if n_rows % DN_CHUNK == 0 else n_rows
    gdn_kw = dict(n_seq=n_seq, n_rows=n_rows, blk=dn_blk, ba_col=wts["ba_col"])
    if s0 is None:
        o_dn, s_new = _gdn(proj, wts["conv_w"], wts["gate_par"], wts["dn_norm_g"], **gdn_kw)
        new_buf = proj.reshape(n_seq, n_rows, -1)[:, n_rows - (CONV_W - 1):, :qkv_dim]
    else:
        assert n_rows == SUBLANES
        o_dn, s_new, nbuf = _gdn(proj, wts["conv_w"], wts["gate_par"], wts["dn_norm_g"],
                                 conv_buf=jnp.swapaxes(conv_buf, 0, 1), s0=s0, **gdn_kw)
        new_buf = jnp.swapaxes(nbuf, 0, 1)
    mlp_out = _mlp(proj, wts["w_spatial"], wts["b_spatial"], wts["mlp_ln_g"], wts["mlp_ln_b"],
                   blk=min(n_rows, MLP_CHUNK), want_v_rows=s0 is not None)
    o_mlp = mlp_out[0]
    v_rows = mlp_out[1].reshape(n_seq, n_rows, -1) if s0 is not None else None
    x3 = _mix(o_dn, o_mlp, x3, mod, 1, wts["w_out"], layer, wts["ln_g"][1], wts["ln_b"][1], alpha)
    x3 = ffn(x3, 2, 1)
    return x3, new_buf, s_new, v_rows, w16


def kernel(x_prompt, x_sample, c_prompt, c_sample, state_delta, state_conv, w_ada, b_ada, ln_g, ln_b,
           ffn_wg, ffn_wu, ffn_wd, w_in, conv_w, a_log, dt_bias, dn_norm_g, mlp_ln_g, mlp_ln_b,
           w_spatial, b_spatial, w_out):
    depth = w_ada.shape[0]
    bp, _, d = x_prompt.shape
    bs = x_sample.shape[0]
    nh = DN_HEADS
    alpha = (2.0 * depth) ** 0.25
    qkvz = 4 * nh * LANES
    gates = 2 * nh

    pad_rows = (-(bs + bp)) % SUBLANES
    c_all = jnp.concatenate([c_sample, c_prompt, jnp.zeros((pad_rows, d), F32)], axis=0)
    w_in_t = jnp.swapaxes(w_in, 1, 2)

    y_p, y_s = x_prompt, x_sample
    delta_p, conv_p, delta_s, conv_s, vrows_s = [], [], [], [], []
    for layer in range(depth):
        mod_s, mod_p = _ada(c_all, w_ada[layer], b_ada[layer], bs)
        gate_par = jnp.zeros((SUBLANES, LANES), F32)
        gate_par = gate_par.at[0, nh:gates].set(a_log[layer]).at[1, nh:gates].set(dt_bias[layer])
        wts = dict(
            ln_g=ln_g[layer], ln_b=ln_b[layer],
            w_cat=_wprep(w_in_t, layer, qkvz, gates), ba_col=(w_in.shape[-1] - gates) // LANES,
            conv_w=conv_w[layer], gate_par=gate_par,
            dn_norm_g=dn_norm_g[layer].reshape(1, LANES), mlp_ln_g=mlp_ln_g[layer], mlp_ln_b=mlp_ln_b[layer],
            w_spatial=w_spatial[layer], b_spatial=b_spatial[layer], w_out=w_out,
        )
        w32 = [(ffn_wg, ffn_wu, ffn_wd, layer, slot) for slot in range(2)]
        y_s, cb_s, ds_s, vr_s, w16 = _trunk_layer(y_s, mod_s, wts, layer, alpha, state_conv[layer],
                                                  state_delta[layer], w32, True)
        y_p, cb_p, ds_p, _, _ = _trunk_layer(y_p, mod_p, wts, layer, alpha, None, None, w16, False)
        delta_p.append(ds_p)
        conv_p.append(cb_p)
        delta_s.append(ds_s)
        conv_s.append(cb_s)
        vrows_s.append(vr_s)
    return (y_p, y_s, jnp.stack(delta_p), jnp.stack(conv_p), jnp.stack(delta_s), jnp.stack(conv_s),
            jnp.stack(vrows_s))
```

```python
import functools
import math

import jax
import jax.numpy as jnp
from jax import lax
from jax.experimental import pallas as pl
from jax.experimental.pallas import tpu as pltpu

F32 = jnp.float32
BF16 = jnp.bfloat16

DN_HEADS = 8
DN_CHUNK = 64
CONV_W = 4
MLP_GROUPS = 4
MLP_CHUNK = 128
LN_EPS = 1e-5
RMS_EPS = 1e-6

LANES = 128
SUBLANES = 8
MIB = 2 ** 20

FFN_TM = 1024
FFN_TF = 256
FFN_TF16 = 512
FFN_RC = 256
FFN_VMEM_LIMIT = 58 * MIB
PROJ_TN = 1280
PROJ_N = 6400
PREP_TC = 256
MIX_TM = 512
ADA_TN = 1024
SEQ_TILE = 128
GDN_TILE_CHAINED = 256
MLP_TILE = 512


def _dot(a, b, precision=None):
    return jnp.dot(a, b, preferred_element_type=F32, precision=precision)


def _dot_nt(a, b, precision=None):
    return lax.dot_general(a, b, (((1,), (1,)), ((), ())),
                           preferred_element_type=F32, precision=precision)


def _dot_tn(a, b):
    return lax.dot_general(a, b, (((0,), (0,)), ((), ())), preferred_element_type=F32)


def _layer_norm(y, g, b):
    mu = jnp.mean(y, axis=-1, keepdims=True)
    yc = y - mu
    var = jnp.mean(yc * yc, axis=-1, keepdims=True)
    return yc * lax.rsqrt(var + LN_EPS) * g + b


def _silu(x):
    return x * jax.nn.sigmoid(x)


def _mod_rows(ref, seq_tiles):
    if seq_tiles:
        return ref[pl.ds(pl.program_id(0) // seq_tiles, 1), :]
    return ref[...]


def _ada_body(c_ref, w_ref, b_ref, ms_ref, mp_ref):
    c = c_ref[...]
    a = _silu(c).astype(BF16)
    y = _dot(a, w_ref[...].astype(BF16)) + b_ref[...]
    ns = ms_ref.shape[0]
    ms_ref[...] = y[:ns]
    mp_ref[...] = y[ns:]


def _ada(c_all, w_ada, b_ada, n_sample):
    rows, d = c_all.shape
    n = w_ada.shape[1]
    return pl.pallas_call(
        _ada_body,
        name="ada_mod",
        grid=(n // ADA_TN,),
        in_specs=[
            pl.BlockSpec((rows, d), lambda j: (0, 0)),
            pl.BlockSpec((d, ADA_TN), lambda j: (0, j)),
            pl.BlockSpec((1, ADA_TN), lambda j: (0, j)),
        ],
        out_specs=[
            pl.BlockSpec((n_sample, ADA_TN), lambda j: (0, j)),
            pl.BlockSpec((rows - n_sample, ADA_TN), lambda j: (0, j)),
        ],
        out_shape=[
            jax.ShapeDtypeStruct((n_sample, n), F32),
            jax.ShapeDtypeStruct((rows - n_sample, n), F32),
        ],
        compiler_params=pltpu.CompilerParams(
            dimension_semantics=("arbitrary",), vmem_limit_bytes=40 * MIB),
    )(c_all, w_ada, b_ada.reshape(1, n))


class _Group:
    def __init__(self, n_seq, n_rows, tm):
        if n_rows >= tm:
            assert n_rows % tm == 0
            self.sb, self.rb = 1, tm
        else:
            assert tm % n_rows == 0 and n_seq % (tm // n_rows) == 0
            self.sb, self.rb = tm // n_rows, n_rows
        self.n_seq, self.n_rows = n_seq, n_rows
        self.tiles_r = n_rows // self.rb
        self.tm = self.sb * self.rb
        self.n_tiles = (n_seq // self.sb) * self.tiles_r
        self.seq_tiles = self.tiles_r if self.sb == 1 else 0

    def x_spec(self, d, ngrid, **kw):
        tr = self.tiles_r
        if ngrid == 2:
            return pl.BlockSpec((self.sb, self.rb, d), lambda m, f: (m // tr, m % tr, 0), **kw)
        return pl.BlockSpec((self.sb, self.rb, d), lambda m: (m // tr, m % tr, 0), **kw)

    def mod_spec(self, mod, d, col, ngrid):
        if self.sb == 1:
            rows, row_blk = mod.shape[0], (lambda m: 0)
        else:
            rows, row_blk = self.sb, (lambda m: m)
        if ngrid == 2:
            return pl.BlockSpec((rows, d), lambda m, f: (row_blk(m), col))
        return pl.BlockSpec((rows, d), lambda m: (row_blk(m), col))


def _ffn_body(x_ref, sh_ref, sc_ref, gt_ref, wg_ref, wu_ref, wd_ref, lng_ref, lnb_ref,
              o_ref, *rest, seq_tiles, alpha):
    w16_refs, h_sc = rest[:-1], rest[-1]
    f = pl.program_id(1)
    nf = pl.num_programs(1)
    tm, d = o_ref.shape
    sb, rb, _ = x_ref.shape
    rc = min(FFN_RC, tm)
    n_chunks = tm // rc
    if w16_refs:
        for src, dst in zip((wg_ref, wu_ref, wd_ref), w16_refs):
            dst[...] = src[...].astype(BF16)
        wg_ref, wu_ref, wd_ref = w16_refs

    def x_chunk(c):
        if sb == 1:
            return x_ref[:, c * rc:(c + 1) * rc, :]
        return x_ref[c * (rc // rb):(c + 1) * (rc // rb)]

    def mod_chunk(ref, c):
        if seq_tiles:
            return _mod_rows(ref, seq_tiles)
        return ref[c * (rc // rb):(c + 1) * (rc // rb), :]

    def weights():
        return wg_ref[...].astype(BF16), wu_ref[...].astype(BF16), wd_ref[...].astype(BF16)

    def gate_up(h16, wg16, wu16):
        return (_silu(_dot(h16, wg16)) * _dot(h16, wu16)).astype(BF16)

    @pl.when(f == 0)
    def _():
        wg16, wu16, wd16 = weights()
        for c in range(n_chunks):
            rs = slice(c * rc, (c + 1) * rc)
            x = x_chunk(c)
            h = x * (1.0 + mod_chunk(sc_ref, c)[:, None, :]) + mod_chunk(sh_ref, c)[:, None, :]
            h16 = h.reshape(rc, d).astype(BF16)
            h_sc[rs, :] = h16
            o_ref[rs, :] = _dot(gate_up(h16, wg16, wu16), wd16)

    @pl.when(jnp.logical_and(f > 0, f < nf - 1))
    def _():
        wg16, wu16, wd16 = weights()
        o_ref[...] += _dot(gate_up(h_sc[...], wg16, wu16), wd16)

    @pl.when(f == nf - 1)
    def _():
        wg16, wu16, wd16 = weights()
        for c in range(n_chunks):
            rs = slice(c * rc, (c + 1) * rc)
            acc = o_ref[rs, :] + _dot(gate_up(h_sc[rs, :], wg16, wu16), wd16)
            x = x_chunk(c)
            gate = mod_chunk(gt_ref, c)[:, None, :]
            y = alpha * x + (0.5 * gate) * acc.reshape(x.shape)
            o_ref[rs, :] = _layer_norm(y.reshape(rc, d), lng_ref[...], lnb_ref[...])


def _ffn(x3, mod, sub, weights, ln_g, ln_b, alpha, emit_w16=False):
    n_seq, n_rows, d = x3.shape
    grp = _Group(n_seq, n_rows, FFN_TM)
    tm = grp.tm
    rc = min(FFN_RC, tm)
    if len(weights) == 5:
        wg, wu, wd, layer, slot = weights
        tf = FFN_TF
        w_specs = [
            pl.BlockSpec((None, None, d, tf), lambda m, f: (layer, slot, 0, f)),
            pl.BlockSpec((None, None, d, tf), lambda m, f: (layer, slot, 0, f)),
            pl.BlockSpec((None, None, tf, d), lambda m, f: (layer, slot, f, 0)),
        ]
    else:
        assert not emit_w16
        wg, wu, wd = weights
        tf = FFN_TF16
        w_specs = [
            pl.BlockSpec((d, tf), lambda m, f: (0, f)),
            pl.BlockSpec((d, tf), lambda m, f: (0, f)),
            pl.BlockSpec((tf, d), lambda m, f: (f, 0)),
        ]
    dff = wg.shape[-1]
    assert dff // tf >= 2 and tm % rc == 0 and (grp.rb % rc == 0 if grp.sb == 1 else rc % grp.rb == 0)
    out_specs = [pl.BlockSpec((tm, d), lambda m, f: (m, 0))]
    out_shape = [jax.ShapeDtypeStruct((n_seq * n_rows, d), F32)]
    if emit_w16:
        assert grp.n_tiles == 1
        out_specs += [
            pl.BlockSpec((d, tf), lambda m, f: (0, f)),
            pl.BlockSpec((d, tf), lambda m, f: (0, f)),
            pl.BlockSpec((tf, d), lambda m, f: (f, 0)),
        ]
        out_shape += [
            jax.ShapeDtypeStruct((d, dff), BF16),
            jax.ShapeDtypeStruct((d, dff), BF16),
            jax.ShapeDtypeStruct((dff, d), BF16),
        ]
    outs = pl.pallas_call(
        functools.partial(_ffn_body, seq_tiles=grp.seq_tiles, alpha=alpha),
        name="swiglu_ln",
        grid=(grp.n_tiles, dff // tf),
        in_specs=[
            grp.x_spec(d, 2),
            grp.mod_spec(mod, d, sub * 3 + 0, 2),
            grp.mod_spec(mod, d, sub * 3 + 1, 2),
            grp.mod_spec(mod, d, sub * 3 + 2, 2),
            *w_specs,
            pl.BlockSpec((1, d), lambda m, f: (0, 0)),
            pl.BlockSpec((1, d), lambda m, f: (0, 0)),
        ],
        out_specs=out_specs,
        out_shape=out_shape,
        scratch_shapes=[pltpu.VMEM((tm, d), BF16)],
        compiler_params=pltpu.CompilerParams(
            dimension_semantics=("arbitrary", "arbitrary"), vmem_limit_bytes=FFN_VMEM_LIMIT),
    )(x3, mod, mod, mod, wg, wu, wd, ln_g.reshape(1, d), ln_b.reshape(1, d))
    y = outs[0].reshape(n_seq, n_rows, d)
    return (y, tuple(outs[1:])) if emit_w16 else y


def _wprep_body(w_ref, o_ref, *, qkvz, gates):
    n_in, cols = w_ref.shape
    n_out = o_ref.shape[0]
    mlp = n_in - qkvz - gates
    o_ref[0:qkvz, :] = w_ref[0:qkvz, :].astype(BF16)
    o_ref[qkvz:qkvz + mlp, :] = w_ref[qkvz + gates:n_in, :].astype(BF16)
    o_ref[qkvz + mlp:n_in, :] = w_ref[qkvz:qkvz + gates, :].astype(BF16)
    o_ref[n_in:n_out, :] = jnp.zeros((n_out - n_in, cols), BF16)


def _wprep(w_in_t, layer, qkvz, gates):
    _, n_in, d = w_in_t.shape
    assert n_in <= PROJ_N and (n_in - gates) % LANES == 0
    return pl.pallas_call(
        functools.partial(_wprep_body, qkvz=qkvz, gates=gates),
        name="proj_weight_prep",
        grid=(d // PREP_TC,),
        in_specs=[pl.BlockSpec((None, n_in, PREP_TC), lambda i: (layer, 0, i))],
        out_specs=pl.BlockSpec((PROJ_N, PREP_TC), lambda i: (0, i)),
        out_shape=jax.ShapeDtypeStruct((PROJ_N, d), BF16),
        compiler_params=pltpu.CompilerParams(
            dimension_semantics=("arbitrary",), vmem_limit_bytes=40 * MIB),
    )(w_in_t)


def _proj_body(x_ref, sh_ref, sc_ref, w_ref, o_ref, h_sc, *, seq_tiles):
    tm, d = h_sc.shape

    @pl.when(pl.program_id(1) == 0)
    def _():
        x = x_ref[...]
        h = x * (1.0 + _mod_rows(sc_ref, seq_tiles)[:, None, :]) + _mod_rows(sh_ref, seq_tiles)[:, None, :]
        h_sc[...] = h.reshape(tm, d).astype(BF16)

    o_ref[...] = _dot_nt(h_sc[...], w_ref[...])


def _proj(x3, mod, sub, w_cat_t):
    n_seq, n_rows, d = x3.shape
    n = w_cat_t.shape[0]
    grp = _Group(n_seq, n_rows, FFN_TM)
    tm = grp.tm
    return pl.pallas_call(
        functools.partial(_proj_body, seq_tiles=grp.seq_tiles),
        name="in_proj",
        grid=(grp.n_tiles, n // PROJ_TN),
        in_specs=[
            grp.x_spec(d, 2),
            grp.mod_spec(mod, d, sub * 3 + 0, 2),
            grp.mod_spec(mod, d, sub * 3 + 1, 2),
            pl.BlockSpec((PROJ_TN, d), lambda m, j: (j, 0)),
        ],
        out_specs=pl.BlockSpec((tm, PROJ_TN), lambda m, j: (m, j)),
        out_shape=jax.ShapeDtypeStruct((n_seq * n_rows, n), F32),
        scratch_shapes=[pltpu.VMEM((tm, d), BF16)],
        compiler_params=pltpu.CompilerParams(
            dimension_semantics=("arbitrary", "arbitrary"), vmem_limit_bytes=48 * MIB),
    )(x3, mod, mod, w_cat_t)


def _block_masks(n, blk):
    shift = int(math.log2(blk))
    i = lax.broadcasted_iota(jnp.int32, (n, n), 0)
    j = lax.broadcasted_iota(jnp.int32, (n, n), 1)
    same = (i >> shift) == (j >> shift)
    return i, j, same


def _gdn_body(*refs, blk, chained, tiles_per_seq, dk_scale):
    if chained:
        (qkv_ref, z_ref, ba_ref, cw_ref, gp_ref, ng_ref, o_ref, st_ref, xc_sc, act_sc) = refs
    else:
        (qkv_ref, z_ref, ba_ref, buf_ref, sin_ref, cw_ref, gp_ref, ng_ref, o_ref, st_ref, nbuf_ref,
         prev_sc, act_sc) = refs
    lt, cqkv = qkv_ref.shape
    hd = LANES
    nh = cqkv // (3 * hd)
    t = pl.program_id(0)
    halo = SUBLANES
    ngrp = lt // SUBLANES

    if chained:
        first = (t % tiles_per_seq) == 0

        @pl.when(first)
        def _():
            xc_sc[pl.ds(0, halo), :] = jnp.zeros((halo, cqkv), F32)
            st_ref[...] = jnp.zeros_like(st_ref)

        @pl.when(jnp.logical_not(first))
        def _():
            xc_sc[pl.ds(0, halo), :] = xc_sc[pl.ds(lt, halo), :]

        xc_sc[pl.ds(halo, lt), :] = qkv_ref[...]
    else:
        assert blk == SUBLANES
        keep = buf_ref.shape[0]

        @pl.when(t == 0)
        def _():
            prev_sc[...] = jnp.zeros_like(prev_sc)

        for s in range(ngrp):
            for r in range(keep):
                row = s * SUBLANES + SUBLANES - keep + r
                prev_sc[row:row + 1, :] = buf_ref[r, s:s + 1, :]
                nbuf_ref[r, s:s + 1, :] = qkv_ref[row:row + 1, :]

    cblk = 2 * LANES
    row_in_grp = lax.broadcasted_iota(jnp.int32, (ngrp, SUBLANES, cblk), 1)
    for c0 in range(0, cqkv, cblk):
        cs = slice(c0, c0 + cblk)
        if chained:
            xe = xc_sc[:, cs].reshape(ngrp + 1, SUBLANES, cblk)
            cur = xe[1:]
        else:
            cur = qkv_ref[:, cs].reshape(ngrp, SUBLANES, cblk)
            prev = prev_sc[:, cs].reshape(ngrp, SUBLANES, cblk)
        y = None
        for j in range(CONV_W):
            d = CONV_W - 1 - j
            if d == 0:
                term = cur
            elif chained:
                rot = pltpu.roll(xe, d, axis=1)
                term = jnp.where(row_in_grp >= d, rot[1:], rot[:-1])
            else:
                term = jnp.where(row_in_grp >= d, pltpu.roll(cur, d, axis=1), pltpu.roll(prev, d, axis=1))
            term = cw_ref[j:j + 1, cs][None] * term
            y = term if y is None else y + term
        act_sc[:, cs] = _silu(y).reshape(lt, cblk)

    st = SEQ_TILE
    nsub = lt // st
    i, j, same = _block_masks(st, blk)
    incl = same & (i >= j)
    strict = same & (i > j)
    hi = lax.Precision.HIGHEST
    incl_f = incl.astype(F32)
    last_f = (j == (i | (blk - 1))).astype(F32)
    eye = (i == j).astype(F32)
    pair_masks = []
    for lvl in range(int(math.log2(blk))):
        pair_masks.append(((i >> (lvl + 1)) == (j >> (lvl + 1))) & ((i >> lvl) != (j >> lvl)))

    heads = range(nh)
    hp = []
    for s in range(nsub):
        rows = slice(s * st, (s + 1) * st)
        ba = ba_ref[rows, :]
        beta_all = jax.nn.sigmoid(ba)
        g_all = -jnp.exp(gp_ref[0:1, :]) * jax.nn.softplus(ba + gp_ref[1:2, :])
        gc_all = _dot(incl_f, g_all, hi)
        gl_all = _dot(last_f, gc_all, hi)
        gc_t = gc_all.T
        for h in heads:
            q = act_sc[rows, h * hd:(h + 1) * hd]
            k = act_sc[rows, (nh + h) * hd:(nh + h + 1) * hd]
            v = act_sc[rows, (2 * nh + h) * hd:(2 * nh + h + 1) * hd]
            q = q * lax.rsqrt(jnp.sum(q * q, axis=-1, keepdims=True) + RMS_EPS) * dk_scale
            k = k * lax.rsqrt(jnp.sum(k * k, axis=-1, keepdims=True) + RMS_EPS)
            beta = beta_all[:, h:h + 1]
            gcc = gc_all[:, nh + h:nh + h + 1]
            gcr = gc_t[nh + h:nh + h + 1, :]
            glc = gl_all[:, nh + h:nh + h + 1]
            decay = jnp.exp(jnp.where(incl, gcc - gcr, -jnp.inf))
            kb = k * beta
            k16 = k.astype(BF16)
            egc = jnp.exp(gcc)
            hp.append(dict(
                sub=s, head=h, decay=decay, k16=k16, kb16=kb.astype(BF16), q16=q.astype(BF16), glc=glc,
                rhs16=jnp.concatenate([v * beta, kb * egc], axis=1).astype(BF16),
                qd=q * egc, kd=k * jnp.exp(glc - gcc)))

    ms = [_dot_nt(p["kb16"], p["k16"]) * jnp.where(strict, p["decay"], 0.0) for p in hp]
    dinvs = [eye - jnp.where(pair_masks[0], m, 0.0) for m in ms]
    for pm in pair_masks[1:]:
        d16s = [d.astype(BF16) for d in dinvs]
        t16s = [_dot(d16, jnp.where(pm, m, 0.0).astype(BF16)).astype(BF16) for d16, m in zip(d16s, ms)]
        dinvs = [d - _dot(t16, d16) for d, t16, d16 in zip(dinvs, t16s, d16s)]
    xs = [_dot(d.astype(BF16), p["rhs16"]) for d, p in zip(dinvs, hp)]
    qk16s = [(_dot_nt(p["q16"], p["k16"]) * p["decay"]).astype(BF16) for p in hp]

    blk_per_sub = st // blk
    vn_parts = [[] for _ in hp]
    os_parts = [[] for _ in hp]
    for b in range(lt // blk):
        lb = b % blk_per_sub
        rs = slice(lb * blk, (lb + 1) * blk)
        cidx = [(b // blk_per_sub) * nh + h for h in heads]
        s0s = [st_ref[0, h] if chained else sin_ref[b, h] for h in heads]
        rr = [_dot(jnp.concatenate([xs[c][rs, hd:], hp[c]["qd"][rs]], axis=0).astype(BF16), s0.astype(BF16))
              for c, s0 in zip(cidx, s0s)]
        vns = [xs[c][rs, :hd] - r[:blk] for c, r in zip(cidx, rr)]
        for h, c in zip(heads, cidx):
            p = hp[c]
            s_new = (s0s[h] * jnp.exp(p["glc"][lb * blk:lb * blk + 1, :])
                     + _dot_tn(p["kd"][rs].astype(BF16), vns[h].astype(BF16)))
            if chained:
                st_ref[0, h] = s_new
            else:
                st_ref[b, h] = s_new
            vn_parts[c].append(vns[h])
            os_parts[c].append(rr[h][blk:])
    for c, p in enumerate(hp):
        rows = slice(p["sub"] * st, (p["sub"] + 1) * st)
        cols = slice(p["head"] * hd, (p["head"] + 1) * hd)
        vn_all = jnp.concatenate(vn_parts[c], axis=0)
        o = jnp.concatenate(os_parts[c], axis=0) + _dot(qk16s[c], vn_all.astype(BF16))
        o = o * lax.rsqrt(jnp.mean(o * o, axis=-1, keepdims=True) + RMS_EPS) * ng_ref[...]
        o = o * _silu(z_ref[rows, cols])
        o_ref[rows, cols] = o.astype(BF16)


def _gdn(proj, conv_w, gate_par, norm_g, *, n_seq, n_rows, blk, ba_col, conv_buf=None, s0=None):
    t_rows = proj.shape[0]
    nh = DN_HEADS
    hd = LANES
    cqkv = 3 * nh * hd
    chained = s0 is None
    lt = GDN_TILE_CHAINED if chained else SEQ_TILE
    assert lt % SEQ_TILE == 0 and SEQ_TILE % blk == 0 and blk & (blk - 1) == 0 and blk >= 2
    n_tiles = t_rows // lt
    common_in = [
        pl.BlockSpec((lt, cqkv), lambda t: (t, 0)),
        pl.BlockSpec((lt, nh * hd), lambda t: (t, 3)),
        pl.BlockSpec((lt, LANES), lambda t: (t, ba_col)),
    ]
    par_in = [
        pl.BlockSpec(conv_w.shape, lambda t: (0, 0)),
        pl.BlockSpec(gate_par.shape, lambda t: (0, 0)),
        pl.BlockSpec((1, hd), lambda t: (0, 0)),
    ]
    o_spec = pl.BlockSpec((lt, nh * hd), lambda t: (t, 0))
    o_shape = jax.ShapeDtypeStruct((t_rows, nh * hd), BF16)
    scratch, extra_specs, extra_shapes = [], [], []
    if chained:
        assert n_rows % lt == 0 and blk == DN_CHUNK
        tps = n_rows // lt
        in_specs = common_in + par_in
        args = (proj, proj, proj, conv_w, gate_par, norm_g)
        st_spec = pl.BlockSpec((1, nh, hd, hd), lambda t: (t // tps, 0, 0, 0))
        scratch.append(pltpu.VMEM((lt + SUBLANES, cqkv), F32))
    else:
        assert blk == n_rows and conv_buf is not None and conv_buf.shape[1:] == (n_seq, cqkv)
        tps = 0
        spt = lt // n_rows
        buf_spec = pl.BlockSpec((conv_buf.shape[0], spt, cqkv), lambda t: (0, t, 0))
        in_specs = common_in + [
            buf_spec,
            pl.BlockSpec((spt, nh, hd, hd), lambda t: (t, 0, 0, 0)),
        ] + par_in
        args = (proj, proj, proj, conv_buf, s0, conv_w, gate_par, norm_g)
        st_spec = pl.BlockSpec((spt, nh, hd, hd), lambda t: (t, 0, 0, 0))
        extra_specs.append(buf_spec)
        extra_shapes.append(jax.ShapeDtypeStruct(conv_buf.shape, F32))
        scratch.append(pltpu.VMEM((lt, cqkv), F32))
    scratch.append(pltpu.VMEM((lt, cqkv), F32))
    return pl.pallas_call(
        functools.partial(_gdn_body, blk=blk, chained=chained, tiles_per_seq=tps,
                          dk_scale=float(hd) ** -0.5),
        name="gdn_chained" if chained else "gdn_stateful",
        grid=(n_tiles,),
        in_specs=in_specs,
        out_specs=[o_spec, st_spec] + extra_specs,
        out_shape=[o_shape, jax.ShapeDtypeStruct((n_seq, nh, hd, hd), F32)] + extra_shapes,
        scratch_shapes=scratch,
        compiler_params=pltpu.CompilerParams(
            dimension_semantics=("arbitrary",), vmem_limit_bytes=56 * MIB),
    )(*args)


def _mlp_body(u_ref, v_ref, ws_ref, bst_ref, lng_ref, lnb_ref, o_ref, *maybe_vr_ref, blk):
    ck = ws_ref.shape[1]
    gd = u_ref.shape[1] // MLP_GROUPS
    i, j, same = _block_masks(ck, blk)
    incl = same & (i >= j)
    hi = lax.Precision.HIGHEST
    if blk != ck:
        rep = (j == (i & (blk - 1))).astype(F32)
        bias_all = _dot(rep, bst_ref[...], hi)
    else:
        bias_all = bst_ref[...]
    for g in range(MLP_GROUPS):
        cs = slice(g * gd, (g + 1) * gd)
        wsp = ws_ref[g]
        if blk != ck:
            wsp = _dot_nt(_dot(rep, wsp, hi), rep, hi)
        wsp = jnp.where(incl, wsp, 0.0).astype(BF16)
        for c in range(u_ref.shape[0] // ck):
            rs = slice(c * ck, (c + 1) * ck)
            uu = jax.nn.gelu(u_ref[rs, cs])
            vv = _layer_norm(jax.nn.gelu(v_ref[rs, cs]), lng_ref[:, cs], lnb_ref[:, cs])
            if maybe_vr_ref:
                maybe_vr_ref[0][rs, cs] = vv
            s = _dot(wsp, vv.astype(BF16)) + bias_all[:, g:g + 1]
            o_ref[rs, cs] = (uu * s).astype(BF16)


def _mlp(proj, w_spatial, b_spatial, ln_g, ln_b, *, blk, want_v_rows):
    t_rows = proj.shape[0]
    lt = MLP_TILE
    ck = w_spatial.shape[1]
    width = ln_g.size
    assert ck == MLP_CHUNK and lt % ck == 0 and ck % blk == 0 and t_rows % lt == 0
    u_col = (4 * DN_HEADS * LANES) // width
    bst = jnp.zeros((ck, LANES), F32).at[:, :MLP_GROUPS].set(b_spatial.T)
    o_spec = pl.BlockSpec((lt, width), lambda t: (t, 0))
    out_specs = [o_spec]
    out_shape = [jax.ShapeDtypeStruct((t_rows, width), BF16)]
    if want_v_rows:
        out_specs.append(o_spec)
        out_shape.append(jax.ShapeDtypeStruct((t_rows, width), F32))
    return pl.pallas_call(
        functools.partial(_mlp_body, blk=blk),
        name="spatial_mlp",
        grid=(t_rows // lt,),
        in_specs=[
            pl.BlockSpec((lt, width), lambda t: (t, u_col)),
            pl.BlockSpec((lt, width), lambda t: (t, u_col + 1)),
            pl.BlockSpec(w_spatial.shape, lambda t: (0, 0, 0)),
            pl.BlockSpec((ck, LANES), lambda t: (0, 0)),
            pl.BlockSpec((1, width), lambda t: (0, 0)),
            pl.BlockSpec((1, width), lambda t: (0, 0)),
        ],
        out_specs=out_specs,
        out_shape=out_shape,
        compiler_params=pltpu.CompilerParams(
            dimension_semantics=("arbitrary",), vmem_limit_bytes=32 * MIB),
    )(proj, proj, w_spatial, bst, ln_g.reshape(1, width), ln_b.reshape(1, width))


def _mix_body(od_ref, om_ref, x_ref, gt_ref, w_ref, lng_ref, lnb_ref, o_ref, w_sc, *, seq_tiles, alpha):
    tm, d = o_ref.shape
    kd = od_ref.shape[1]

    @pl.when(pl.program_id(0) == 0)
    def _():
        w_sc[...] = w_ref[...].astype(BF16)

    sb, rb, _ = x_ref.shape
    rc = min(FFN_RC, tm)
    for c in range(tm // rc):
        rs = slice(c * rc, (c + 1) * rc)
        mix = _dot(od_ref[rs, :], w_sc[:kd, :]) + _dot(om_ref[rs, :], w_sc[kd:, :])
        if sb == 1:
            x = x_ref[:, rs, :]
            gate = _mod_rows(gt_ref, seq_tiles)[:, None, :]
        else:
            ss = slice(c * (rc // rb), (c + 1) * (rc // rb))
            x = x_ref[ss]
            gate = gt_ref[ss, :][:, None, :]
        y = alpha * x + gate * mix.reshape(x.shape)
        o_ref[rs, :] = _layer_norm(y.reshape(rc, d), lng_ref[...], lnb_ref[...])


def _mix(o_dn, o_mlp, x3, mod, sub, w_out, layer, ln_g, ln_b, alpha):
    n_seq, n_rows, d = x3.shape
    grp = _Group(n_seq, n_rows, MIX_TM)
    tm = grp.tm
    out = pl.pallas_call(
        functools.partial(_mix_body, seq_tiles=grp.seq_tiles, alpha=alpha),
        name="out_mix_ln",
        grid=(grp.n_tiles,),
        in_specs=[
            pl.BlockSpec((tm, o_dn.shape[1]), lambda m: (m, 0)),
            pl.BlockSpec((tm, o_mlp.shape[1]), lambda m: (m, 0)),
            grp.x_spec(d, 1),
            grp.mod_spec(mod, d, sub * 3 + 2, 1),
            pl.BlockSpec((None,) + w_out.shape[1:], lambda m: (layer, 0, 0), pipeline_mode=pl.Buffered(1)),
            pl.BlockSpec((1, d), lambda m: (0, 0)),
            pl.BlockSpec((1, d), lambda m: (0, 0)),
        ],
        out_specs=pl.BlockSpec((tm, d), lambda m: (m, 0)),
        out_shape=jax.ShapeDtypeStruct((n_seq * n_rows, d), F32),
        scratch_shapes=[pltpu.VMEM(w_out.shape[1:], BF16)],
        compiler_params=pltpu.CompilerParams(
            dimension_semantics=("arbitrary",), vmem_limit_bytes=56 * MIB),
    )(o_dn, o_mlp, x3, mod, w_out, ln_g.reshape(1, d), ln_b.reshape(1, d))
    return out.reshape(n_seq, n_rows, d)


def _trunk_layer(x3, mod, wts, layer, alpha, conv_buf, s0, ffn_weights, emit_w16):
    n_seq, n_rows, d = x3.shape
    qkv_dim = 3 * DN_HEADS * LANES
    w16 = []

    def ffn(x3, sub, slot):
        r = _ffn(x3, mod, sub, ffn_weights[slot], wts["ln_g"][sub], wts["ln_b"][sub], alpha, emit_w16)
        if emit_w16:
            w16.append(r[1])
            return r[0]
        return r

    x3 = ffn(x3, 0, 0)
    proj = _proj(x3, mod, 1, wts["w_cat"])
    dn_blk = DN_CHUNK if n_rows % DN_CHUNK == 0 else n_rows
    gdn_kw = dict(n_seq=n_seq, n_rows=n_rows, blk=dn_blk, ba_col=wts["ba_col"])
    if s0 is None:
        o_dn, s_new = _gdn(proj, wts["conv_w"], wts["gate_par"], wts["dn_norm_g"], **gdn_kw)
        new_buf = proj.reshape(n_seq, n_rows, -1)[:, n_rows - (CONV_W - 1):, :qkv_dim]
    else:
        assert n_rows == SUBLANES
        o_dn, s_new, nbuf = _gdn(proj, wts["conv_w"], wts["gate_par"], wts["dn_norm_g"],
                                 conv_buf=jnp.swapaxes(conv_buf, 0, 1), s0=s0, **gdn_kw)
        new_buf = jnp.swapaxes(nbuf, 0, 1)
    mlp_out = _mlp(proj, wts["w_spatial"], wts["b_spatial"], wts["mlp_ln_g"], wts["mlp_ln_b"],
                   blk=min(n_rows, MLP_CHUNK), want_v_rows=s0 is not None)
    o_mlp = mlp_out[0]
    v_rows = mlp_out[1].reshape(n_seq, n_rows, -1) if s0 is not None else None
    x3 = _mix(o_dn, o_mlp, x3, mod, 1, wts["w_out"], layer, wts["ln_g"][1], wts["ln_b"][1], alpha)
    x3 = ffn(x3, 2, 1)
    return x3, new_buf, s_new, v_rows, w16


def kernel(x_prompt, x_sample, c_prompt, c_sample, state_delta, state_conv, w_ada, b_ada, ln_g, ln_b,
           ffn_wg, ffn_wu, ffn_wd, w_in, conv_w, a_log, dt_bias, dn_norm_g, mlp_ln_g, mlp_ln_b,
           w_spatial, b_spatial, w_out):
    depth = w_ada.shape[0]
    bp, _, d = x_prompt.shape
    bs = x_sample.shape[0]
    nh = DN_HEADS
    alpha = (2.0 * depth) ** 0.25
    qkvz = 4 * nh * LANES
    gates = 2 * nh

    pad_rows = (-(bs + bp)) % SUBLANES
    c_all = jnp.concatenate([c_sample, c_prompt, jnp.zeros((pad_rows, d), F32)], axis=0)
    w_in_t = jnp.swapaxes(w_in, 1, 2)

    y_p, y_s = x_prompt, x_sample
    delta_p, conv_p, delta_s, conv_s, vrows_s = [], [], [], [], []
    for layer in range(depth):
        mod_s, mod_p = _ada(c_all, w_ada[layer], b_ada[layer], bs)
        gate_par = jnp.zeros((SUBLANES, LANES), F32)
        gate_par = gate_par.at[0, nh:gates].set(a_log[layer]).at[1, nh:gates].set(dt_bias[layer])
        wts = dict(
            ln_g=ln_g[layer], ln_b=ln_b[layer],
            w_cat=_wprep(w_in_t, layer, qkvz, gates), ba_col=(w_in.shape[-1] - gates) // LANES,
            conv_w=conv_w[layer], gate_par=gate_par,
            dn_norm_g=dn_norm_g[layer].reshape(1, LANES), mlp_ln_g=mlp_ln_g[layer], mlp_ln_b=mlp_ln_b[layer],
            w_spatial=w_spatial[layer], b_spatial=b_spatial[layer], w_out=w_out,
        )
        w32 = [(ffn_wg, ffn_wu, ffn_wd, layer, slot) for slot in range(2)]
        y_s, cb_s, ds_s, vr_s, w16 = _trunk_layer(y_s, mod_s, wts, layer, alpha, state_conv[layer],
                                                  state_delta[layer], w32, True)
        y_p, cb_p, ds_p, _, _ = _trunk_layer(y_p, mod_p, wts, layer, alpha, None, None, w16, False)
        delta_p.append(ds_p)
        conv_p.append(cb_p)
        delta_s.append(ds_s)
        conv_s.append(cb_s)
        vrows_s.append(vr_s)
    return (y_p, y_s, jnp.stack(delta_p), jnp.stack(conv_p), jnp.stack(delta_s), jnp.stack(conv_s),
            jnp.stack(vrows_s))
```

```python
import functools
import math

import jax
import jax.numpy as jnp
from jax import lax
from jax.experimental import pallas as pl
from jax.experimental.pallas import tpu as pltpu

F32 = jnp.float32
BF16 = jnp.bfloat16

DN_HEADS = 8
DN_CHUNK = 64
CONV_W = 4
MLP_GROUPS = 4
MLP_CHUNK = 128
LN_EPS = 1e-5
RMS_EPS = 1e-6

LANES = 128
SUBLANES = 8
MXU_COLS = 256
MIB = 2 ** 20

FFN_TM = 1024
FFN_TF = 256
FFN_TF16 = 512
FFN_RC = 256
FFN_VMEM_LIMIT = 58 * MIB
PROJ_TN = 1280
PROJ_N = 6400
PREP_TC = 256
MIX_TM = 512
ADA_TN = 1024
SEQ_TILE = 128
GDN_TILE_CHAINED = 256
MLP_TILE = 512


def _dot(a, b, precision=None):
    return jnp.dot(a, b, preferred_element_type=F32, precision=precision)


def _dot_nt(a, b, precision=None):
    return lax.dot_general(a, b, (((1,), (1,)), ((), ())),
                           preferred_element_type=F32, precision=precision)


def _dot_tn(a, b):
    return lax.dot_general(a, b, (((0,), (0,)), ((), ())), preferred_element_type=F32)


def _layer_norm(y, g, b):
    mu = jnp.mean(y, axis=-1, keepdims=True)
    yc = y - mu
    var = jnp.mean(yc * yc, axis=-1, keepdims=True)
    return yc * lax.rsqrt(var + LN_EPS) * g + b


def _silu(x):
    return x * jax.nn.sigmoid(x)


def _mod_rows(ref, seq_tiles):
    if seq_tiles:
        return ref[pl.ds(pl.program_id(0) // seq_tiles, 1), :]
    return ref[...]


def _ada_body(c_ref, w_ref, b_ref, ms_ref, mp_ref):
    c = c_ref[...]
    a = _silu(c).astype(BF16)
    y = _dot(a, w_ref[...].astype(BF16)) + b_ref[...]
    ns = ms_ref.shape[0]
    ms_ref[...] = y[:ns]
    mp_ref[...] = y[ns:]


def _ada(c_all, w_ada, b_ada, n_sample):
    rows, d = c_all.shape
    n = w_ada.shape[1]
    return pl.pallas_call(
        _ada_body,
        name="ada_mod",
        grid=(n // ADA_TN,),
        in_specs=[
            pl.BlockSpec((rows, d), lambda j: (0, 0)),
            pl.BlockSpec((d, ADA_TN), lambda j: (0, j)),
            pl.BlockSpec((1, ADA_TN), lambda j: (0, j)),
        ],
        out_specs=[
            pl.BlockSpec((n_sample, ADA_TN), lambda j: (0, j)),
            pl.BlockSpec((rows - n_sample, ADA_TN), lambda j: (0, j)),
        ],
        out_shape=[
            jax.ShapeDtypeStruct((n_sample, n), F32),
            jax.ShapeDtypeStruct((rows - n_sample, n), F32),
        ],
        compiler_params=pltpu.CompilerParams(
            dimension_semantics=("arbitrary",), vmem_limit_bytes=40 * MIB),
    )(c_all, w_ada, b_ada.reshape(1, n))


class _Group:
    def __init__(self, n_seq, n_rows, tm):
        if n_rows >= tm:
            assert n_rows % tm == 0
            self.sb, self.rb = 1, tm
        else:
            assert tm % n_rows == 0 and n_seq % (tm // n_rows) == 0
            self.sb, self.rb = tm // n_rows, n_rows
        self.n_seq, self.n_rows = n_seq, n_rows
        self.tiles_r = n_rows // self.rb
        self.tm = self.sb * self.rb
        self.n_tiles = (n_seq // self.sb) * self.tiles_r
        self.seq_tiles = self.tiles_r if self.sb == 1 else 0

    def x_spec(self, d, ngrid, **kw):
        tr = self.tiles_r
        if ngrid == 2:
            return pl.BlockSpec((self.sb, self.rb, d), lambda m, f: (m // tr, m % tr, 0), **kw)
        return pl.BlockSpec((self.sb, self.rb, d), lambda m: (m // tr, m % tr, 0), **kw)

    def mod_spec(self, mod, d, col, ngrid):
        if self.sb == 1:
            rows, row_blk = mod.shape[0], (lambda m: 0)
        else:
            rows, row_blk = self.sb, (lambda m: m)
        if ngrid == 2:
            return pl.BlockSpec((rows, d), lambda m, f: (row_blk(m), col))
        return pl.BlockSpec((rows, d), lambda m: (row_blk(m), col))


def _ffn_body(x_ref, sh_ref, sc_ref, gt_ref, wg_ref, wu_ref, wd_ref, lng_ref, lnb_ref,
              o_ref, *rest, seq_tiles, alpha):
    w16_refs, h_sc = rest[:-1], rest[-1]
    f = pl.program_id(1)
    nf = pl.num_programs(1)
    tm, d = o_ref.shape
    sb, rb, _ = x_ref.shape
    rc = min(FFN_RC, tm)
    n_chunks = tm // rc
    if w16_refs:
        for src, dst in zip((wg_ref, wu_ref, wd_ref), w16_refs):
            dst[...] = src[...].astype(BF16)
        wg_ref, wu_ref, wd_ref = w16_refs

    def x_chunk(c):
        if sb == 1:
            return x_ref[:, c * rc:(c + 1) * rc, :]
        return x_ref[c * (rc // rb):(c + 1) * (rc // rb)]

    def mod_chunk(ref, c):
        if seq_tiles:
            return _mod_rows(ref, seq_tiles)
        return ref[c * (rc // rb):(c + 1) * (rc // rb), :]

    def weights():
        return wg_ref[...].astype(BF16), wu_ref[...].astype(BF16), wd_ref[...].astype(BF16)

    def gate_up(h16, wg16, wu16):
        return (_silu(_dot(h16, wg16)) * _dot(h16, wu16)).astype(BF16)

    @pl.when(f == 0)
    def _():
        wg16, wu16, wd16 = weights()
        for c in range(n_chunks):
            rs = slice(c * rc, (c + 1) * rc)
            x = x_chunk(c)
            h = x * (1.0 + mod_chunk(sc_ref, c)[:, None, :]) + mod_chunk(sh_ref, c)[:, None, :]
            h16 = h.reshape(rc, d).astype(BF16)
            h_sc[rs, :] = h16
            o_ref[rs, :] = _dot(gate_up(h16, wg16, wu16), wd16)

    @pl.when(jnp.logical_and(f > 0, f < nf - 1))
    def _():
        wg16, wu16, wd16 = weights()
        h16 = h_sc[...]
        half = wd16.shape[0] // 2
        if half % MXU_COLS:
            o_ref[...] += _dot(gate_up(h16, wg16, wu16), wd16)
        else:
            a0 = gate_up(h16, wg16[:, :half], wu16[:, :half])
            a1 = gate_up(h16, wg16[:, half:], wu16[:, half:])
            o_ref[...] += _dot(a0, wd16[:half, :]) + _dot(a1, wd16[half:, :])

    @pl.when(f == nf - 1)
    def _():
        wg16, wu16, wd16 = weights()
        for c in range(n_chunks):
            rs = slice(c * rc, (c + 1) * rc)
            acc = o_ref[rs, :] + _dot(gate_up(h_sc[rs, :], wg16, wu16), wd16)
            x = x_chunk(c)
            gate = mod_chunk(gt_ref, c)[:, None, :]
            y = alpha * x + (0.5 * gate) * acc.reshape(x.shape)
            o_ref[rs, :] = _layer_norm(y.reshape(rc, d), lng_ref[...], lnb_ref[...])


def _ffn(x3, mod, sub, weights, ln_g, ln_b, alpha, emit_w16=False):
    n_seq, n_rows, d = x3.shape
    grp = _Group(n_seq, n_rows, FFN_TM)
    tm = grp.tm
    rc = min(FFN_RC, tm)
    if len(weights) == 5:
        wg, wu, wd, layer, slot = weights
        tf = FFN_TF
        w_specs = [
            pl.BlockSpec((None, None, d, tf), lambda m, f: (layer, slot, 0, f)),
            pl.BlockSpec((None, None, d, tf), lambda m, f: (layer, slot, 0, f)),
            pl.BlockSpec((None, None, tf, d), lambda m, f: (layer, slot, f, 0)),
        ]
    else:
        assert not emit_w16
        wg, wu, wd = weights
        tf = FFN_TF16
        w_specs = [
            pl.BlockSpec((d, tf), lambda m, f: (0, f)),
            pl.BlockSpec((d, tf), lambda m, f: (0, f)),
            pl.BlockSpec((tf, d), lambda m, f: (f, 0)),
        ]
    dff = wg.shape[-1]
    assert dff // tf >= 2 and tm % rc == 0 and (grp.rb % rc == 0 if grp.sb == 1 else rc % grp.rb == 0)
    out_specs = [pl.BlockSpec((tm, d), lambda m, f: (m, 0))]
    out_shape = [jax.ShapeDtypeStruct((n_seq * n_rows, d), F32)]
    if emit_w16:
        assert grp.n_tiles == 1
        out_specs += [
            pl.BlockSpec((d, tf), lambda m, f: (0, f)),
            pl.BlockSpec((d, tf), lambda m, f: (0, f)),
            pl.BlockSpec((tf, d), lambda m, f: (f, 0)),
        ]
        out_shape += [
            jax.ShapeDtypeStruct((d, dff), BF16),
            jax.ShapeDtypeStruct((d, dff), BF16),
            jax.ShapeDtypeStruct((dff, d), BF16),
        ]
    outs = pl.pallas_call(
        functools.partial(_ffn_body, seq_tiles=grp.seq_tiles, alpha=alpha),
        name="swiglu_ln",
        grid=(grp.n_tiles, dff // tf),
        in_specs=[
            grp.x_spec(d, 2),
            grp.mod_spec(mod, d, sub * 3 + 0, 2),
            grp.mod_spec(mod, d, sub * 3 + 1, 2),
            grp.mod_spec(mod, d, sub * 3 + 2, 2),
            *w_specs,
            pl.BlockSpec((1, d), lambda m, f: (0, 0)),
            pl.BlockSpec((1, d), lambda m, f: (0, 0)),
        ],
        out_specs=out_specs,
        out_shape=out_shape,
        scratch_shapes=[pltpu.VMEM((tm, d), BF16)],
        compiler_params=pltpu.CompilerParams(
            dimension_semantics=("arbitrary", "arbitrary"), vmem_limit_bytes=FFN_VMEM_LIMIT),
    )(x3, mod, mod, mod, wg, wu, wd, ln_g.reshape(1, d), ln_b.reshape(1, d))
    y = outs[0].reshape(n_seq, n_rows, d)
    return (y, tuple(outs[1:])) if emit_w16 else y


def _wprep_body(w_ref, o_ref, *, qkvz, gates):
    n_in, cols = w_ref.shape
    n_out = o_ref.shape[0]
    mlp = n_in - qkvz - gates
    o_ref[0:qkvz, :] = w_ref[0:qkvz, :].astype(BF16)
    o_ref[qkvz:qkvz + mlp, :] = w_ref[qkvz + gates:n_in, :].astype(BF16)
    o_ref[qkvz + mlp:n_in, :] = w_ref[qkvz:qkvz + gates, :].astype(BF16)
    o_ref[n_in:n_out, :] = jnp.zeros((n_out - n_in, cols), BF16)


def _wprep(w_in_t, layer, qkvz, gates):
    _, n_in, d = w_in_t.shape
    assert n_in <= PROJ_N and (n_in - gates) % LANES == 0
    return pl.pallas_call(
        functools.partial(_wprep_body, qkvz=qkvz, gates=gates),
        name="proj_weight_prep",
        grid=(d // PREP_TC,),
        in_specs=[pl.BlockSpec((None, n_in, PREP_TC), lambda i: (layer, 0, i))],
        out_specs=pl.BlockSpec((PROJ_N, PREP_TC), lambda i: (0, i)),
        out_shape=jax.ShapeDtypeStruct((PROJ_N, d), BF16),
        compiler_params=pltpu.CompilerParams(
            dimension_semantics=("arbitrary",), vmem_limit_bytes=40 * MIB),
    )(w_in_t)


def _proj_body(x_ref, sh_ref, sc_ref, w_ref, o_ref, h_sc, *, seq_tiles):
    tm, d = h_sc.shape
    sb, rb, _ = x_ref.shape
    rc = min(FFN_RC, tm)
    j = pl.program_id(1)

    @pl.when(j == 0)
    def _():
        for c in range(tm // rc):
            rs = slice(c * rc, (c + 1) * rc)
            if sb == 1:
                x = x_ref[:, rs, :]
                sc, sh = _mod_rows(sc_ref, seq_tiles), _mod_rows(sh_ref, seq_tiles)
            else:
                ss = slice(c * (rc // rb), (c + 1) * (rc // rb))
                x = x_ref[ss]
                sc, sh = sc_ref[ss, :], sh_ref[ss, :]
            h16 = (x * (1.0 + sc[:, None, :]) + sh[:, None, :]).reshape(rc, d).astype(BF16)
            h_sc[rs, :] = h16
            o_ref[rs, :] = _dot_nt(h16, w_ref[...])

    @pl.when(j != 0)
    def _():
        o_ref[...] = _dot_nt(h_sc[...], w_ref[...])


def _proj(x3, mod, sub, w_cat_t):
    n_seq, n_rows, d = x3.shape
    n = w_cat_t.shape[0]
    grp = _Group(n_seq, n_rows, FFN_TM)
    tm = grp.tm
    return pl.pallas_call(
        functools.partial(_proj_body, seq_tiles=grp.seq_tiles),
        name="in_proj",
        grid=(grp.n_tiles, n // PROJ_TN),
        in_specs=[
            grp.x_spec(d, 2),
            grp.mod_spec(mod, d, sub * 3 + 0, 2),
            grp.mod_spec(mod, d, sub * 3 + 1, 2),
            pl.BlockSpec((PROJ_TN, d), lambda m, j: (j, 0)),
        ],
        out_specs=pl.BlockSpec((tm, PROJ_TN), lambda m, j: (m, j)),
        out_shape=jax.ShapeDtypeStruct((n_seq * n_rows, n), F32),
        scratch_shapes=[pltpu.VMEM((tm, d), BF16)],
        compiler_params=pltpu.CompilerParams(
            dimension_semantics=("arbitrary", "arbitrary"), vmem_limit_bytes=48 * MIB),
    )(x3, mod, mod, w_cat_t)


def _block_masks(n, blk):
    shift = int(math.log2(blk))
    i = lax.broadcasted_iota(jnp.int32, (n, n), 0)
    j = lax.broadcasted_iota(jnp.int32, (n, n), 1)
    same = (i >> shift) == (j >> shift)
    return i, j, same


def _gdn_body(*refs, blk, chained, tiles_per_seq, dk_scale):
    if chained:
        (qkv_ref, z_ref, ba_ref, cw_ref, gp_ref, ng_ref, o_ref, st_ref, xc_sc, act_sc) = refs
    else:
        (qkv_ref, z_ref, ba_ref, buf_ref, sin_ref, cw_ref, gp_ref, ng_ref, o_ref, st_ref, nbuf_ref,
         prev_sc, act_sc) = refs
    lt, cqkv = qkv_ref.shape
    hd = LANES
    nh = cqkv // (3 * hd)
    t = pl.program_id(0)
    halo = SUBLANES
    ngrp = lt // SUBLANES

    if chained:
        first = (t % tiles_per_seq) == 0

        @pl.when(first)
        def _():
            xc_sc[pl.ds(0, halo), :] = jnp.zeros((halo, cqkv), F32)
            st_ref[...] = jnp.zeros_like(st_ref)

        @pl.when(jnp.logical_not(first))
        def _():
            xc_sc[pl.ds(0, halo), :] = xc_sc[pl.ds(lt, halo), :]

        xc_sc[pl.ds(halo, lt), :] = qkv_ref[...]
    else:
        assert blk == SUBLANES
        keep = buf_ref.shape[0]

        @pl.when(t == 0)
        def _():
            prev_sc[...] = jnp.zeros_like(prev_sc)

        for s in range(ngrp):
            for r in range(keep):
                row = s * SUBLANES + SUBLANES - keep + r
                prev_sc[row:row + 1, :] = buf_ref[r, s:s + 1, :]
                nbuf_ref[r, s:s + 1, :] = qkv_ref[row:row + 1, :]

    cblk = 2 * LANES
    row_in_grp = lax.broadcasted_iota(jnp.int32, (ngrp, SUBLANES, cblk), 1)
    for c0 in range(0, cqkv, cblk):
        cs = slice(c0, c0 + cblk)
        if chained:
            xe = xc_sc[:, cs].reshape(ngrp + 1, SUBLANES, cblk)
            cur = xe[1:]
        else:
            cur = qkv_ref[:, cs].reshape(ngrp, SUBLANES, cblk)
            prev = prev_sc[:, cs].reshape(ngrp, SUBLANES, cblk)
        y = None
        for j in range(CONV_W):
            d = CONV_W - 1 - j
            if d == 0:
                term = cur
            elif chained:
                rot = pltpu.roll(xe, d, axis=1)
                term = jnp.where(row_in_grp >= d, rot[1:], rot[:-1])
            else:
                term = jnp.where(row_in_grp >= d, pltpu.roll(cur, d, axis=1), pltpu.roll(prev, d, axis=1))
            term = cw_ref[j:j + 1, cs][None] * term
            y = term if y is None else y + term
        act_sc[:, cs] = _silu(y).reshape(lt, cblk)

    st = SEQ_TILE
    nsub = lt // st
    i, j, same = _block_masks(st, blk)
    incl = same & (i >= j)
    strict = same & (i > j)
    hi = lax.Precision.HIGHEST
    incl_f = incl.astype(F32)
    last_f = (j == (i | (blk - 1))).astype(F32)
    eye = (i == j).astype(F32)
    pair_masks = []
    for lvl in range(int(math.log2(blk))):
        pair_masks.append(((i >> (lvl + 1)) == (j >> (lvl + 1))) & ((i >> lvl) != (j >> lvl)))

    heads = range(nh)
    hp = []
    for s in range(nsub):
        rows = slice(s * st, (s + 1) * st)
        ba = ba_ref[rows, :]
        beta_all = jax.nn.sigmoid(ba)
        g_all = -jnp.exp(gp_ref[0:1, :]) * jax.nn.softplus(ba + gp_ref[1:2, :])
        gc_all = _dot(incl_f, g_all, hi)
        gl_all = _dot(last_f, gc_all, hi)
        gc_t = gc_all.T
        for h in heads:
            q = act_sc[rows, h * hd:(h + 1) * hd]
            k = act_sc[rows, (nh + h) * hd:(nh + h + 1) * hd]
            v = act_sc[rows, (2 * nh + h) * hd:(2 * nh + h + 1) * hd]
            q = q * lax.rsqrt(jnp.sum(q * q, axis=-1, keepdims=True) + RMS_EPS) * dk_scale
            k = k * lax.rsqrt(jnp.sum(k * k, axis=-1, keepdims=True) + RMS_EPS)
            beta = beta_all[:, h:h + 1]
            gcc = gc_all[:, nh + h:nh + h + 1]
            gcr = gc_t[nh + h:nh + h + 1, :]
            glc = gl_all[:, nh + h:nh + h + 1]
            decay = jnp.exp(jnp.where(incl, gcc - gcr, -jnp.inf))
            kb = k * beta
            k16 = k.astype(BF16)
            egc = jnp.exp(gcc)
            hp.append(dict(
                sub=s, head=h, decay=decay, k16=k16, kb16=kb.astype(BF16), q16=q.astype(BF16), glc=glc,
                rhs16=jnp.concatenate([v * beta, kb * egc], axis=1).astype(BF16),
                qd=q * egc, kd=k * jnp.exp(glc - gcc)))

    ms = [_dot_nt(p["kb16"], p["k16"]) * jnp.where(strict, p["decay"], 0.0) for p in hp]
    dinvs = [eye - jnp.where(pair_masks[0], m, 0.0) for m in ms]
    for pm in pair_masks[1:]:
        d16s = [d.astype(BF16) for d in dinvs]
        t16s = [_dot(d16, jnp.where(pm, m, 0.0).astype(BF16)).astype(BF16) for d16, m in zip(d16s, ms)]
        dinvs = [d - _dot(t16, d16) for d, t16, d16 in zip(dinvs, t16s, d16s)]
    xs = [_dot(d.astype(BF16), p["rhs16"]) for d, p in zip(dinvs, hp)]
    qk16s = [(_dot_nt(p["q16"], p["k16"]) * p["decay"]).astype(BF16) for p in hp]

    blk_per_sub = st // blk
    vn_parts = [[] for _ in hp]
    os_parts = [[] for _ in hp]
    for b in range(lt // blk):
        lb = b % blk_per_sub
        rs = slice(lb * blk, (lb + 1) * blk)
        cidx = [(b // blk_per_sub) * nh + h for h in heads]
        s0s = [st_ref[0, h] if chained else sin_ref[b, h] for h in heads]
        rr = [_dot(jnp.concatenate([xs[c][rs, hd:], hp[c]["qd"][rs]], axis=0).astype(BF16), s0.astype(BF16))
              for c, s0 in zip(cidx, s0s)]
        vns = [xs[c][rs, :hd] - r[:blk] for c, r in zip(cidx, rr)]
        for h, c in zip(heads, cidx):
            p = hp[c]
            s_new = (s0s[h] * jnp.exp(p["glc"][lb * blk:lb * blk + 1, :])
                     + _dot_tn(p["kd"][rs].astype(BF16), vns[h].astype(BF16)))
            if chained:
                st_ref[0, h] = s_new
            else:
                st_ref[b, h] = s_new
            vn_parts[c].append(vns[h])
            os_parts[c].append(rr[h][blk:])
    for c, p in enumerate(hp):
        rows = slice(p["sub"] * st, (p["sub"] + 1) * st)
        cols = slice(p["head"] * hd, (p["head"] + 1) * hd)
        vn_all = jnp.concatenate(vn_parts[c], axis=0)
        o = jnp.concatenate(os_parts[c], axis=0) + _dot(qk16s[c], vn_all.astype(BF16))
        o = o * lax.rsqrt(jnp.mean(o * o, axis=-1, keepdims=True) + RMS_EPS) * ng_ref[...]
        o = o * _silu(z_ref[rows, cols])
        o_ref[rows, cols] = o.astype(BF16)


def _gdn(proj, conv_w, gate_par, norm_g, *, n_seq, n_rows, blk, ba_col, conv_buf=None, s0=None):
    t_rows = proj.shape[0]
    nh = DN_HEADS
    hd = LANES
    cqkv = 3 * nh * hd
    chained = s0 is None
    lt = GDN_TILE_CHAINED if chained else SEQ_TILE
    assert lt % SEQ_TILE == 0 and SEQ_TILE % blk == 0 and blk & (blk - 1) == 0 and blk >= 2
    n_tiles = t_rows // lt
    common_in = [
        pl.BlockSpec((lt, cqkv), lambda t: (t, 0)),
        pl.BlockSpec((lt, nh * hd), lambda t: (t, 3)),
        pl.BlockSpec((lt, LANES), lambda t: (t, ba_col)),
    ]
    par_in = [
        pl.BlockSpec(conv_w.shape, lambda t: (0, 0)),
        pl.BlockSpec(gate_par.shape, lambda t: (0, 0)),
        pl.BlockSpec((1, hd), lambda t: (0, 0)),
    ]
    o_spec = pl.BlockSpec((lt, nh * hd), lambda t: (t, 0))
    o_shape = jax.ShapeDtypeStruct((t_rows, nh * hd), BF16)
    scratch, extra_specs, extra_shapes = [], [], []
    if chained:
        assert n_rows % lt == 0 and blk == DN_CHUNK
        tps = n_rows // lt
        in_specs = common_in + par_in
        args = (proj, proj, proj, conv_w, gate_par, norm_g)
        st_spec = pl.BlockSpec((1, nh, hd, hd), lambda t: (t // tps, 0, 0, 0))
        scratch.append(pltpu.VMEM((lt + SUBLANES, cqkv), F32))
    else:
        assert blk == n_rows and conv_buf is not None and conv_buf.shape[1:] == (n_seq, cqkv)
        tps = 0
        spt = lt // n_rows
        buf_spec = pl.BlockSpec((conv_buf.shape[0], spt, cqkv), lambda t: (0, t, 0))
        in_specs = common_in + [
            buf_spec,
            pl.BlockSpec((spt, nh, hd, hd), lambda t: (t, 0, 0, 0)),
        ] + par_in
        args = (proj, proj, proj, conv_buf, s0, conv_w, gate_par, norm_g)
        st_spec = pl.BlockSpec((spt, nh, hd, hd), lambda t: (t, 0, 0, 0))
        extra_specs.append(buf_spec)
        extra_shapes.append(jax.ShapeDtypeStruct(conv_buf.shape, F32))
        scratch.append(pltpu.VMEM((lt, cqkv), F32))
    scratch.append(pltpu.VMEM((lt, cqkv), F32))
    return pl.pallas_call(
        functools.partial(_gdn_body, blk=blk, chained=chained, tiles_per_seq=tps,
                          dk_scale=float(hd) ** -0.5),
        name="gdn_chained" if chained else "gdn_stateful",
        grid=(n_tiles,),
        in_specs=in_specs,
        out_specs=[o_spec, st_spec] + extra_specs,
        out_shape=[o_shape, jax.ShapeDtypeStruct((n_seq, nh, hd, hd), F32)] + extra_shapes,
        scratch_shapes=scratch,
        compiler_params=pltpu.CompilerParams(
            dimension_semantics=("arbitrary",), vmem_limit_bytes=56 * MIB),
    )(*args)


def _mlp_body(u_ref, v_ref, ws_ref, bst_ref, lng_ref, lnb_ref, o_ref, *maybe_vr_ref, blk):
    ck = ws_ref.shape[1]
    gd = u_ref.shape[1] // MLP_GROUPS
    i, j, same = _block_masks(ck, blk)
    incl = same & (i >= j)
    hi = lax.Precision.HIGHEST
    if blk != ck:
        rep = (j == (i & (blk - 1))).astype(F32)
        bias_all = _dot(rep, bst_ref[...], hi)
    else:
        bias_all = bst_ref[...]
    for g in range(MLP_GROUPS):
        cs = slice(g * gd, (g + 1) * gd)
        wsp = ws_ref[g]
        if blk != ck:
            wsp = _dot_nt(_dot(rep, wsp, hi), rep, hi)
        wsp = jnp.where(incl, wsp, 0.0).astype(BF16)
        for c in range(u_ref.shape[0] // ck):
            rs = slice(c * ck, (c + 1) * ck)
            uu = jax.nn.gelu(u_ref[rs, cs])
            vv = _layer_norm(jax.nn.gelu(v_ref[rs, cs]), lng_ref[:, cs], lnb_ref[:, cs])
            if maybe_vr_ref:
                maybe_vr_ref[0][rs, cs] = vv
            s = _dot(wsp, vv.astype(BF16)) + bias_all[:, g:g + 1]
            o_ref[rs, cs] = (uu * s).astype(BF16)


def _mlp(proj, w_spatial, b_spatial, ln_g, ln_b, *, blk, want_v_rows):
    t_rows = proj.shape[0]
    lt = MLP_TILE
    ck = w_spatial.shape[1]
    width = ln_g.size
    assert ck == MLP_CHUNK and lt % ck == 0 and ck % blk == 0 and t_rows % lt == 0
    u_col = (4 * DN_HEADS * LANES) // width
    bst = jnp.zeros((ck, LANES), F32).at[:, :MLP_GROUPS].set(b_spatial.T)
    o_spec = pl.BlockSpec((lt, width), lambda t: (t, 0))
    out_specs = [o_spec]
    out_shape = [jax.ShapeDtypeStruct((t_rows, width), BF16)]
    if want_v_rows:
        out_specs.append(o_spec)
        out_shape.append(jax.ShapeDtypeStruct((t_rows, width), F32))
    return pl.pallas_call(
        functools.partial(_mlp_body, blk=blk),
        name="spatial_mlp",
        grid=(t_rows // lt,),
        in_specs=[
            pl.BlockSpec((lt, width), lambda t: (t, u_col)),
            pl.BlockSpec((lt, width), lambda t: (t, u_col + 1)),
            pl.BlockSpec(w_spatial.shape, lambda t: (0, 0, 0)),
            pl.BlockSpec((ck, LANES), lambda t: (0, 0)),
            pl.BlockSpec((1, width), lambda t: (0, 0)),
            pl.BlockSpec((1, width), lambda t: (0, 0)),
        ],
        out_specs=out_specs,
        out_shape=out_shape,
        compiler_params=pltpu.CompilerParams(
            dimension_semantics=("arbitrary",), vmem_limit_bytes=32 * MIB),
    )(proj, proj, w_spatial, bst, ln_g.reshape(1, width), ln_b.reshape(1, width))


def _mix_body(od_ref, om_ref, x_ref, gt_ref, w_ref, lng_ref, lnb_ref, o_ref, w_sc, *, seq_tiles, alpha):
    tm, d = o_ref.shape
    kd = od_ref.shape[1]

    @pl.when(pl.program_id(0) == 0)
    def _():
        w_sc[...] = w_ref[...].astype(BF16)

    sb, rb, _ = x_ref.shape
    rc = min(FFN_RC, tm)
    for c in range(tm // rc):
        rs = slice(c * rc, (c + 1) * rc)
        mix = _dot(od_ref[rs, :], w_sc[:kd, :]) + _dot(om_ref[rs, :], w_sc[kd:, :])
        if sb == 1:
            x = x_ref[:, rs, :]
            gate = _mod_rows(gt_ref, seq_tiles)[:, None, :]
        else:
            ss = slice(c * (rc // rb), (c + 1) * (rc // rb))
            x = x_ref[ss]
            gate = gt_ref[ss, :][:, None, :]
        y = alpha * x + gate * mix.reshape(x.shape)
        o_ref[rs, :] = _layer_norm(y.reshape(rc, d), lng_ref[...], lnb_ref[...])


def _mix(o_dn, o_mlp, x3, mod, sub, w_out, layer, ln_g, ln_b, alpha):
    n_seq, n_rows, d = x3.shape
    grp = _Group(n_seq, n_rows, MIX_TM)
    tm = grp.tm
    out = pl.pallas_call(
        functools.partial(_mix_body, seq_tiles=grp.seq_tiles, alpha=alpha),
        name="out_mix_ln",
        grid=(grp.n_tiles,),
        in_specs=[
            pl.BlockSpec((tm, o_dn.shape[1]), lambda m: (m, 0)),
            pl.BlockSpec((tm, o_mlp.shape[1]), lambda m: (m, 0)),
            grp.x_spec(d, 1),
            grp.mod_spec(mod, d, sub * 3 + 2, 1),
            pl.BlockSpec((None,) + w_out.shape[1:], lambda m: (layer, 0, 0), pipeline_mode=pl.Buffered(1)),
            pl.BlockSpec((1, d), lambda m: (0, 0)),
            pl.BlockSpec((1, d), lambda m: (0, 0)),
        ],
        out_specs=pl.BlockSpec((tm, d), lambda m: (m, 0)),
        out_shape=jax.ShapeDtypeStruct((n_seq * n_rows, d), F32),
        scratch_shapes=[pltpu.VMEM(w_out.shape[1:], BF16)],
        compiler_params=pltpu.CompilerParams(
            dimension_semantics=("arbitrary",), vmem_limit_bytes=56 * MIB),
    )(o_dn, o_mlp, x3, mod, w_out, ln_g.reshape(1, d), ln_b.reshape(1, d))
    return out.reshape(n_seq, n_rows, d)


def _trunk_layer(x3, mod, wts, layer, alpha, conv_buf, s0, ffn_weights, emit_w16):
    n_seq, n_rows, d = x3.shape
    qkv_dim = 3 * DN_HEADS * LANES
    w16 = []

    def ffn(x3, sub, slot):
        r = _ffn(x3, mod, sub, ffn_weights[slot], wts["ln_g"][sub], wts["ln_b"][sub], alpha, emit_w16)
        if emit_w16:
            w16.append(r[1])
            return r[0]
        return r

    x3 = ffn(x3, 0, 0)
    proj = _proj(x3, mod, 1, wts["w_cat"])
    dn_blk = DN_CHUNK if n_rows % DN_CHUNK == 0 else n_rows
    gdn_kw = dict(n_seq=n_seq, n_rows=n_rows, blk=dn_blk, ba_col=wts["ba_col"])
    if s0 is None:
        o_dn, s_new = _gdn(proj, wts["conv_w"], wts["gate_par"], wts["dn_norm_g"], **gdn_kw)
        new_buf = proj.reshape(n_seq, n_rows, -1)[:, n_rows - (CONV_W - 1):, :qkv_dim]
    else:
        assert n_rows == SUBLANES
        o_dn, s_new, nbuf = _gdn(proj, wts["conv_w"], wts["gate_par"], wts["dn_norm_g"],
                                 conv_buf=jnp.swapaxes(conv_buf, 0, 1), s0=s0, **gdn_kw)
        new_buf = jnp.swapaxes(nbuf, 0, 1)
    mlp_out = _mlp(proj, wts["w_spatial"], wts["b_spatial"], wts["mlp_ln_g"], wts["mlp_ln_b"],
                   blk=min(n_rows, MLP_CHUNK), want_v_rows=s0 is not None)
    o_mlp = mlp_out[0]
    v_rows = mlp_out[1].reshape(n_seq, n_rows, -1) if s0 is not None else None
    x3 = _mix(o_dn, o_mlp, x3, mod, 1, wts["w_out"], layer, wts["ln_g"][1], wts["ln_b"][1], alpha)
    x3 = ffn(x3, 2, 1)
    return x3, new_buf, s_new, v_rows, w16


def kernel(x_prompt, x_sample, c_prompt, c_sample, state_delta, state_conv, w_ada, b_ada, ln_g, ln_b,
           ffn_wg, ffn_wu, ffn_wd, w_in, conv_w, a_log, dt_bias, dn_norm_g, mlp_ln_g, mlp_ln_b,
           w_spatial, b_spatial, w_out):
    depth = w_ada.shape[0]
    bp, _, d = x_prompt.shape
    bs = x_sample.shape[0]
    nh = DN_HEADS
    alpha = (2.0 * depth) ** 0.25
    qkvz = 4 * nh * LANES
    gates = 2 * nh

    pad_rows = (-(bs + bp)) % SUBLANES
    c_all = jnp.concatenate([c_sample, c_prompt, jnp.zeros((pad_rows, d), F32)], axis=0)
    w_in_t = jnp.swapaxes(w_in, 1, 2)

    y_p, y_s = x_prompt, x_sample
    delta_p, conv_p, delta_s, conv_s, vrows_s = [], [], [], [], []
    for layer in range(depth):
        mod_s, mod_p = _ada(c_all, w_ada[layer], b_ada[layer], bs)
        gate_par = jnp.zeros((SUBLANES, LANES), F32)
        gate_par = gate_par.at[0, nh:gates].set(a_log[layer]).at[1, nh:gates].set(dt_bias[layer])
        wts = dict(
            ln_g=ln_g[layer], ln_b=ln_b[layer],
            w_cat=_wprep(w_in_t, layer, qkvz, gates), ba_col=(w_in.shape[-1] - gates) // LANES,
            conv_w=conv_w[layer], gate_par=gate_par,
            dn_norm_g=dn_norm_g[layer].reshape(1, LANES), mlp_ln_g=mlp_ln_g[layer], mlp_ln_b=mlp_ln_b[layer],
            w_spatial=w_spatial[layer], b_spatial=b_spatial[layer], w_out=w_out,
        )
        w32 = [(ffn_wg, ffn_wu, ffn_wd, layer, slot) for slot in range(2)]
        y_s, cb_s, ds_s, vr_s, w16 = _trunk_layer(y_s, mod_s, wts, layer, alpha, state_conv[layer],
                                                  state_delta[layer], w32, True)
        y_p, cb_p, ds_p, _, _ = _trunk_layer(y_p, mod_p, wts, layer, alpha, None, None, w16, False)
        delta_p.append(ds_p)
        conv_p.append(cb_p)
        delta_s.append(ds_s)
        conv_s.append(cb_s)
        vrows_s.append(vr_s)
    return (y_p, y_s, jnp.stack(delta_p), jnp.stack(conv_p), jnp.stack(delta_s), jnp.stack(conv_s),
            jnp.stack(vrows_s))
```

```python
import functools
import math

import jax
import jax.numpy as jnp
from jax import lax
from jax.experimental import pallas as pl
from jax.experimental.pallas import tpu as pltpu

F32 = jnp.float32
BF16 = jnp.bfloat16

DN_HEADS = 8
DN_CHUNK = 64
CONV_W = 4
MLP_GROUPS = 4
MLP_CHUNK = 128
LN_EPS = 1e-5
RMS_EPS = 1e-6

LANES = 128
SUBLANES = 8
MXU_COLS = 256
MIB = 2 ** 20

FFN_TM = 1024
FFN_TF = 256
FFN_TF16 = 512
FFN_RC = 256
FFN_VMEM_LIMIT = 58 * MIB
PROJ_TN = 1280
PROJ_N = 6400
PREP_TC = 256
MIX_TM = 512
ADA_TN = 1024
SEQ_TILE = 128
GDN_TILE_CHAINED = 256
MLP_TILE = 512


def _dot(a, b, precision=None):
    return jnp.dot(a, b, preferred_element_type=F32, precision=precision)


def _dot_nt(a, b, precision=None):
    return lax.dot_general(a, b, (((1,), (1,)), ((), ())),
                           preferred_element_type=F32, precision=precision)


def _dot_tn(a, b):
    return lax.dot_general(a, b, (((0,), (0,)), ((), ())), preferred_element_type=F32)


def _layer_norm(y, g, b):
    mu = jnp.mean(y, axis=-1, keepdims=True)
    yc = y - mu
    var = jnp.mean(yc * yc, axis=-1, keepdims=True)
    return yc * lax.rsqrt(var + LN_EPS) * g + b


def _silu(x):
    return x * jax.nn.sigmoid(x)


def _mod_rows(ref, seq_tiles):
    if seq_tiles:
        return ref[pl.ds(pl.program_id(0) // seq_tiles, 1), :]
    return ref[...]


def _ada_body(c_ref, w_ref, b_ref, ms_ref, mp_ref):
    c = c_ref[...]
    a = _silu(c).astype(BF16)
    y = _dot(a, w_ref[...].astype(BF16)) + b_ref[...]
    ns = ms_ref.shape[0]
    ms_ref[...] = y[:ns]
    mp_ref[...] = y[ns:]


def _ada(c_all, w_ada, b_ada, n_sample):
    rows, d = c_all.shape
    n = w_ada.shape[1]
    return pl.pallas_call(
        _ada_body,
        name="ada_mod",
        grid=(n // ADA_TN,),
        in_specs=[
            pl.BlockSpec((rows, d), lambda j: (0, 0)),
            pl.BlockSpec((d, ADA_TN), lambda j: (0, j)),
            pl.BlockSpec((1, ADA_TN), lambda j: (0, j)),
        ],
        out_specs=[
            pl.BlockSpec((n_sample, ADA_TN), lambda j: (0, j)),
            pl.BlockSpec((rows - n_sample, ADA_TN), lambda j: (0, j)),
        ],
        out_shape=[
            jax.ShapeDtypeStruct((n_sample, n), F32),
            jax.ShapeDtypeStruct((rows - n_sample, n), F32),
        ],
        compiler_params=pltpu.CompilerParams(
            dimension_semantics=("arbitrary",), vmem_limit_bytes=40 * MIB),
    )(c_all, w_ada, b_ada.reshape(1, n))


class _Group:
    def __init__(self, n_seq, n_rows, tm):
        if n_rows >= tm:
            assert n_rows % tm == 0
            self.sb, self.rb = 1, tm
        else:
            assert tm % n_rows == 0 and n_seq % (tm // n_rows) == 0
            self.sb, self.rb = tm // n_rows, n_rows
        self.n_seq, self.n_rows = n_seq, n_rows
        self.tiles_r = n_rows // self.rb
        self.tm = self.sb * self.rb
        self.n_tiles = (n_seq // self.sb) * self.tiles_r
        self.seq_tiles = self.tiles_r if self.sb == 1 else 0

    def x_spec(self, d, ngrid, **kw):
        tr = self.tiles_r
        if ngrid == 2:
            return pl.BlockSpec((self.sb, self.rb, d), lambda m, f: (m // tr, m % tr, 0), **kw)
        return pl.BlockSpec((self.sb, self.rb, d), lambda m: (m // tr, m % tr, 0), **kw)

    def mod_spec(self, mod, d, col, ngrid):
        if self.sb == 1:
            rows, row_blk = mod.shape[0], (lambda m: 0)
        else:
            rows, row_blk = self.sb, (lambda m: m)
        if ngrid == 2:
            return pl.BlockSpec((rows, d), lambda m, f: (row_blk(m), col))
        return pl.BlockSpec((rows, d), lambda m: (row_blk(m), col))


def _ffn_body(x_ref, sh_ref, sc_ref, gt_ref, wg_ref, wu_ref, wd_ref, lng_ref, lnb_ref,
              o_ref, *rest, seq_tiles, alpha):
    w16_refs, h_sc = rest[:-1], rest[-1]
    f = pl.program_id(1)
    nf = pl.num_programs(1)
    tm, d = o_ref.shape
    sb, rb, _ = x_ref.shape
    rc = min(FFN_RC, tm)
    n_chunks = tm // rc
    if w16_refs:
        for src, dst in zip((wg_ref, wu_ref, wd_ref), w16_refs):
            dst[...] = src[...].astype(BF16)
        wg_ref, wu_ref, wd_ref = w16_refs

    def x_chunk(c):
        if sb == 1:
            return x_ref[:, c * rc:(c + 1) * rc, :]
        return x_ref[c * (rc // rb):(c + 1) * (rc // rb)]

    def mod_chunk(ref, c):
        if seq_tiles:
            return _mod_rows(ref, seq_tiles)
        return ref[c * (rc // rb):(c + 1) * (rc // rb), :]

    def weights():
        return wg_ref[...].astype(BF16), wu_ref[...].astype(BF16), wd_ref[...].astype(BF16)

    def gate_up(h16, wg16, wu16):
        return (_silu(_dot(h16, wg16)) * _dot(h16, wu16)).astype(BF16)

    @pl.when(f == 0)
    def _():
        wg16, wu16, wd16 = weights()
        for c in range(n_chunks):
            rs = slice(c * rc, (c + 1) * rc)
            x = x_chunk(c)
            h = x * (1.0 + mod_chunk(sc_ref, c)[:, None, :]) + mod_chunk(sh_ref, c)[:, None, :]
            h16 = h.reshape(rc, d).astype(BF16)
            h_sc[rs, :] = h16
            o_ref[rs, :] = _dot(gate_up(h16, wg16, wu16), wd16)

    @pl.when(jnp.logical_and(f > 0, f < nf - 1))
    def _():
        wg16, wu16, wd16 = weights()
        h16 = h_sc[...]
        half = wd16.shape[0] // 2
        if half % MXU_COLS:
            o_ref[...] += _dot(gate_up(h16, wg16, wu16), wd16)
        else:
            a0 = gate_up(h16, wg16[:, :half], wu16[:, :half])
            a1 = gate_up(h16, wg16[:, half:], wu16[:, half:])
            o_ref[...] += _dot(a0, wd16[:half, :]) + _dot(a1, wd16[half:, :])

    @pl.when(f == nf - 1)
    def _():
        wg16, wu16, wd16 = weights()
        for c in range(n_chunks):
            rs = slice(c * rc, (c + 1) * rc)
            acc = o_ref[rs, :] + _dot(gate_up(h_sc[rs, :], wg16, wu16), wd16)
            x = x_chunk(c)
            gate = mod_chunk(gt_ref, c)[:, None, :]
            y = alpha * x + (0.5 * gate) * acc.reshape(x.shape)
            o_ref[rs, :] = y.reshape(rc, d)
            ya = o_ref[rs, :]
            mu = jnp.mean(ya, axis=-1, keepdims=True)
            var = jnp.mean(jnp.square(ya - mu), axis=-1, keepdims=True)
            yb = o_ref[rs, :]
            o_ref[rs, :] = (yb - mu) * lax.rsqrt(var + LN_EPS) * lng_ref[...] + lnb_ref[...]


def _ffn(x3, mod, sub, weights, ln_g, ln_b, alpha, emit_w16=False):
    n_seq, n_rows, d = x3.shape
    grp = _Group(n_seq, n_rows, FFN_TM)
    tm = grp.tm
    rc = min(FFN_RC, tm)
    if len(weights) == 5:
        wg, wu, wd, layer, slot = weights
        tf = FFN_TF
        w_specs = [
            pl.BlockSpec((None, None, d, tf), lambda m, f: (layer, slot, 0, f)),
            pl.BlockSpec((None, None, d, tf), lambda m, f: (layer, slot, 0, f)),
            pl.BlockSpec((None, None, tf, d), lambda m, f: (layer, slot, f, 0)),
        ]
    else:
        assert not emit_w16
        wg, wu, wd = weights
        tf = FFN_TF16
        w_specs = [
            pl.BlockSpec((d, tf), lambda m, f: (0, f)),
            pl.BlockSpec((d, tf), lambda m, f: (0, f)),
            pl.BlockSpec((tf, d), lambda m, f: (f, 0)),
        ]
    dff = wg.shape[-1]
    assert dff // tf >= 2 and tm % rc == 0 and (grp.rb % rc == 0 if grp.sb == 1 else rc % grp.rb == 0)
    out_specs = [pl.BlockSpec((tm, d), lambda m, f: (m, 0))]
    out_shape = [jax.ShapeDtypeStruct((n_seq * n_rows, d), F32)]
    if emit_w16:
        assert grp.n_tiles == 1
        out_specs += [
            pl.BlockSpec((d, tf), lambda m, f: (0, f)),
            pl.BlockSpec((d, tf), lambda m, f: (0, f)),
            pl.BlockSpec((tf, d), lambda m, f: (f, 0)),
        ]
        out_shape += [
            jax.ShapeDtypeStruct((d, dff), BF16),
            jax.ShapeDtypeStruct((d, dff), BF16),
            jax.ShapeDtypeStruct((dff, d), BF16),
        ]
    outs = pl.pallas_call(
        functools.partial(_ffn_body, seq_tiles=grp.seq_tiles, alpha=alpha),
        name="swiglu_ln",
        grid=(grp.n_tiles, dff // tf),
        in_specs=[
            grp.x_spec(d, 2),
            grp.mod_spec(mod, d, sub * 3 + 0, 2),
            grp.mod_spec(mod, d, sub * 3 + 1, 2),
            grp.mod_spec(mod, d, sub * 3 + 2, 2),
            *w_specs,
            pl.BlockSpec((1, d), lambda m, f: (0, 0)),
            pl.BlockSpec((1, d), lambda m, f: (0, 0)),
        ],
        out_specs=out_specs,
        out_shape=out_shape,
        scratch_shapes=[pltpu.VMEM((tm, d), BF16)],
        compiler_params=pltpu.CompilerParams(
            dimension_semantics=("arbitrary", "arbitrary"), vmem_limit_bytes=FFN_VMEM_LIMIT),
    )(x3, mod, mod, mod, wg, wu, wd, ln_g.reshape(1, d), ln_b.reshape(1, d))
    y = outs[0].reshape(n_seq, n_rows, d)
    return (y, tuple(outs[1:])) if emit_w16 else y


def _wprep_body(w_ref, o_ref, *, qkvz, gates):
    n_in, cols = w_ref.shape
    n_out = o_ref.shape[0]
    mlp = n_in - qkvz - gates
    o_ref[0:qkvz, :] = w_ref[0:qkvz, :].astype(BF16)
    o_ref[qkvz:qkvz + mlp, :] = w_ref[qkvz + gates:n_in, :].astype(BF16)
    o_ref[qkvz + mlp:n_in, :] = w_ref[qkvz:qkvz + gates, :].astype(BF16)
    o_ref[n_in:n_out, :] = jnp.zeros((n_out - n_in, cols), BF16)


def _wprep(w_in_t, layer, qkvz, gates):
    _, n_in, d = w_in_t.shape
    assert n_in <= PROJ_N and (n_in - gates) % LANES == 0
    return pl.pallas_call(
        functools.partial(_wprep_body, qkvz=qkvz, gates=gates),
        name="proj_weight_prep",
        grid=(d // PREP_TC,),
        in_specs=[pl.BlockSpec((None, n_in, PREP_TC), lambda i: (layer, 0, i))],
        out_specs=pl.BlockSpec((PROJ_N, PREP_TC), lambda i: (0, i)),
        out_shape=jax.ShapeDtypeStruct((PROJ_N, d), BF16),
        compiler_params=pltpu.CompilerParams(
            dimension_semantics=("arbitrary",), vmem_limit_bytes=40 * MIB),
    )(w_in_t)


def _proj_body(x_ref, sh_ref, sc_ref, w_ref, o_ref, h_sc, *, seq_tiles):
    tm, d = h_sc.shape
    sb, rb, _ = x_ref.shape
    rc = min(FFN_RC, tm)
    j = pl.program_id(1)

    @pl.when(j == 0)
    def _():
        for c in range(tm // rc):
            rs = slice(c * rc, (c + 1) * rc)
            if sb == 1:
                x = x_ref[:, rs, :]
                sc, sh = _mod_rows(sc_ref, seq_tiles), _mod_rows(sh_ref, seq_tiles)
            else:
                ss = slice(c * (rc // rb), (c + 1) * (rc // rb))
                x = x_ref[ss]
                sc, sh = sc_ref[ss, :], sh_ref[ss, :]
            h16 = (x * (1.0 + sc[:, None, :]) + sh[:, None, :]).reshape(rc, d).astype(BF16)
            h_sc[rs, :] = h16
            o_ref[rs, :] = _dot_nt(h16, w_ref[...])

    @pl.when(j != 0)
    def _():
        o_ref[...] = _dot_nt(h_sc[...], w_ref[...])


def _proj(x3, mod, sub, w_cat_t):
    n_seq, n_rows, d = x3.shape
    n = w_cat_t.shape[0]
    grp = _Group(n_seq, n_rows, FFN_TM)
    tm = grp.tm
    return pl.pallas_call(
        functools.partial(_proj_body, seq_tiles=grp.seq_tiles),
        name="in_proj",
        grid=(grp.n_tiles, n // PROJ_TN),
        in_specs=[
            grp.x_spec(d, 2),
            grp.mod_spec(mod, d, sub * 3 + 0, 2),
            grp.mod_spec(mod, d, sub * 3 + 1, 2),
            pl.BlockSpec((PROJ_TN, d), lambda m, j: (j, 0)),
        ],
        out_specs=pl.BlockSpec((tm, PROJ_TN), lambda m, j: (m, j)),
        out_shape=jax.ShapeDtypeStruct((n_seq * n_rows, n), F32),
        scratch_shapes=[pltpu.VMEM((tm, d), BF16)],
        compiler_params=pltpu.CompilerParams(
            dimension_semantics=("arbitrary", "arbitrary"), vmem_limit_bytes=48 * MIB),
    )(x3, mod, mod, w_cat_t)


def _block_masks(n, blk):
    shift = int(math.log2(blk))
    i = lax.broadcasted_iota(jnp.int32, (n, n), 0)
    j = lax.broadcasted_iota(jnp.int32, (n, n), 1)
    same = (i >> shift) == (j >> shift)
    return i, j, same


def _gdn_body(*refs, blk, chained, tiles_per_seq, dk_scale):
    if chained:
        (qkv_ref, z_ref, ba_ref, cw_ref, gp_ref, ng_ref, o_ref, st_ref, xc_sc, act_sc) = refs
    else:
        (qkv_ref, z_ref, ba_ref, buf_ref, sin_ref, cw_ref, gp_ref, ng_ref, o_ref, st_ref, nbuf_ref,
         prev_sc, act_sc) = refs
    lt, cqkv = qkv_ref.shape
    hd = LANES
    nh = cqkv // (3 * hd)
    t = pl.program_id(0)
    halo = SUBLANES
    ngrp = lt // SUBLANES

    if chained:
        first = (t % tiles_per_seq) == 0

        @pl.when(first)
        def _():
            xc_sc[pl.ds(0, halo), :] = jnp.zeros((halo, cqkv), F32)
            st_ref[...] = jnp.zeros_like(st_ref)

        @pl.when(jnp.logical_not(first))
        def _():
            xc_sc[pl.ds(0, halo), :] = xc_sc[pl.ds(lt, halo), :]

        xc_sc[pl.ds(halo, lt), :] = qkv_ref[...]
    else:
        assert blk == SUBLANES
        keep = buf_ref.shape[0]

        @pl.when(t == 0)
        def _():
            prev_sc[...] = jnp.zeros_like(prev_sc)

        for s in range(ngrp):
            for r in range(keep):
                row = s * SUBLANES + SUBLANES - keep + r
                prev_sc[row:row + 1, :] = buf_ref[r, s:s + 1, :]
                nbuf_ref[r, s:s + 1, :] = qkv_ref[row:row + 1, :]

    cblk = 2 * LANES
    row_in_grp = lax.broadcasted_iota(jnp.int32, (ngrp, SUBLANES, cblk), 1)
    for c0 in range(0, cqkv, cblk):
        cs = slice(c0, c0 + cblk)
        if chained:
            xe = xc_sc[:, cs].reshape(ngrp + 1, SUBLANES, cblk)
            cur = xe[1:]
        else:
            cur = qkv_ref[:, cs].reshape(ngrp, SUBLANES, cblk)
            prev = prev_sc[:, cs].reshape(ngrp, SUBLANES, cblk)
        y = None
        for j in range(CONV_W):
            d = CONV_W - 1 - j
            if d == 0:
                term = cur
            elif chained:
                rot = pltpu.roll(xe, d, axis=1)
                term = jnp.where(row_in_grp >= d, rot[1:], rot[:-1])
            else:
                term = jnp.where(row_in_grp >= d, pltpu.roll(cur, d, axis=1), pltpu.roll(prev, d, axis=1))
            term = cw_ref[j:j + 1, cs][None] * term
            y = term if y is None else y + term
        act_sc[:, cs] = _silu(y).reshape(lt, cblk)

    st = SEQ_TILE
    nsub = lt // st
    i, j, same = _block_masks(st, blk)
    incl = same & (i >= j)
    strict = same & (i > j)
    hi = lax.Precision.HIGHEST
    incl_f = incl.astype(F32)
    last_f = (j == (i | (blk - 1))).astype(F32)
    eye = (i == j).astype(F32)
    pair_masks = []
    for lvl in range(int(math.log2(blk))):
        pair_masks.append(((i >> (lvl + 1)) == (j >> (lvl + 1))) & ((i >> lvl) != (j >> lvl)))

    heads = range(nh)
    hp = []
    for s in range(nsub):
        rows = slice(s * st, (s + 1) * st)
        ba = ba_ref[rows, :]
        beta_all = jax.nn.sigmoid(ba)
        g_all = -jnp.exp(gp_ref[0:1, :]) * jax.nn.softplus(ba + gp_ref[1:2, :])
        gc_all = _dot(incl_f, g_all, hi)
        gl_all = _dot(last_f, gc_all, hi)
        gc_t = gc_all.T
        for h in heads:
            q = act_sc[rows, h * hd:(h + 1) * hd]
            k = act_sc[rows, (nh + h) * hd:(nh + h + 1) * hd]
            v = act_sc[rows, (2 * nh + h) * hd:(2 * nh + h + 1) * hd]
            q = q * lax.rsqrt(jnp.sum(q * q, axis=-1, keepdims=True) + RMS_EPS) * dk_scale
            k = k * lax.rsqrt(jnp.sum(k * k, axis=-1, keepdims=True) + RMS_EPS)
            beta = beta_all[:, h:h + 1]
            gcc = gc_all[:, nh + h:nh + h + 1]
            gcr = gc_t[nh + h:nh + h + 1, :]
            glc = gl_all[:, nh + h:nh + h + 1]
            decay = jnp.exp(jnp.where(incl, gcc - gcr, -jnp.inf))
            kb = k * beta
            k16 = k.astype(BF16)
            egc = jnp.exp(gcc)
            hp.append(dict(
                sub=s, head=h, decay=decay, k16=k16, kb16=kb.astype(BF16), q16=q.astype(BF16), glc=glc,
                rhs16=jnp.concatenate([v * beta, kb * egc], axis=1).astype(BF16),
                qd=q * egc, kd=k * jnp.exp(glc - gcc)))

    ms = [_dot_nt(p["kb16"], p["k16"]) * jnp.where(strict, p["decay"], 0.0) for p in hp]
    dinvs = [eye - jnp.where(pair_masks[0], m, 0.0) for m in ms]
    for pm in pair_masks[1:]:
        d16s = [d.astype(BF16) for d in dinvs]
        t16s = [_dot(d16, jnp.where(pm, m, 0.0).astype(BF16)).astype(BF16) for d16, m in zip(d16s, ms)]
        dinvs = [d - _dot(t16, d16) for d, t16, d16 in zip(dinvs, t16s, d16s)]
    xs = [_dot(d.astype(BF16), p["rhs16"]) for d, p in zip(dinvs, hp)]
    qk16s = [(_dot_nt(p["q16"], p["k16"]) * p["decay"]).astype(BF16) for p in hp]

    blk_per_sub = st // blk
    vn_parts = [[] for _ in hp]
    os_parts = [[] for _ in hp]
    for b in range(lt // blk):
        lb = b % blk_per_sub
        rs = slice(lb * blk, (lb + 1) * blk)
        cidx = [(b // blk_per_sub) * nh + h for h in heads]
        s0s = [st_ref[0, h] if chained else sin_ref[b, h] for h in heads]
        rr = [_dot(jnp.concatenate([xs[c][rs, hd:], hp[c]["qd"][rs]], axis=0).astype(BF16), s0.astype(BF16))
              for c, s0 in zip(cidx, s0s)]
        vns = [xs[c][rs, :hd] - r[:blk] for c, r in zip(cidx, rr)]
        for h, c in zip(heads, cidx):
            p = hp[c]
            s_new = (s0s[h] * jnp.exp(p["glc"][lb * blk:lb * blk + 1, :])
                     + _dot_tn(p["kd"][rs].astype(BF16), vns[h].astype(BF16)))
            if chained:
                st_ref[0, h] = s_new
            else:
                st_ref[b, h] = s_new
            vn_parts[c].append(vns[h])
            os_parts[c].append(rr[h][blk:])
    for c, p in enumerate(hp):
        rows = slice(p["sub"] * st, (p["sub"] + 1) * st)
        cols = slice(p["head"] * hd, (p["head"] + 1) * hd)
        vn_all = jnp.concatenate(vn_parts[c], axis=0)
        o = jnp.concatenate(os_parts[c], axis=0) + _dot(qk16s[c], vn_all.astype(BF16))
        o = o * lax.rsqrt(jnp.mean(o * o, axis=-1, keepdims=True) + RMS_EPS) * ng_ref[...]
        o = o * _silu(z_ref[rows, cols])
        o_ref[rows, cols] = o.astype(BF16)


def _gdn(proj, conv_w, gate_par, norm_g, *, n_seq, n_rows, blk, ba_col, conv_buf=None, s0=None):
    t_rows = proj.shape[0]
    nh = DN_HEADS
    hd = LANES
    cqkv = 3 * nh * hd
    chained = s0 is None
    lt = GDN_TILE_CHAINED if chained else SEQ_TILE
    assert lt % SEQ_TILE == 0 and SEQ_TILE % blk == 0 and blk & (blk - 1) == 0 and blk >= 2
    n_tiles = t_rows // lt
    common_in = [
        pl.BlockSpec((lt, cqkv), lambda t: (t, 0)),
        pl.BlockSpec((lt, nh * hd), lambda t: (t, 3)),
        pl.BlockSpec((lt, LANES), lambda t: (t, ba_col)),
    ]
    par_in = [
        pl.BlockSpec(conv_w.shape, lambda t: (0, 0)),
        pl.BlockSpec(gate_par.shape, lambda t: (0, 0)),
        pl.BlockSpec((1, hd), lambda t: (0, 0)),
    ]
    o_spec = pl.BlockSpec((lt, nh * hd), lambda t: (t, 0))
    o_shape = jax.ShapeDtypeStruct((t_rows, nh * hd), BF16)
    scratch, extra_specs, extra_shapes = [], [], []
    if chained:
        assert n_rows % lt == 0 and blk == DN_CHUNK
        tps = n_rows // lt
        in_specs = common_in + par_in
        args = (proj, proj, proj, conv_w, gate_par, norm_g)
        st_spec = pl.BlockSpec((1, nh, hd, hd), lambda t: (t // tps, 0, 0, 0))
        scratch.append(pltpu.VMEM((lt + SUBLANES, cqkv), F32))
    else:
        assert blk == n_rows and conv_buf is not None and conv_buf.shape[1:] == (n_seq, cqkv)
        tps = 0
        spt = lt // n_rows
        buf_spec = pl.BlockSpec((conv_buf.shape[0], spt, cqkv), lambda t: (0, t, 0))
        in_specs = common_in + [
            buf_spec,
            pl.BlockSpec((spt, nh, hd, hd), lambda t: (t, 0, 0, 0)),
        ] + par_in
        args = (proj, proj, proj, conv_buf, s0, conv_w, gate_par, norm_g)
        st_spec = pl.BlockSpec((spt, nh, hd, hd), lambda t: (t, 0, 0, 0))
        extra_specs.append(buf_spec)
        extra_shapes.append(jax.ShapeDtypeStruct(conv_buf.shape, F32))
        scratch.append(pltpu.VMEM((lt, cqkv), F32))
    scratch.append(pltpu.VMEM((lt, cqkv), F32))
    return pl.pallas_call(
        functools.partial(_gdn_body, blk=blk, chained=chained, tiles_per_seq=tps,
                          dk_scale=float(hd) ** -0.5),
        name="gdn_chained" if chained else "gdn_stateful",
        grid=(n_tiles,),
        in_specs=in_specs,
        out_specs=[o_spec, st_spec] + extra_specs,
        out_shape=[o_shape, jax.ShapeDtypeStruct((n_seq, nh, hd, hd), F32)] + extra_shapes,
        scratch_shapes=scratch,
        compiler_params=pltpu.CompilerParams(
            dimension_semantics=("arbitrary",), vmem_limit_bytes=56 * MIB),
    )(*args)


def _mlp_body(u_ref, v_ref, ws_ref, bst_ref, lng_ref, lnb_ref, o_ref, *maybe_vr_ref, blk):
    ck = ws_ref.shape[1]
    gd = u_ref.shape[1] // MLP_GROUPS
    i, j, same = _block_masks(ck, blk)
    incl = same & (i >= j)
    hi = lax.Precision.HIGHEST
    if blk != ck:
        rep = (j == (i & (blk - 1))).astype(F32)
        bias_all = _dot(rep, bst_ref[...], hi)
    else:
        bias_all = bst_ref[...]
    for g in range(MLP_GROUPS):
        cs = slice(g * gd, (g + 1) * gd)
        wsp = ws_ref[g]
        if blk != ck:
            wsp = _dot_nt(_dot(rep, wsp, hi), rep, hi)
        wsp = jnp.where(incl, wsp, 0.0).astype(BF16)
        for c in range(u_ref.shape[0] // ck):
            rs = slice(c * ck, (c + 1) * ck)
            uu = jax.nn.gelu(u_ref[rs, cs])
            vv = _layer_norm(jax.nn.gelu(v_ref[rs, cs]), lng_ref[:, cs], lnb_ref[:, cs])
            if maybe_vr_ref:
                maybe_vr_ref[0][rs, cs] = vv
            s = _dot(wsp, vv.astype(BF16)) + bias_all[:, g:g + 1]
            o_ref[rs, cs] = (uu * s).astype(BF16)


def _mlp(proj, w_spatial, b_spatial, ln_g, ln_b, *, blk, want_v_rows):
    t_rows = proj.shape[0]
    lt = MLP_TILE
    ck = w_spatial.shape[1]
    width = ln_g.size
    assert ck == MLP_CHUNK and lt % ck == 0 and ck % blk == 0 and t_rows % lt == 0
    u_col = (4 * DN_HEADS * LANES) // width
    bst = jnp.zeros((ck, LANES), F32).at[:, :MLP_GROUPS].set(b_spatial.T)
    o_spec = pl.BlockSpec((lt, width), lambda t: (t, 0))
    out_specs = [o_spec]
    out_shape = [jax.ShapeDtypeStruct((t_rows, width), BF16)]
    if want_v_rows:
        out_specs.append(o_spec)
        out_shape.append(jax.ShapeDtypeStruct((t_rows, width), F32))
    return pl.pallas_call(
        functools.partial(_mlp_body, blk=blk),
        name="spatial_mlp",
        grid=(t_rows // lt,),
        in_specs=[
            pl.BlockSpec((lt, width), lambda t: (t, u_col)),
            pl.BlockSpec((lt, width), lambda t: (t, u_col + 1)),
            pl.BlockSpec(w_spatial.shape, lambda t: (0, 0, 0)),
            pl.BlockSpec((ck, LANES), lambda t: (0, 0)),
            pl.BlockSpec((1, width), lambda t: (0, 0)),
            pl.BlockSpec((1, width), lambda t: (0, 0)),
        ],
        out_specs=out_specs,
        out_shape=out_shape,
        compiler_params=pltpu.CompilerParams(
            dimension_semantics=("arbitrary",), vmem_limit_bytes=32 * MIB),
    )(proj, proj, w_spatial, bst, ln_g.reshape(1, width), ln_b.reshape(1, width))


def _mix_body(od_ref, om_ref, x_ref, gt_ref, w_ref, lng_ref, lnb_ref, o_ref, w_sc, *, seq_tiles, alpha):
    tm, d = o_ref.shape
    kd = od_ref.shape[1]

    @pl.when(pl.program_id(0) == 0)
    def _():
        w_sc[...] = w_ref[...].astype(BF16)

    sb, rb, _ = x_ref.shape
    rc = min(FFN_RC, tm)
    for c in range(tm // rc):
        rs = slice(c * rc, (c + 1) * rc)
        mix = _dot(od_ref[rs, :], w_sc[:kd, :]) + _dot(om_ref[rs, :], w_sc[kd:, :])
        if sb == 1:
            x = x_ref[:, rs, :]
            gate = _mod_rows(gt_ref, seq_tiles)[:, None, :]
        else:
            ss = slice(c * (rc // rb), (c + 1) * (rc // rb))
            x = x_ref[ss]
            gate = gt_ref[ss, :][:, None, :]
        y = alpha * x + gate * mix.reshape(x.shape)
        o_ref[rs, :] = _layer_norm(y.reshape(rc, d), lng_ref[...], lnb_ref[...])


def _mix(o_dn, o_mlp, x3, mod, sub, w_out, layer, ln_g, ln_b, alpha):
    n_seq, n_rows, d = x3.shape
    grp = _Group(n_seq, n_rows, MIX_TM)
    tm = grp.tm
    out = pl.pallas_call(
        functools.partial(_mix_body, seq_tiles=grp.seq_tiles, alpha=alpha),
        name="out_mix_ln",
        grid=(grp.n_tiles,),
        in_specs=[
            pl.BlockSpec((tm, o_dn.shape[1]), lambda m: (m, 0)),
            pl.BlockSpec((tm, o_mlp.shape[1]), lambda m: (m, 0)),
            grp.x_spec(d, 1),
            grp.mod_spec(mod, d, sub * 3 + 2, 1),
            pl.BlockSpec((None,) + w_out.shape[1:], lambda m: (layer, 0, 0), pipeline_mode=pl.Buffered(1)),
            pl.BlockSpec((1, d), lambda m: (0, 0)),
            pl.BlockSpec((1, d), lambda m: (0, 0)),
        ],
        out_specs=pl.BlockSpec((tm, d), lambda m: (m, 0)),
        out_shape=jax.ShapeDtypeStruct((n_seq * n_rows, d), F32),
        scratch_shapes=[pltpu.VMEM(w_out.shape[1:], BF16)],
        compiler_params=pltpu.CompilerParams(
            dimension_semantics=("arbitrary",), vmem_limit_bytes=56 * MIB),
    )(o_dn, o_mlp, x3, mod, w_out, ln_g.reshape(1, d), ln_b.reshape(1, d))
    return out.reshape(n_seq, n_rows, d)


def _trunk_layer(x3, mod, wts, layer, alpha, conv_buf, s0, ffn_weights, emit_w16):
    n_seq, n_rows, d = x3.shape
    qkv_dim = 3 * DN_HEADS * LANES
    w16 = []

    def ffn(x3, sub, slot):
        r = _ffn(x3, mod, sub, ffn_weights[slot], wts["ln_g"][sub], wts["ln_b"][sub], alpha, emit_w16)
        if emit_w16:
            w16.append(r[1])
            return r[0]
        return r

    x3 = ffn(x3, 0, 0)
    proj = _proj(x3, mod, 1, wts["w_cat"])
    dn_blk = DN_CHUNK if n_rows % DN_CHUNK == 0 else n_rows
    gdn_kw = dict(n_seq=n_seq, n_rows=n_rows, blk=dn_blk, ba_col=wts["ba_col"])
    if s0 is None:
        o_dn, s_new = _gdn(proj, wts["conv_w"], wts["gate_par"], wts["dn_norm_g"], **gdn_kw)
        new_buf = proj.reshape(n_seq, n_rows, -1)[:, n_rows - (CONV_W - 1):, :qkv_dim]
    else:
        assert n_rows == SUBLANES
        o_dn, s_new, nbuf = _gdn(proj, wts["conv_w"], wts["gate_par"], wts["dn_norm_g"],
                                 conv_buf=jnp.swapaxes(conv_buf, 0, 1), s0=s0, **gdn_kw)
        new_buf = jnp.swapaxes(nbuf, 0, 1)
    mlp_out = _mlp(proj, wts["w_spatial"], wts["b_spatial"], wts["mlp_ln_g"], wts["mlp_ln_b"],
                   blk=min(n_rows, MLP_CHUNK), want_v_rows=s0 is not None)
    o_mlp = mlp_out[0]
    v_rows = mlp_out[1].reshape(n_seq, n_rows, -1) if s0 is not None else None
    x3 = _mix(o_dn, o_mlp, x3, mod, 1, wts["w_out"], layer, wts["ln_g"][1], wts["ln_b"][1], alpha)
    x3 = ffn(x3, 2, 1)
    return x3, new_buf, s_new, v_rows, w16


def kernel(x_prompt, x_sample, c_prompt, c_sample, state_delta, state_conv, w_ada, b_ada, ln_g, ln_b,
           ffn_wg, ffn_wu, ffn_wd, w_in, conv_w, a_log, dt_bias, dn_norm_g, mlp_ln_g, mlp_ln_b,
           w_spatial, b_spatial, w_out):
    depth = w_ada.shape[0]
    bp, _, d = x_prompt.shape
    bs = x_sample.shape[0]
    nh = DN_HEADS
    alpha = (2.0 * depth) ** 0.25
    qkvz = 4 * nh * LANES
    gates = 2 * nh

    pad_rows = (-(bs + bp)) % SUBLANES
    c_all = jnp.concatenate([c_sample, c_prompt, jnp.zeros((pad_rows, d), F32)], axis=0)
    w_in_t = jnp.swapaxes(w_in, 1, 2)

    y_p, y_s = x_prompt, x_sample
    delta_p, conv_p, delta_s, conv_s, vrows_s = [], [], [], [], []
    for layer in range(depth):
        mod_s, mod_p = _ada(c_all, w_ada[layer], b_ada[layer], bs)
        gate_par = jnp.zeros((SUBLANES, LANES), F32)
        gate_par = gate_par.at[0, nh:gates].set(a_log[layer]).at[1, nh:gates].set(dt_bias[layer])
        wts = dict(
            ln_g=ln_g[layer], ln_b=ln_b[layer],
            w_cat=_wprep(w_in_t, layer, qkvz, gates), ba_col=(w_in.shape[-1] - gates) // LANES,
            conv_w=conv_w[layer], gate_par=gate_par,
            dn_norm_g=dn_norm_g[layer].reshape(1, LANES), mlp_ln_g=mlp_ln_g[layer], mlp_ln_b=mlp_ln_b[layer],
            w_spatial=w_spatial[layer], b_spatial=b_spatial[layer], w_out=w_out,
        )
        w32 = [(ffn_wg, ffn_wu, ffn_wd, layer, slot) for slot in range(2)]
        y_s, cb_s, ds_s, vr_s, w16 = _trunk_layer(y_s, mod_s, wts, layer, alpha, state_conv[layer],
                                                  state_delta[layer], w32, True)
        y_p, cb_p, ds_p, _, _ = _trunk_layer(y_p, mod_p, wts, layer, alpha, None, None, w16, False)
        delta_p.append(ds_p)
        conv_p.append(cb_p)
        delta_s.append(ds_s)
        conv_s.append(cb_s)
        vrows_s.append(vr_s)
    return (y_p, y_s, jnp.stack(delta_p), jnp.stack(conv_p), jnp.stack(delta_s), jnp.stack(conv_s),
            jnp.stack(vrows_s))
```
